```python
import jax
import jax.numpy as jnp
from jax import lax
import numpy as np

D_MODEL = 2048
BATCH = 4
SEQ = 4096
DEPTH = 2
DEC_BATCH = 8
DEC_SEQ = 4096
PAST_LEN = 128

GRID_W = 64
EPS = 1e-6
N_GROUPS = 4
GROUP_W = D_MODEL // N_GROUPS

ATT_HD = 64
ATT_HQ = GROUP_W // ATT_HD
ATT_HKV = max(1, ATT_HQ // 4)
Q_BLOCK = 128
ROPE_BASE = 10000.0

LRU_W = GROUP_W
LRU_BD = 64
LRU_BLOCKS = LRU_W // LRU_BD
LRU_C = 8.0
CONV_W = 4
CONV_LEFT = 2

RET_HD = 128
RET_H = GROUP_W // RET_HD
RET_CHUNK = 128

RWKV_HD = 64
RWKV_H = GROUP_W // RWKV_HD
RWKV_DECAY_LORA = 64
RWKV_A_LORA = 64
RWKV_G_LORA = 128
RWKV_LN_EPS = 64e-5

D_FF = ((8 * D_MODEL + 3 * 256 - 1) // (3 * 256)) * 256

A_Q = ATT_HQ * ATT_HD
A_KV = ATT_HKV * ATT_HD
A_COLS = A_Q + 2 * A_KV
B_COLS = 2 * LRU_W
C_COLS = 4 * GROUP_W
D_MAIN = 3 * GROUP_W
D_COLS = D_MAIN + 2 * RWKV_DECAY_LORA + RWKV_A_LORA + RWKV_G_LORA
OFF_B = A_COLS
OFF_C = OFF_B + B_COLS
OFF_D = OFF_C + C_COLS
IN_COLS = OFF_D + D_COLS

kernel_name = "hybrid_parallel_head_group_encoder"

F32 = jnp.float32


def rms_norm(x, g, eps=EPS):
    xf = x.astype(F32)
    y = xf * lax.rsqrt(jnp.mean(xf * xf, axis=-1, keepdims=True) + eps)
    return (y * g.astype(F32)).astype(x.dtype)


def grid_positions(n_tokens):
    n_rows = n_tokens // GRID_W
    rows = jnp.repeat(jnp.arange(n_rows, dtype=jnp.int32), GRID_W)
    cols = jnp.tile(jnp.arange(GRID_W, dtype=jnp.int32), n_rows)
    return rows, cols


def rope_1d(x, pos):
    d = x.shape[-1]
    inv = 1.0 / (ROPE_BASE ** (jnp.arange(0, d, 2, dtype=F32) / d))
    ang = pos.astype(F32)[:, None] * inv[None, :]
    cos = jnp.cos(ang)[None, :, None, :]
    sin = jnp.sin(ang)[None, :, None, :]
    x1, x2 = x[..., : d // 2], x[..., d // 2:]
    return jnp.concatenate([x1 * cos - x2 * sin, x1 * sin + x2 * cos], axis=-1)


def rope_2d(x, rows, cols):
    half = x.shape[-1] // 2
    return jnp.concatenate([rope_1d(x[..., :half], rows), rope_1d(x[..., half:], cols)], axis=-1)


def attention_group(z, q_gain, k_gain, out_gain, rows, cols):
    bsz, n_tok, _ = z.shape
    q = z[..., :A_Q].reshape(bsz, n_tok, ATT_HQ, ATT_HD)
    k = z[..., A_Q:A_Q + A_KV].reshape(bsz, n_tok, ATT_HKV, ATT_HD)
    v = z[..., A_Q + A_KV:].reshape(bsz, n_tok, ATT_HKV, ATT_HD).astype(F32)
    q = rope_2d(rms_norm(q, q_gain).astype(F32), rows, cols) * (ATT_HD ** -0.5)
    k = rope_2d(rms_norm(k, k_gain).astype(F32), rows, cols)
    n_rep = ATT_HQ // ATT_HKV
    q_blocks = q.reshape(bsz, n_tok // Q_BLOCK, Q_BLOCK, ATT_HKV, n_rep, ATT_HD).transpose(1, 0, 2, 3, 4, 5)

    def one_block(qb):
        s = jnp.einsum("bqhgd,bkhd->bhgqk", qb, k)
        p = jax.nn.softmax(s, axis=-1)
        return jnp.einsum("bhgqk,bkhd->bqhgd", p, v)

    o = lax.map(one_block, q_blocks)
    o = o.transpose(1, 0, 2, 3, 4, 5).reshape(bsz, n_tok, A_Q)
    return rms_norm(o, out_gain).astype(z.dtype)


def depthwise_conv_centred(x, w, b):
    n_tok = x.shape[1]
    xp = jnp.pad(x, ((0, 0), (CONV_LEFT, CONV_W - 1 - CONV_LEFT), (0, 0)))
    out = b
    for j in range(CONV_W):
        out = out + xp[:, j:j + n_tok] * w[j]
    return out


def _linear_combine(e1, e2):
    a1, b1 = e1
    a2, b2 = e2
    return a1 * a2, a2 * b1 + b2


def rglru_direction(x, wa, ba, wx, bx, lam, reverse):
    bsz, n_tok, ch = x.shape
    xb = x.reshape(bsz, n_tok, LRU_BLOCKS, LRU_BD)
    r = jax.nn.sigmoid(jnp.einsum("bsni,nij->bsnj", xb, wa).reshape(bsz, n_tok, ch) + ba)
    i = jax.nn.sigmoid(jnp.einsum("bsni,nij->bsnj", xb, wx).reshape(bsz, n_tok, ch) + bx)
    log_a = -LRU_C * r * jax.nn.softplus(-lam)
    a = jnp.exp(log_a)
    b = jnp.sqrt(-jnp.expm1(2.0 * log_a)) * (i * x)
    _, h = lax.associative_scan(_linear_combine, (a, b), axis=1, reverse=reverse)
    return h


def rglru_group(z, conv_w, conv_b, wa, ba, wx, bx, lam, out_gain):
    xr = z[..., :LRU_W].astype(F32)
    gate = z[..., LRU_W:].astype(F32)
    xc = depthwise_conv_centred(xr, conv_w.astype(F32), conv_b.astype(F32))
    h = (rglru_direction(xc, wa[0].astype(F32), ba[0].astype(F32), wx[0].astype(F32),
                         bx[0].astype(F32), lam[0].astype(F32), False)
         + rglru_direction(xc, wa[1].astype(F32), ba[1].astype(F32), wx[1].astype(F32),
                           bx[1].astype(F32), lam[1].astype(F32), True))
    y = h * jax.nn.gelu(gate)
    return rms_norm(y, out_gain).astype(z.dtype)


def retention_causal_chunkwise(q, k, v, log_g, include_diag):
    bsz, n_tok, n_h, d = q.shape
    n_ch = n_tok // RET_CHUNK
    qc = q.reshape(bsz, n_ch, RET_CHUNK, n_h, d)
    kc = k.reshape(bsz, n_ch, RET_CHUNK, n_h, d)
    vc = v.reshape(bsz, n_ch, RET_CHUNK, n_h, v.shape[-1])
    idx = jnp.arange(RET_CHUNK, dtype=F32)
    diff = idx[:, None] - idx[None, :]
    mask = (diff >= 0) if include_diag else (diff > 0)
    decay = jnp.where(mask[None], jnp.exp(jnp.where(mask, diff, 0.0)[None] * log_g[:, None, None]), 0.0)
    s = jnp.einsum("bnihd,bnjhd->bnhij", qc, kc) * decay[None, None]
    o_intra = jnp.einsum("bnhij,bnjhe->bnihe", s, vc)
    k_w = jnp.exp((RET_CHUNK - 1 - idx)[None, :] * log_g[:, None])
    kv = jnp.einsum("bnjhd,bnjhe,hj->nbhde", kc, vc, k_w)
    g_chunk = jnp.exp(RET_CHUNK * log_g)[None, :, None, None]

    def step(state, kv_n):
        return g_chunk * state + kv_n, state

    init = jnp.zeros((bsz, n_h, d, v.shape[-1]), F32)
    _, s_prev = lax.scan(step, init, kv)
    q_w = jnp.exp((idx + 1.0)[None, :] * log_g[:, None])
    o_cross = jnp.einsum("bnihd,nbhde,hi->bnihe", qc, s_prev, q_w)
    return (o_intra + o_cross).reshape(bsz, n_tok, n_h, v.shape[-1])


def retention_group(z, norm_gain, rows, cols):
    bsz, n_tok, _ = z.shape
    q = z[..., :GROUP_W].reshape(bsz, n_tok, RET_H, RET_HD).astype(F32)
    k = z[..., GROUP_W:2 * GROUP_W].reshape(bsz, n_tok, RET_H, RET_HD).astype(F32)
    v = z[..., 2 * GROUP_W:3 * GROUP_W].reshape(bsz, n_tok, RET_H, RET_HD).astype(F32)
    gate = z[..., 3 * GROUP_W:].astype(F32)
    q = rope_2d(q, rows, cols)
    k = rope_2d(k, rows, cols) * (RET_HD ** -0.5)
    log_g = jnp.log1p(-jnp.exp2(-5.0 - jnp.arange(RET_H, dtype=F32)))
    fwd = retention_causal_chunkwise(q, k, v, log_g, True)
    bwd = jnp.flip(retention_causal_chunkwise(jnp.flip(q, 1), jnp.flip(k, 1), jnp.flip(v, 1), log_g, False), 1)
    o = rms_norm(fwd + bwd, norm_gain.reshape(RET_H, RET_HD))
    o = o.reshape(bsz, n_tok, GROUP_W) * jax.nn.silu(gate)
    return o.astype(z.dtype)


def token_shift_centred(x):
    prev = jnp.pad(x[:, :-1], ((0, 0), (1, 0), (0, 0)))
    nxt = jnp.pad(x[:, 1:], ((0, 0), (0, 1), (0, 0)))
    return 0.5 * (prev + nxt)


def rwkv7_scan(r, w, k, v, kk, b, include_current, reverse):
    bsz, _, n_h, n = r.shape
    xs = tuple(jnp.moveaxis(t, 1, 0) for t in (r, w, k, v, kk, b))

    def step(state, inp):
        r_t, w_t, k_t, v_t, kk_t, b_t = inp
        sa = jnp.einsum("bhvk,bhk->bhv", state, kk_t)
        new = (state * w_t[:, :, None, :] - sa[..., None] * b_t[:, :, None, :]
               + v_t[..., None] * k_t[:, :, None, :])
        read = new if include_current else state
        return new, jnp.einsum("bhvk,bhk->bhv", read, r_t)

    init = jnp.zeros((bsz, n_h, n, n), F32)
    _, ys = lax.scan(step, init, xs, reverse=reverse)
    return jnp.moveaxis(ys, 0, 1)


def rwkv7_group(z, mu, w0, w2, a0, a2, g2, k_k, k_a, r_k, ln_w, ln_b):
    bsz, n_tok, _ = z.shape
    zf = z.astype(F32)
    zf = zf + (token_shift_centred(zf) - zf) * mu.astype(F32)
    r = zf[..., :GROUP_W]
    k = zf[..., GROUP_W:2 * GROUP_W]
    v = zf[..., 2 * GROUP_W:3 * GROUP_W]
    o = D_MAIN
    wd_f = zf[..., o:o + RWKV_DECAY_LORA]
    wd_b = zf[..., o + RWKV_DECAY_LORA:o + 2 * RWKV_DECAY_LORA]
    o = o + 2 * RWKV_DECAY_LORA
    ad = zf[..., o:o + RWKV_A_LORA]
    gd = zf[..., o + RWKV_A_LORA:]

    def decay(wd, w0_, w2_):
        wl = -jax.nn.softplus(-(w0_.astype(F32) + jnp.tanh(wd) @ w2_.astype(F32))) - 0.5
        return jnp.exp(-jnp.exp(wl))

    w_f = decay(wd_f, w0[0], w2[0])
    w_b = decay(wd_b, w0[1], w2[1])
    a = jax.nn.sigmoid(a0.astype(F32) + ad @ a2.astype(F32))
    g = jax.nn.sigmoid(gd) @ g2.astype(F32)
    heads = lambda t: t.reshape(bsz, n_tok, RWKV_H, RWKV_HD)
    kk = heads(k * k_k.astype(F32))
    kk = kk * lax.rsqrt(jnp.sum(kk * kk, axis=-1, keepdims=True) + 1e-12)
    k = heads(k * (1.0 + (a - 1.0) * k_a.astype(F32)))
    r, v, a, w_f, w_b = heads(r), heads(v), heads(a), heads(w_f), heads(w_b)
    b = kk * a
    y = rwkv7_scan(r, w_f, k, v, kk, b, True, False) + rwkv7_scan(r, w_b, k, v, kk, b, False, True)
    y = y + jnp.sum(r * k * r_k.astype(F32), axis=-1, keepdims=True) * v
    mean = jnp.mean(y, axis=-1, keepdims=True)
    var = jnp.mean(jnp.square(y - mean), axis=-1, keepdims=True)
    y = ((y - mean) * lax.rsqrt(var + RWKV_LN_EPS) * ln_w.astype(F32).reshape(RWKV_H, RWKV_HD)
         + ln_b.astype(F32).reshape(RWKV_H, RWKV_HD))
    y = y.reshape(bsz, n_tok, GROUP_W) * g
    return y.astype(z.dtype)


def hybrid_layer(x, rows, cols, p):
    h = rms_norm(x, p["norm_mix"])
    z = h @ p["w_in"]
    ya = attention_group(z[..., :OFF_B], p["attn_q_norm"], p["attn_k_norm"], p["attn_out_norm"], rows, cols)
    yb = rglru_group(z[..., OFF_B:OFF_C], p["lru_conv_w"], p["lru_conv_b"], p["lru_wa"], p["lru_ba"],
                     p["lru_wx"], p["lru_bx"], p["lru_lambda"], p["lru_out_norm"])
    yc = retention_group(z[..., OFF_C:OFF_D], p["ret_norm"], rows, cols)
    yd = rwkv7_group(z[..., OFF_D:], p["rwkv_mu"], p["rwkv_w0"], p["rwkv_w2"], p["rwkv_a0"], p["rwkv_a2"],
                     p["rwkv_g2"], p["rwkv_k_k"], p["rwkv_k_a"], p["rwkv_r_k"], p["rwkv_ln_w"], p["rwkv_ln_b"])
    y = jnp.concatenate([ya, yb, yc, yd], axis=-1) @ p["w_out"]
    x = x + y.astype(x.dtype)
    h = rms_norm(x, p["norm_ffn"])
    f = (jax.nn.silu(h @ p["w_gate"]) * (h @ p["w_up"])) @ p["w_down"]
    return x + f.astype(x.dtype)


def run_trunk(x, params):
    rows, cols = grid_positions(x.shape[1])
    for layer in range(DEPTH):
        x = hybrid_layer(x, rows, cols, {name: arr[layer] for name, arr in params.items()})
    return x


def setup_inputs(seed: int = 0) -> dict:
    key = jax.random.key(seed)
    keys = iter(jax.random.split(key, 40))
    nrm = lambda shape, scale: scale * jax.random.normal(next(keys), shape, F32)
    unif = lambda shape, lo, hi: jax.random.uniform(next(keys), shape, F32, lo, hi)
    L = DEPTH
    res_scale = (2.0 * DEPTH) ** -0.5
    x_prompt = nrm((BATCH, SEQ, D_MODEL), 1.0)
    x_sample = nrm((DEC_BATCH, DEC_SEQ, D_MODEL), 1.0)
    norm_mix = 1.0 + nrm((L, D_MODEL), 0.02)
    w_in = nrm((L, D_MODEL, IN_COLS), D_MODEL ** -0.5)
    attn_q_norm = 1.0 + nrm((L, ATT_HD), 0.02)
    attn_k_norm = 1.0 + nrm((L, ATT_HD), 0.02)
    attn_out_norm = 1.0 + nrm((L, GROUP_W), 0.02)
    lru_conv_w = nrm((L, CONV_W, LRU_W), 0.5)
    lru_conv_b = nrm((L, LRU_W), 0.02)
    lru_wa = nrm((L, 2, LRU_BLOCKS, LRU_BD, LRU_BD), LRU_BD ** -0.5)
    lru_ba = nrm((L, 2, LRU_W), 0.1)
    lru_wx = nrm((L, 2, LRU_BLOCKS, LRU_BD, LRU_BD), LRU_BD ** -0.5)
    lru_bx = nrm((L, 2, LRU_W), 0.1)
    lru_a = unif((L, 2, LRU_W), 0.9, 0.999) ** (1.0 / LRU_C)
    lru_lambda = jnp.log(lru_a) - jnp.log1p(-lru_a)
    lru_out_norm = 1.0 + nrm((L, GROUP_W), 0.02)
    ret_norm = 1.0 + nrm((L, GROUP_W), 0.02)
    rwkv_mu = unif((L, D_COLS), 0.0, 1.0)
    rwkv_w0 = unif((L, 2, GROUP_W), -6.5, -1.5)
    rwkv_w2 = nrm((L, 2, RWKV_DECAY_LORA, GROUP_W), 0.1 * RWKV_DECAY_LORA ** -0.5)
    rwkv_a0 = nrm((L, GROUP_W), 0.1)
    rwkv_a2 = nrm((L, RWKV_A_LORA, GROUP_W), 0.3 * RWKV_A_LORA ** -0.5)
    rwkv_g2 = nrm((L, RWKV_G_LORA, GROUP_W), RWKV_G_LORA ** -0.5)
    rwkv_k_k = 0.85 + nrm((L, GROUP_W), 0.02)
    rwkv_k_a = 1.0 + nrm((L, GROUP_W), 0.02)
    rwkv_r_k = nrm((L, RWKV_H, RWKV_HD), 0.1)
    rwkv_ln_w = 1.0 + nrm((L, GROUP_W), 0.02)
    rwkv_ln_b = nrm((L, GROUP_W), 0.02)
    w_out = nrm((L, D_MODEL, D_MODEL), res_scale * D_MODEL ** -0.5)
    norm_ffn = 1.0 + nrm((L, D_MODEL), 0.02)
    w_gate = nrm((L, D_MODEL, D_FF), D_MODEL ** -0.5)
    w_up = nrm((L, D_MODEL, D_FF), D_MODEL ** -0.5)
    w_down = nrm((L, D_FF, D_MODEL), res_scale * D_FF ** -0.5)
    return {
        "x_prompt": x_prompt, "x_sample": x_sample,
        "norm_mix": norm_mix, "w_in": w_in,
        "attn_q_norm": attn_q_norm, "attn_k_norm": attn_k_norm, "attn_out_norm": attn_out_norm,
        "lru_conv_w": lru_conv_w, "lru_conv_b": lru_conv_b, "lru_wa": lru_wa, "lru_ba": lru_ba,
        "lru_wx": lru_wx, "lru_bx": lru_bx, "lru_lambda": lru_lambda, "lru_out_norm": lru_out_norm,
        "ret_norm": ret_norm,
        "rwkv_mu": rwkv_mu, "rwkv_w0": rwkv_w0, "rwkv_w2": rwkv_w2, "rwkv_a0": rwkv_a0,
        "rwkv_a2": rwkv_a2, "rwkv_g2": rwkv_g2, "rwkv_k_k": rwkv_k_k, "rwkv_k_a": rwkv_k_a,
        "rwkv_r_k": rwkv_r_k, "rwkv_ln_w": rwkv_ln_w, "rwkv_ln_b": rwkv_ln_b,
        "w_out": w_out, "norm_ffn": norm_ffn, "w_gate": w_gate, "w_up": w_up, "w_down": w_down,
    }


def reference(x_prompt, x_sample, norm_mix, w_in, attn_q_norm, attn_k_norm, attn_out_norm,
              lru_conv_w, lru_conv_b, lru_wa, lru_ba, lru_wx, lru_bx, lru_lambda, lru_out_norm,
              ret_norm, rwkv_mu, rwkv_w0, rwkv_w2, rwkv_a0, rwkv_a2, rwkv_g2, rwkv_k_k, rwkv_k_a,
              rwkv_r_k, rwkv_ln_w, rwkv_ln_b, w_out, norm_ffn, w_gate, w_up, w_down):
    params = {
        "norm_mix": norm_mix, "w_in": w_in,
        "attn_q_norm": attn_q_norm, "attn_k_norm": attn_k_norm, "attn_out_norm": attn_out_norm,
        "lru_conv_w": lru_conv_w, "lru_conv_b": lru_conv_b, "lru_wa": lru_wa, "lru_ba": lru_ba,
        "lru_wx": lru_wx, "lru_bx": lru_bx, "lru_lambda": lru_lambda, "lru_out_norm": lru_out_norm,
        "ret_norm": ret_norm,
        "rwkv_mu": rwkv_mu, "rwkv_w0": rwkv_w0, "rwkv_w2": rwkv_w2, "rwkv_a0": rwkv_a0,
        "rwkv_a2": rwkv_a2, "rwkv_g2": rwkv_g2, "rwkv_k_k": rwkv_k_k, "rwkv_k_a": rwkv_k_a,
        "rwkv_r_k": rwkv_r_k, "rwkv_ln_w": rwkv_ln_w, "rwkv_ln_b": rwkv_ln_b,
        "w_out": w_out, "norm_ffn": norm_ffn, "w_gate": w_gate, "w_up": w_up, "w_down": w_down,
    }
    y_prompt = run_trunk(x_prompt, params)
    y_sample = run_trunk(x_sample, params)
    return (y_prompt, y_sample)
```

```python
import functools

import jax
import jax.numpy as jnp
from jax import lax
from jax.experimental import pallas as pl
from jax.experimental.pallas import tpu as pltpu

F32 = jnp.float32
BF16 = jnp.bfloat16

D_MODEL = 2048
GRID_W = 64
EPS = 1e-6
GROUP_W = 512
ATT_HD = 64
ATT_HQ = 8
ATT_HKV = 2
ROPE_BASE = 10000.0
LRU_BD = 64
LRU_C = 8.0
CONV_W = 4
CONV_LEFT = 2
RET_HD = 128
RET_H = 4
RWKV_HD = 64
RWKV_LN_EPS = 64e-5
D_FF = 5632

A_COLS = 768
B_COLS = 1024
C_COLS = 2048
D_MAIN = 1536
OFF_B = A_COLS
OFF_C = OFF_B + B_COLS
OFF_D = OFF_C + C_COLS
D_COLS_PAD = 1920
Z_COLS = OFF_D + D_COLS_PAD

LANE = 128
VMEM_LIMIT = 56 * 1024 * 1024

RET_CHUNK = 256
RWKV_CHUNK = 64


def _cparams(sem):
    return pltpu.CompilerParams(dimension_semantics=sem, vmem_limit_bytes=VMEM_LIMIT)


def _dot(a, b):
    return jnp.dot(a.astype(BF16), b.astype(BF16), preferred_element_type=F32)


def _dot_nt(a, b):
    return lax.dot_general(a.astype(BF16), b.astype(BF16), (((1,), (1,)), ((), ())),
                           preferred_element_type=F32)


def _split(x):
    hi = x.astype(BF16)
    lo = (x - hi.astype(F32)).astype(BF16)
    return hi, lo


def _dot3(x, wh, wl):
    xh, xl = _split(x)
    return (jnp.dot(xh, wh, preferred_element_type=F32)
            + jnp.dot(xl, wh, preferred_element_type=F32)
            + jnp.dot(xh, wl, preferred_element_type=F32))


def _seg_sum(x, bd):
    xh, xl = _split(x)
    return jnp.dot(xh, bd, preferred_element_type=F32) + jnp.dot(xl, bd, preferred_element_type=F32)


def _rope(x, cos, sin_signed, half):
    n = x.shape[-1]
    lane = lax.broadcasted_iota(jnp.int32, x.shape, 1)
    first = (lane % (2 * half)) < half
    rot = jnp.where(first, pltpu.roll(x, n - half, 1), pltpu.roll(x, half, 1))
    return x * cos + rot * sin_signed


def _softplus(x):
    return jnp.maximum(x, 0.0) + jnp.log1p(jnp.exp(-jnp.abs(x)))


def _sigmoid(x):
    return 1.0 / (1.0 + jnp.exp(-x))


def _inproj_kernel(x_ref, g_ref, w_ref, o_ref, h_ref):
    @pl.when(pl.program_id(1) == 0)
    def _():
        x = x_ref[...]
        ms = jnp.mean(x * x, axis=-1, keepdims=True)
        h_ref[...] = (x * lax.rsqrt(ms + EPS) * g_ref[...]).astype(BF16)

    o_ref[...] = jnp.dot(h_ref[...], w_ref[...], preferred_element_type=F32)


def _inproj(x2d, gain, w):
    t, d = x2d.shape
    n = w.shape[1]
    tm = min(1024, t)
    tn = 640
    return pl.pallas_call(
        _inproj_kernel,
        name="inproj",
        grid=(t // tm, n // tn),
        in_specs=[pl.BlockSpec((tm, d), lambda i, j: (i, 0)),
                  pl.BlockSpec((1, d), lambda i, j: (0, 0)),
                  pl.BlockSpec((d, tn), lambda i, j: (0, j))],
        out_specs=pl.BlockSpec((tm, tn), lambda i, j: (i, j)),
        out_shape=jax.ShapeDtypeStruct((t, n), F32),
        scratch_shapes=[pltpu.VMEM((tm, d), BF16)],
        compiler_params=_cparams(("parallel", "arbitrary")),
    )(x2d, gain, w)


def _attn_kernel(q_ref, kv_ref, cq_ref, sq_ref, ck_ref, sk_ref, gq_ref, gk_ref, bdq_ref, bdk_ref,
                 o_ref, k_s, v_s):
    @pl.when(pl.program_id(1) == 0)
    def _():
        kv = kv_ref[0]
        k = kv[:, :LANE]
        ms = _seg_sum(k * k, bdk_ref[...]) * (1.0 / ATT_HD)
        k = k * lax.rsqrt(ms + EPS) * gk_ref[...]
        k = _rope(k, ck_ref[...], sk_ref[...], ATT_HD // 4)
        for g in range(ATT_HKV):
            k_s[g] = k[:, g * ATT_HD:(g + 1) * ATT_HD].astype(BF16)
            v_s[g] = kv[:, LANE + g * ATT_HD:LANE + (g + 1) * ATT_HD].astype(BF16)

    q = q_ref[0]
    ms = _seg_sum(q * q, bdq_ref[...]) * (1.0 / ATT_HD)
    q = q * lax.rsqrt(ms + EPS) * gq_ref[...]
    q = _rope(q, cq_ref[...], sq_ref[...], ATT_HD // 4).astype(BF16)
    n_rep = ATT_HQ // ATT_HKV
    for h in range(ATT_HQ):
        g = h // n_rep
        s = _dot_nt(q[:, h * ATT_HD:(h + 1) * ATT_HD], k_s[g])
        m = jnp.max(s, axis=-1, keepdims=True)
        p = jnp.exp(s - m)
        l = jnp.sum(p, axis=-1, keepdims=True)
        o = jnp.dot(p.astype(BF16), v_s[g], preferred_element_type=F32)
        o_ref[0, :, h * ATT_HD:(h + 1) * ATT_HD] = o / l


def _attention(z, tabs, gq, gk, bdq, bdk):
    b, s, _ = z.shape
    tq = min(256, s)
    cq, sq, ck, sk = tabs
    return pl.pallas_call(
        _attn_kernel,
        name="attn",
        grid=(b, s // tq),
        in_specs=[pl.BlockSpec((1, tq, 512), lambda i, j: (i, j, 0)),
                  pl.BlockSpec((1, s, 256), lambda i, j: (i, 0, 2)),
                  pl.BlockSpec((tq, 512), lambda i, j: (j, 0)),
                  pl.BlockSpec((tq, 512), lambda i, j: (j, 0)),
                  pl.BlockSpec((s, LANE), lambda i, j: (0, 0)),
                  pl.BlockSpec((s, LANE), lambda i, j: (0, 0)),
                  pl.BlockSpec((1, 512), lambda i, j: (0, 0)),
                  pl.BlockSpec((1, LANE), lambda i, j: (0, 0)),
                  pl.BlockSpec((512, 512), lambda i, j: (0, 0)),
                  pl.BlockSpec((LANE, LANE), lambda i, j: (0, 0))],
        out_specs=pl.BlockSpec((1, tq, 512), lambda i, j: (i, j, 0)),
        out_shape=jax.ShapeDtypeStruct((b, s, GROUP_W), F32),
        scratch_shapes=[pltpu.VMEM((ATT_HKV, s, ATT_HD), BF16),
                        pltpu.VMEM((ATT_HKV, s, ATT_HD), BF16)],
        compiler_params=_cparams(("parallel", "arbitrary")),
    )(z, z, cq, sq, ck, sk, gq, gk, bdq, bdk)


def _shift_rows(v, d, fill, row):
    n = v.shape[0]
    if d > 0:
        return jnp.where(row >= d, pltpu.roll(v, d, 0), fill)
    return jnp.where(row < n + d, pltpu.roll(v, n + d, 0), fill)


def _linear_scan(a, b, row, reverse):
    n = a.shape[0]
    d = 1
    while d < n:
        sd = -d if reverse else d
        b = a * _shift_rows(b, sd, 0.0, row) + b
        if 2 * d < n:
            a = a * _shift_rows(a, sd, 1.0, row)
        d *= 2
    return b


def _lru_kernel(x_ref, gate_ref, cw_ref, cb_ref, wah_ref, wal_ref, wxh_ref, wxl_ref,
                ba_ref, bx_ref, lam_ref, o_ref):
    x = x_ref[0]
    row = lax.broadcasted_iota(jnp.int32, x.shape, 0)
    xc = cb_ref[...]
    for j in range(CONV_W):
        d = CONV_LEFT - j
        xs = x if d == 0 else _shift_rows(x, d, 0.0, row)
        xc = xc + xs * cw_ref[j:j + 1, :]
    h = None
    for d in range(2):
        r = _sigmoid(_dot3(xc, wah_ref[d, 0], wal_ref[d, 0]) + ba_ref[d:d + 1, :])
        i = _sigmoid(_dot3(xc, wxh_ref[d, 0], wxl_ref[d, 0]) + bx_ref[d:d + 1, :])
        log_a = -LRU_C * r * _softplus(-lam_ref[d:d + 1, :])
        a = jnp.exp(log_a)
        bb = jnp.sqrt(-jnp.tanh(log_a) * (a * a + 1.0)) * (i * xc)
        hd = _linear_scan(a, bb, row, reverse=(d == 1))
        h = hd if h is None else h + hd
    gate = gate_ref[0]
    gelu = 0.5 * gate * (1.0 + jnp.tanh(0.7978845608028654 * (gate + 0.044715 * gate * gate * gate)))
    o_ref[0] = h * gelu


def _rglru(z, cw, cb, wah, wal, wxh, wxl, ba, bx, lam):
    b, s, _ = z.shape
    nc = GROUP_W // LANE
    xo = OFF_B // LANE
    go = (OFF_B + GROUP_W) // LANE
    vec = lambda r: pl.BlockSpec((r, LANE), lambda i, c: (0, c))
    wsp = pl.BlockSpec((2, 1, LANE, LANE), lambda i, c: (0, c, 0, 0))
    return pl.pallas_call(
        _lru_kernel,
        name="rglru",
        grid=(b, nc),
        in_specs=[pl.BlockSpec((1, s, LANE), lambda i, c: (i, 0, xo + c)),
                  pl.BlockSpec((1, s, LANE), lambda i, c: (i, 0, go + c)),
                  vec(CONV_W), vec(1), wsp, wsp, wsp, wsp, vec(2), vec(2), vec(2)],
        out_specs=pl.BlockSpec((1, s, LANE), lambda i, c: (i, 0, c)),
        out_shape=jax.ShapeDtypeStruct((b, s, GROUP_W), F32),
        compiler_params=_cparams(("parallel", "parallel")),
    )(z, z, cw, cb, wah, wal, wxh, wxl, ba, bx, lam)


def _ret_kernel(q_ref, k_ref, v_ref, g_ref, cos_ref, sin_ref, lg_ref, gain_ref, o_ref,
                q_s, k_s, v_s, acc, st):
    s_len = q_ref.shape[1]
    c = min(RET_CHUNK, s_len)
    nch = s_len // c
    half = RET_HD // 4
    q_s[...] = _rope(q_ref[0], cos_ref[...], sin_ref[...], half).astype(BF16)
    k_s[...] = (_rope(k_ref[0], cos_ref[...], sin_ref[...], half) * (RET_HD ** -0.5)).astype(BF16)
    v_s[...] = v_ref[0].astype(BF16)

    lg = lg_ref[0]
    ii = lax.broadcasted_iota(jnp.int32, (c, c), 0)
    jj = lax.broadcasted_iota(jnp.int32, (c, c), 1)
    gam = jnp.exp(jnp.abs(ii - jj).astype(F32) * lg[:, :1])
    ri = lax.broadcasted_iota(jnp.int32, (c, LANE), 0).astype(F32)
    qw_f = jnp.exp((ri + 1.0) * lg)
    qw_b = jnp.exp((c - ri) * lg)
    kw_f = jnp.exp((c - 1.0 - ri) * lg)
    kw_b = jnp.exp(ri * lg)
    g_chunk = jnp.exp(c * lg)
    gain = gain_ref[...]

    def kv_inc(kc, vc, kw):
        return jnp.dot((kc.astype(F32) * kw).T.astype(BF16), vc, preferred_element_type=F32)

    st[...] = jnp.zeros_like(st)

    def fwd_body(n, carry):
        sl = pl.ds(pl.multiple_of(n * c, c), c)
        qc, kc, vc = q_s[sl, :], k_s[sl, :], v_s[sl, :]
        sc = _dot_nt(qc, kc) * gam
        o = jnp.dot(sc.astype(BF16), vc, preferred_element_type=F32)
        o = o + jnp.dot(qc, st[...].astype(BF16), preferred_element_type=F32) * qw_f
        acc[sl, :] = o
        st[...] = g_chunk * st[...] + kv_inc(kc, vc, kw_f)
        return carry

    lax.fori_loop(0, nch, fwd_body, 0)
    st[...] = jnp.zeros_like(st)

    def bwd_body(m, carry):
        n = nch - 1 - m
        sl = pl.ds(pl.multiple_of(n * c, c), c)
        qc, kc, vc = q_s[sl, :], k_s[sl, :], v_s[sl, :]
        o = acc[sl, :] + jnp.dot(qc, st[...].astype(BF16), preferred_element_type=F32) * qw_b
        st[...] = g_chunk * st[...] + kv_inc(kc, vc, kw_b)
        ms = jnp.mean(o * o, axis=-1, keepdims=True)
        o = o * lax.rsqrt(ms + EPS) * gain
        gate = g_ref[0, sl, :]
        o_ref[0, sl, :] = o * (gate * _sigmoid(gate))
        return carry

    lax.fori_loop(0, nch, bwd_body, 0)


def _retention(z, cos, sin, lg, gain):
    b, s, _ = z.shape
    base = OFF_C // LANE
    zs = lambda o: pl.BlockSpec((1, s, LANE), lambda i, h: (i, 0, base + o + h))
    return pl.pallas_call(
        _ret_kernel,
        name="retention",
        grid=(b, RET_H),
        in_specs=[zs(0), zs(RET_H), zs(2 * RET_H), zs(3 * RET_H),
                  pl.BlockSpec((s, LANE), lambda i, h: (0, 0)),
                  pl.BlockSpec((s, LANE), lambda i, h: (0, 0)),
                  pl.BlockSpec((1, 1, LANE), lambda i, h: (h, 0, 0)),
                  pl.BlockSpec((1, LANE), lambda i, h: (0, h))],
        out_specs=pl.BlockSpec((1, s, LANE), lambda i, h: (i, 0, h)),
        out_shape=jax.ShapeDtypeStruct((b, s, GROUP_W), F32),
        scratch_shapes=[pltpu.VMEM((s, LANE), BF16), pltpu.VMEM((s, LANE), BF16),
                        pltpu.VMEM((s, LANE), BF16), pltpu.VMEM((s, LANE), F32),
                        pltpu.VMEM((LANE, LANE), F32)],
        compiler_params=_cparams(("parallel", "parallel")),
    )(z, z, z, z, cos, sin, lg, gain)


def _rwkv_prep_kernel(z_ref, zp_ref, zn_ref, mu_ref, w2h_ref, w2l_ref, a2h_ref, a2l_ref,
                      g2h_ref, g2l_ref, w0_ref, a0_ref, kk_ref, ka_ref, bd_ref,
                      r_o, k_o, v_o, kk_o, b_o, lwf_o, lwb_o, g_o):
    i = pl.program_id(1)
    last = pl.num_programs(1) - 1
    z = z_ref[0]
    tt = z.shape[0]
    prev_row = jnp.where(i > 0, zp_ref[0][7:8, :], 0.0)
    next_row = jnp.where(i < last, zn_ref[0][0:1, :], 0.0)
    row = lax.broadcasted_iota(jnp.int32, z.shape, 0)
    prev = jnp.where(row == 0, prev_row, pltpu.roll(z, 1, 0))
    nxt = jnp.where(row == tt - 1, next_row, pltpu.roll(z, tt - 1, 0))
    zs = z + (0.5 * (prev + nxt) - z) * mu_ref[...]
    g = GROUP_W
    r, k, v = zs[:, :g], zs[:, g:2 * g], zs[:, 2 * g:3 * g]
    lora_w = _dot3(jnp.tanh(zs[:, D_MAIN:D_MAIN + LANE]), w2h_ref[...], w2l_ref[...])
    lw = -jnp.exp(-_softplus(-(w0_ref[...] + lora_w)) - 0.5)
    a = _sigmoid(a0_ref[...] + _dot3(zs[:, D_MAIN + LANE:D_MAIN + 2 * LANE], a2h_ref[...], a2l_ref[...]))
    gg = _dot3(_sigmoid(zs[:, D_MAIN + 2 * LANE:]), g2h_ref[...], g2l_ref[...])
    kk = k * kk_ref[...]
    kk = kk * lax.rsqrt(_seg_sum(kk * kk, bd_ref[...]) + 1e-12)
    r_o[0] = r
    k_o[0] = k * (1.0 + (a - 1.0) * ka_ref[...])
    v_o[0] = v
    kk_o[0] = kk
    b_o[0] = kk * a
    lwf_o[0] = lw[:, :g]
    lwb_o[0] = lw[:, g:]
    g_o[0] = gg


def _rwkv_prep(z, mu, w2h, w2l, a2h, a2l, g2h, g2l, w0, a0, k_k, k_a, bd):
    b, s, _ = z.shape
    tt = min(256, s)
    nb8 = s // 8
    cb = OFF_D // D_COLS_PAD
    full = lambda a: pl.BlockSpec(a.shape, lambda i, j: (0,) * a.ndim)
    osp = pl.BlockSpec((1, tt, GROUP_W), lambda i, j: (i, j, 0))
    osh = jax.ShapeDtypeStruct((b, s, GROUP_W), F32)
    return pl.pallas_call(
        _rwkv_prep_kernel,
        name="rwkv_prep",
        grid=(b, s // tt),
        in_specs=[pl.BlockSpec((1, tt, D_COLS_PAD), lambda i, j: (i, j, cb)),
                  pl.BlockSpec((1, 8, D_COLS_PAD), lambda i, j: (i, jnp.maximum(j * (tt // 8) - 1, 0), cb)),
                  pl.BlockSpec((1, 8, D_COLS_PAD), lambda i, j: (i, jnp.minimum((j + 1) * (tt // 8), nb8 - 1), cb)),
                  full(mu), full(w2h), full(w2l), full(a2h), full(a2l), full(g2h), full(g2l),
                  full(w0), full(a0), full(k_k), full(k_a), full(bd)],
        out_specs=[osp] * 8,
        out_shape=[osh] * 8,
        compiler_params=_cparams(("parallel", "parallel")),
    )(z, z, z, mu, w2h, w2l, a2h, a2l, g2h, g2l, w0, a0, k_k, k_a, bd)


def _rwkv_chunk(r, lw, k, v, kk, b, ht, fwd):
    c = r.shape[0]
    hd = RWKV_HD
    row = lax.broadcasted_iota(jnp.int32, (c, LANE), 0)
    cum = lw
    d = 1
    while d < c:
        cum = cum + _shift_rows(cum, d if fwd else -d, 0.0, row)
        d *= 2
    cum_ex = cum - lw
    mid = cum[c // 2:c // 2 + 1, :]
    tot = cum[c - 1:c, :] if fwd else cum[0:1, :]
    e_ex = jnp.exp(cum_ex)
    c_hat = jnp.exp(cum_ex - mid)
    c_til = jnp.exp(mid - cum)
    c_end = jnp.exp(tot - cum)
    kk0 = kk * e_ex
    kk_hat = kk * c_hat
    if fwd:
        r0 = r * jnp.exp(cum)
        r_hat = r * jnp.exp(cum - mid)
    else:
        r0 = r * e_ex
        r_hat = r * c_hat
    k_til = k * c_til
    b_til = b * c_til

    lane = lax.broadcasted_iota(jnp.int32, (c, LANE), 1)
    head0 = lane < hd
    stack_heads = lambda t: jnp.concatenate([jnp.where(head0, t, 0.0), jnp.where(head0, 0.0, t)], axis=0)
    dup = lambda t: jnp.concatenate([t, t], axis=0)
    unstack = lambda t: jnp.where(head0, t[:c], t[c:])

    i2 = lax.broadcasted_iota(jnp.int32, (2 * c, 2 * c), 0)
    j2 = lax.broadcasted_iota(jnp.int32, (2 * c, 2 * c), 1)
    same = (i2 // c) == (j2 // c)
    ti, tj = i2 % c, j2 % c
    if fwd:
        m_strict = same & (ti > tj)
        m_y = same & (ti >= tj)
    else:
        m_strict = same & (ti < tj)
        m_y = m_strict

    lhs_k = stack_heads(kk_hat)
    lhs_r = stack_heads(r_hat)
    rhs_k = dup(k_til)
    rhs_b = dup(b_til)
    m_kv = jnp.where(m_strict, _dot_nt(lhs_k, rhs_k), 0.0)
    n_mat = jnp.where(m_strict, _dot_nt(lhs_k, rhs_b), 0.0)
    a_rk = jnp.where(m_y, _dot_nt(lhs_r, rhs_k), 0.0)
    a_rb = jnp.where(m_y, _dot_nt(lhs_r, rhs_b), 0.0)

    eye = (i2 == j2).astype(F32)
    t_inv = eye - n_mat
    pw = _dot(n_mat, n_mat)
    step = 2
    while True:
        t_inv = t_inv + _dot(t_inv, pw)
        step *= 2
        if step >= c:
            break
        pw = _dot(pw, pw)

    v_st = dup(v)
    x_st = dup(_dot_nt(kk0, ht)) + _dot(m_kv, v_st)
    u_st = _dot(t_inv, x_st)
    y_st = _dot(a_rk, v_st) - _dot(a_rb, u_st)
    y = unstack(y_st) + _dot_nt(r0, ht)
    u = unstack(u_st)

    lhs_t = jnp.concatenate([v, -u], axis=0).T
    rhs_e = jnp.concatenate([k * c_end, b * c_end], axis=0)
    hi = lax.broadcasted_iota(jnp.int32, (LANE, LANE), 0)
    hj = lax.broadcasted_iota(jnp.int32, (LANE, LANE), 1)
    inc = jnp.where((hi // hd) == (hj // hd), _dot(lhs_t, rhs_e), 0.0)
    return y, ht * jnp.exp(tot) + inc


def _rwkv_scan_kernel(rf, kf, vf, kkf, bf, lwf, rb, kb, vb, kkb, bb, lwb, yf_ref, yb_ref, h_ref):
    @pl.when(pl.program_id(1) == 0)
    def _():
        h_ref[...] = jnp.zeros_like(h_ref)

    dirs = ((rf, lwf, kf, vf, kkf, bf, yf_ref, True), (rb, lwb, kb, vb, kkb, bb, yb_ref, False))
    for d, (r_, lw_, k_, v_, kk_, b_, y_ref, fwd) in enumerate(dirs):
        for p in range(GROUP_W // LANE):
            sl = slice(p * LANE, (p + 1) * LANE)
            y, hn = _rwkv_chunk(r_[0, :, sl], lw_[0, :, sl], k_[0, :, sl], v_[0, :, sl],
                                kk_[0, :, sl], b_[0, :, sl], h_ref[d, p], fwd)
            h_ref[d, p] = hn
            y_ref[0, :, sl] = y


def _rwkv_scan(r, k, v, kk, b, lwf, lwb):
    bsz, s, _ = r.shape
    c = RWKV_CHUNK
    nch = s // c
    fs = pl.BlockSpec((1, c, GROUP_W), lambda i, n: (i, n, 0))
    bs = pl.BlockSpec((1, c, GROUP_W), lambda i, n: (i, nch - 1 - n, 0))
    osh = jax.ShapeDtypeStruct((bsz, s, GROUP_W), F32)
    return pl.pallas_call(
        _rwkv_scan_kernel,
        name="rwkv_scan",
        grid=(bsz, nch),
        in_specs=[fs] * 6 + [bs] * 6,
        out_specs=[fs, bs],
        out_shape=[osh, osh],
        scratch_shapes=[pltpu.VMEM((2, GROUP_W // LANE, LANE, LANE), F32)],
        compiler_params=_cparams(("parallel", "arbitrary")),
    )(r, k, v, kk, b, lwf, r, k, v, kk, b, lwb)


def _rwkv_post_kernel(yf_ref, yb_ref, r_ref, k_ref, v_ref, g_ref, rk_ref, lnw_ref, lnb_ref, bd_ref, o_ref):
    bd = bd_ref[...]
    v = v_ref[...]
    y = yf_ref[...] + yb_ref[...] + _seg_sum(r_ref[...] * k_ref[...] * rk_ref[...], bd) * v
    mean = _seg_sum(y, bd) * (1.0 / RWKV_HD)
    yc = y - mean
    var = _seg_sum(yc * yc, bd) * (1.0 / RWKV_HD)
    y = yc * lax.rsqrt(var + RWKV_LN_EPS) * lnw_ref[...] + lnb_ref[...]
    o_ref[...] = y * g_ref[...]


def _rwkv_post(yf, yb, r, k, v, g, rk, lnw, lnb, bd):
    t = yf.shape[0]
    tm = min(1024, t)
    big = pl.BlockSpec((tm, GROUP_W), lambda i: (i, 0))
    vec = pl.BlockSpec((1, GROUP_W), lambda i: (0, 0))
    return pl.pallas_call(
        _rwkv_post_kernel,
        name="rwkv_post",
        grid=(t // tm,),
        in_specs=[big] * 6 + [vec] * 3 + [pl.BlockSpec((GROUP_W, GROUP_W), lambda i: (0, 0))],
        out_specs=big,
        out_shape=jax.ShapeDtypeStruct((t, GROUP_W), F32),
        compiler_params=_cparams(("parallel",)),
    )(yf, yb, r, k, v, g, rk, lnw, lnb, bd)


def _outproj_kernel(ya_ref, yb_ref, yc_ref, yd_ref, ga_ref, gb_ref, w_ref, x_ref, o_ref, cat_ref):
    @pl.when(pl.program_id(1) == 0)
    def _():
        g = GROUP_W
        ya = ya_ref[...]
        ya = ya * lax.rsqrt(jnp.mean(ya * ya, axis=-1, keepdims=True) + EPS) * ga_ref[...]
        yb = yb_ref[...]
        yb = yb * lax.rsqrt(jnp.mean(yb * yb, axis=-1, keepdims=True) + EPS) * gb_ref[...]
        cat_ref[:, 0:g] = ya.astype(BF16)
        cat_ref[:, g:2 * g] = yb.astype(BF16)
        cat_ref[:, 2 * g:3 * g] = yc_ref[...].astype(BF16)
        cat_ref[:, 3 * g:4 * g] = yd_ref[...].astype(BF16)

    o_ref[...] = x_ref[...] + jnp.dot(cat_ref[...], w_ref[...], preferred_element_type=F32)


def _outproj(ya, yb, yc, yd, ga, gb, w, x2d):
    t, d = x2d.shape
    tm = min(1024, t)
    tn = 512
    ysp = pl.BlockSpec((tm, GROUP_W), lambda i, j: (i, 0))
    gsp = pl.BlockSpec((1, GROUP_W), lambda i, j: (0, 0))
    return pl.pallas_call(
        _outproj_kernel,
        name="outproj",
        grid=(t // tm, d // tn),
        in_specs=[ysp, ysp, ysp, ysp, gsp, gsp,
                  pl.BlockSpec((d, tn), lambda i, j: (0, j)),
                  pl.BlockSpec((tm, tn), lambda i, j: (i, j))],
        out_specs=pl.BlockSpec((tm, tn), lambda i, j: (i, j)),
        out_shape=jax.ShapeDtypeStruct((t, d), F32),
        scratch_shapes=[pltpu.VMEM((tm, d), BF16)],
        compiler_params=_cparams(("parallel", "arbitrary")),
    )(ya, yb, yc, yd, ga, gb, w, x2d)


def _ffn_kernel(x_ref, g_ref, wg_ref, wu_ref, wd_ref, o_ref, h_ref, acc_ref):
    j = pl.program_id(1)

    @pl.when(j == 0)
    def _():
        x = x_ref[...]
        ms = jnp.mean(x * x, axis=-1, keepdims=True)
        h_ref[...] = (x * lax.rsqrt(ms + EPS) * g_ref[...]).astype(BF16)
        acc_ref[...] = x

    h = h_ref[...]
    gt = jnp.dot(h, wg_ref[...], preferred_element_type=F32)
    up = jnp.dot(h, wu_ref[...], preferred_element_type=F32)
    act = (gt * _sigmoid(gt) * up).astype(BF16)
    acc_ref[...] += jnp.dot(act, wd_ref[...], preferred_element_type=F32)

    @pl.when(j == pl.num_programs(1) - 1)
    def _():
        o_ref[...] = acc_ref[...]


def _ffn(x2d, gain, wg, wu, wd):
    t, d = x2d.shape
    f = wg.shape[1]
    tm = min(512, t)
    tf = 512
    return pl.pallas_call(
        _ffn_kernel,
        name="ffn",
        grid=(t // tm, f // tf),
        in_specs=[pl.BlockSpec((tm, d), lambda i, j: (i, 0)),
                  pl.BlockSpec((1, d), lambda i, j: (0, 0)),
                  pl.BlockSpec((d, tf), lambda i, j: (0, j)),
                  pl.BlockSpec((d, tf), lambda i, j: (0, j)),
                  pl.BlockSpec((tf, d), lambda i, j: (j, 0))],
        out_specs=pl.BlockSpec((tm, d), lambda i, j: (i, 0)),
        out_shape=jax.ShapeDtypeStruct((t, d), F32),
        scratch_shapes=[pltpu.VMEM((tm, d), BF16), pltpu.VMEM((tm, d), F32)],
        compiler_params=_cparams(("parallel", "arbitrary")),
    )(x2d, gain, wg, wu, wd)


def _block_diag_ones(n, seg):
    i = jnp.arange(n) // seg
    return (i[:, None] == i[None, :]).astype(BF16)


def _rope_tables(n_tok, head_dim, n_copies):
    half = head_dim // 2
    quarter = half // 2
    rows = (jnp.arange(n_tok, dtype=jnp.int32) // GRID_W).astype(F32)
    cols = (jnp.arange(n_tok, dtype=jnp.int32) % GRID_W).astype(F32)
    inv = 1.0 / (ROPE_BASE ** (jnp.arange(0, half, 2, dtype=F32) / half))
    ang_r = rows[:, None] * inv[None, :]
    ang_c = cols[:, None] * inv[None, :]
    cos = jnp.concatenate([jnp.cos(ang_r)] * 2 + [jnp.cos(ang_c)] * 2, axis=-1)
    sin = jnp.concatenate([-jnp.sin(ang_r), jnp.sin(ang_r), -jnp.sin(ang_c), jnp.sin(ang_c)], axis=-1)
    del quarter
    return jnp.tile(cos, (1, n_copies)), jnp.tile(sin, (1, n_copies))


def _pair_block_diag(w):
    lead = w.shape[:-3]
    w = w.reshape(lead + (4, 2, LRU_BD, LRU_BD))
    z = jnp.zeros_like(w[..., 0, :, :])
    top = jnp.concatenate([w[..., 0, :, :], z], axis=-1)
    bot = jnp.concatenate([z, w[..., 1, :, :]], axis=-1)
    return jnp.concatenate([top, bot], axis=-2)


def _split_w(w):
    hi = w.astype(BF16)
    return hi, (w - hi.astype(F32)).astype(BF16)


def _prep_layer(p):
    w_in = p["w_in"]
    d = w_in.shape[0]
    q = w_in[:, :512]
    kx = w_in[:, 512:640]
    vx = w_in[:, 640:768]
    zero64 = jnp.zeros((d, 64), F32)
    dm = w_in[:, OFF_D:]
    d_part = jnp.concatenate([dm[:, :D_MAIN + 192], zero64, dm[:, D_MAIN + 192:]], axis=1)
    w_z = jnp.concatenate([q, kx, vx, w_in[:, OFF_B:OFF_D], d_part], axis=1).astype(BF16)

    mu = p["rwkv_mu"]
    mu_pad = jnp.concatenate([mu[:D_MAIN + 192], jnp.zeros((64,), F32), mu[D_MAIN + 192:]])[None, :]
    w2 = p["rwkv_w2"]
    zl = jnp.zeros_like(w2[0])
    w2cat = jnp.concatenate([jnp.concatenate([w2[0], zl], axis=1),
                             jnp.concatenate([zl, w2[1]], axis=1)], axis=0)
    a2pad = jnp.concatenate([p["rwkv_a2"], jnp.zeros_like(p["rwkv_a2"])], axis=0)
    out = {
        "norm_mix": p["norm_mix"][None, :],
        "w_z": w_z,
        "gq": jnp.tile(p["attn_q_norm"] * (ATT_HD ** -0.5), ATT_HQ)[None, :],
        "gk": jnp.tile(p["attn_k_norm"], ATT_HKV)[None, :],
        "ga": p["attn_out_norm"][None, :],
        "cw": p["lru_conv_w"], "cb": p["lru_conv_b"][None, :],
        "wa": _split_w(_pair_block_diag(p["lru_wa"])), "wx": _split_w(_pair_block_diag(p["lru_wx"])),
        "ba": p["lru_ba"], "bx": p["lru_bx"], "lam": p["lru_lambda"],
        "gb": p["lru_out_norm"][None, :],
        "ret_gain": p["ret_norm"][None, :],
        "mu": mu_pad,
        "w2": _split_w(w2cat), "a2": _split_w(a2pad), "g2": _split_w(p["rwkv_g2"]),
        "w0": p["rwkv_w0"].reshape(1, 2 * GROUP_W), "a0": p["rwkv_a0"][None, :],
        "k_k": p["rwkv_k_k"][None, :], "k_a": p["rwkv_k_a"][None, :],
        "r_k": p["rwkv_r_k"].reshape(1, GROUP_W),
        "ln_w": p["rwkv_ln_w"][None, :], "ln_b": p["rwkv_ln_b"][None, :],
        "w_out": p["w_out"].astype(BF16),
        "norm_ffn": p["norm_ffn"][None, :],
        "w_gate": p["w_gate"].astype(BF16), "w_up": p["w_up"].astype(BF16),
        "w_down": p["w_down"].astype(BF16),
    }
    return out


def _layer(x, lp, consts):
    bsz, s, d = x.shape
    t = bsz * s
    x2d = x.reshape(t, d)
    z = _inproj(x2d, lp["norm_mix"], lp["w_z"]).reshape(bsz, s, Z_COLS)

    ya = _attention(z, consts["att_tabs"], lp["gq"], lp["gk"], consts["bd512"], consts["bd128"])
    yb = _rglru(z, lp["cw"], lp["cb"], lp["wa"][0], lp["wa"][1], lp["wx"][0], lp["wx"][1],
                lp["ba"], lp["bx"], lp["lam"])
    yc = _retention(z, consts["ret_cos"], consts["ret_sin"], consts["ret_lg"], lp["ret_gain"])
    r, k, v, kk, b, lwf, lwb, g = _rwkv_prep(
        z, lp["mu"], lp["w2"][0], lp["w2"][1], lp["a2"][0], lp["a2"][1], lp["g2"][0], lp["g2"][1],
        lp["w0"], lp["a0"], lp["k_k"], lp["k_a"], consts["bd512"])
    yf, ybw = _rwkv_scan(r, k, v, kk, b, lwf, lwb)
    f2 = lambda a: a.reshape(t, GROUP_W)
    yd = _rwkv_post(f2(yf), f2(ybw), f2(r), f2(k), f2(v), f2(g), lp["r_k"], lp["ln_w"], lp["ln_b"],
                    consts["bd512"])
    x2d = _outproj(f2(ya), f2(yb), f2(yc), yd, lp["ga"], lp["gb"], lp["w_out"], x2d)
    x2d = _ffn(x2d, lp["norm_ffn"], lp["w_gate"], lp["w_up"], lp["w_down"])
    return x2d.reshape(bsz, s, d)


def _consts(s):
    cq, sq = _rope_tables(s, ATT_HD, ATT_HQ)
    ck, sk = _rope_tables(s, ATT_HD, ATT_HKV)
    rc, rs = _rope_tables(s, RET_HD, 1)
    log_g = jnp.log1p(-jnp.exp2(-5.0 - jnp.arange(RET_H, dtype=F32)))
    return {
        "att_tabs": (cq, sq, ck, sk),
        "ret_cos": rc, "ret_sin": rs,
        "ret_lg": jnp.broadcast_to(log_g[:, None, None], (RET_H, 1, LANE)),
        "bd512": _block_diag_ones(GROUP_W, 64),
        "bd128": _block_diag_ones(LANE, 64),
    }


def _trunk(x, layers, consts):
    for lp in layers:
        x = _layer(x, lp, consts)
    return x


def kernel(x_prompt, x_sample, norm_mix, w_in, attn_q_norm, attn_k_norm, attn_out_norm, lru_conv_w, lru_conv_b, lru_wa, lru_ba, lru_wx, lru_bx, lru_lambda, lru_out_norm, ret_norm, rwkv_mu, rwkv_w0, rwkv_w2, rwkv_a0, rwkv_a2, rwkv_g2, rwkv_k_k, rwkv_k_a, rwkv_r_k, rwkv_ln_w, rwkv_ln_b, w_out, norm_ffn, w_gate, w_up, w_down):
    params = {
        "norm_mix": norm_mix, "w_in": w_in,
        "attn_q_norm": attn_q_norm, "attn_k_norm": attn_k_norm, "attn_out_norm": attn_out_norm,
        "lru_conv_w": lru_conv_w, "lru_conv_b": lru_conv_b, "lru_wa": lru_wa, "lru_ba": lru_ba,
        "lru_wx": lru_wx, "lru_bx": lru_bx, "lru_lambda": lru_lambda, "lru_out_norm": lru_out_norm,
        "ret_norm": ret_norm,
        "rwkv_mu": rwkv_mu, "rwkv_w0": rwkv_w0, "rwkv_w2": rwkv_w2, "rwkv_a0": rwkv_a0,
        "rwkv_a2": rwkv_a2, "rwkv_g2": rwkv_g2, "rwkv_k_k": rwkv_k_k, "rwkv_k_a": rwkv_k_a,
        "rwkv_r_k": rwkv_r_k, "rwkv_ln_w": rwkv_ln_w, "rwkv_ln_b": rwkv_ln_b,
        "w_out": w_out, "norm_ffn": norm_ffn, "w_gate": w_gate, "w_up": w_up, "w_down": w_down,
    }
    depth = w_in.shape[0]
    layers = [_prep_layer({name: arr[l] for name, arr in params.items()}) for l in range(depth)]
    y_prompt = _trunk(x_prompt, layers, _consts(x_prompt.shape[1]))
    y_sample = _trunk(x_sample, layers, _consts(x_sample.shape[1]))
    return (y_prompt, y_sample)
```

```python
import functools

import jax
import jax.numpy as jnp
from jax import lax
from jax.experimental import pallas as pl
from jax.experimental.pallas import tpu as pltpu

F32 = jnp.float32
BF16 = jnp.bfloat16

D_MODEL = 2048
GRID_W = 64
EPS = 1e-6
GROUP_W = 512
ATT_HD = 64
ATT_HQ = 8
ATT_HKV = 2
ROPE_BASE = 10000.0
LRU_BD = 64
LRU_C = 8.0
CONV_W = 4
CONV_LEFT = 2
RET_HD = 128
RET_H = 4
RWKV_HD = 64
RWKV_LN_EPS = 64e-5
D_FF = 5632

A_COLS = 768
B_COLS = 1024
C_COLS = 2048
D_MAIN = 1536
OFF_B = A_COLS
OFF_C = OFF_B + B_COLS
OFF_D = OFF_C + C_COLS
D_COLS_PAD = 1920
Z_COLS = OFF_D + D_COLS_PAD

LANE = 128
VMEM_LIMIT = 56 * 1024 * 1024

RET_CHUNK = 256
RWKV_CHUNK = 64


def _cparams(sem):
    return pltpu.CompilerParams(dimension_semantics=sem, vmem_limit_bytes=VMEM_LIMIT)


def _dot(a, b):
    return jnp.dot(a.astype(BF16), b.astype(BF16), preferred_element_type=F32)


def _dot_nt(a, b):
    return lax.dot_general(a.astype(BF16), b.astype(BF16), (((1,), (1,)), ((), ())),
                           preferred_element_type=F32)


def _split(x):
    hi = x.astype(BF16)
    lo = (x - hi.astype(F32)).astype(BF16)
    return hi, lo


def _dot3(x, wh, wl):
    xh, xl = _split(x)
    return (jnp.dot(xh, wh, preferred_element_type=F32)
            + jnp.dot(xl, wh, preferred_element_type=F32)
            + jnp.dot(xh, wl, preferred_element_type=F32))


def _seg_sum(x, bd):
    xh, xl = _split(x)
    return jnp.dot(xh, bd, preferred_element_type=F32) + jnp.dot(xl, bd, preferred_element_type=F32)


def _rope(x, cos, sin_signed, half):
    n = x.shape[-1]
    lane = lax.broadcasted_iota(jnp.int32, x.shape, 1)
    first = (lane % (2 * half)) < half
    rot = jnp.where(first, pltpu.roll(x, n - half, 1), pltpu.roll(x, half, 1))
    return x * cos + rot * sin_signed


def _softplus(x):
    return jnp.maximum(x, 0.0) + jnp.log1p(jnp.exp(-jnp.abs(x)))


def _sigmoid(x):
    return 1.0 / (1.0 + jnp.exp(-x))


def _rmsnorm_kernel(x_ref, g_ref, o_ref):
    x = x_ref[...]
    ms = jnp.mean(x * x, axis=-1, keepdims=True)
    o_ref[...] = (x * lax.rsqrt(ms + EPS) * g_ref[...]).astype(BF16)


def _rmsnorm(x2d, gain):
    t, d = x2d.shape
    tm = min(1024, t)
    return pl.pallas_call(
        _rmsnorm_kernel,
        name="rmsnorm",
        grid=(t // tm,),
        in_specs=[pl.BlockSpec((tm, d), lambda i: (i, 0)),
                  pl.BlockSpec((1, d), lambda i: (0, 0))],
        out_specs=pl.BlockSpec((tm, d), lambda i: (i, 0)),
        out_shape=jax.ShapeDtypeStruct((t, d), BF16),
        compiler_params=_cparams(("parallel",)),
    )(x2d, gain)


def _inproj_kernel(h_ref, w_ref, o_ref):
    o_ref[...] = jnp.dot(h_ref[...], w_ref[...], preferred_element_type=F32).astype(BF16)


def _inproj(x2d, gain, w):
    h = _rmsnorm(x2d, gain)
    t, d = h.shape
    n = w.shape[1]
    tm = min(2048, t)
    tn = 640
    return pl.pallas_call(
        _inproj_kernel,
        name="inproj",
        grid=(t // tm, n // tn),
        in_specs=[pl.BlockSpec((tm, d), lambda i, j: (i, 0)),
                  pl.BlockSpec((d, tn), lambda i, j: (0, j))],
        out_specs=pl.BlockSpec((tm, tn), lambda i, j: (i, j)),
        out_shape=jax.ShapeDtypeStruct((t, n), BF16),
        compiler_params=_cparams(("parallel", "arbitrary")),
    )(h, w)


def _attn_kernel(q_ref, kv_ref, cq_ref, sq_ref, ck_ref, sk_ref, gq_ref, gk_ref, bdq_ref, bdk_ref,
                 o_ref, k_s, v_s):
    @pl.when(pl.program_id(1) == 0)
    def _():
        kv = kv_ref[0]
        k = kv[:, :LANE].astype(F32)
        ms = _seg_sum(k * k, bdk_ref[...]) * (1.0 / ATT_HD)
        k = k * lax.rsqrt(ms + EPS) * gk_ref[...]
        k = _rope(k, ck_ref[...], sk_ref[...], ATT_HD // 4)
        ones = jnp.ones((kv.shape[0], ATT_HD), BF16)
        for g in range(ATT_HKV):
            k_s[g] = k[:, g * ATT_HD:(g + 1) * ATT_HD].astype(BF16)
            v_s[g] = jnp.concatenate([kv[:, LANE + g * ATT_HD:LANE + (g + 1) * ATT_HD], ones], axis=-1)

    q = q_ref[0].astype(F32)
    ms = _seg_sum(q * q, bdq_ref[...]) * (1.0 / ATT_HD)
    q = q * lax.rsqrt(ms + EPS) * gq_ref[...]
    q = _rope(q, cq_ref[...], sq_ref[...], ATT_HD // 4).astype(BF16)
    n_rep = ATT_HQ // ATT_HKV
    for h in range(ATT_HQ):
        g = h // n_rep
        s = _dot_nt(q[:, h * ATT_HD:(h + 1) * ATT_HD], k_s[g])
        m = jnp.max(s, axis=-1, keepdims=True)
        p = jnp.exp((s - m).astype(BF16))
        o = jnp.dot(p, v_s[g], preferred_element_type=F32)
        o_ref[0, :, h * ATT_HD:(h + 1) * ATT_HD] = (o[:, :ATT_HD] / o[:, ATT_HD:]).astype(BF16)


def _attention(z, tabs, gq, gk, bdq, bdk):
    b, s, _ = z.shape
    tq = min(256, s)
    cq, sq, ck, sk = tabs
    return pl.pallas_call(
        _attn_kernel,
        name="attn",
        grid=(b, s // tq),
        in_specs=[pl.BlockSpec((1, tq, 512), lambda i, j: (i, j, 0)),
                  pl.BlockSpec((1, s, 256), lambda i, j: (i, 0, 2)),
                  pl.BlockSpec((tq, 512), lambda i, j: (j, 0)),
                  pl.BlockSpec((tq, 512), lambda i, j: (j, 0)),
                  pl.BlockSpec((s, LANE), lambda i, j: (0, 0)),
                  pl.BlockSpec((s, LANE), lambda i, j: (0, 0)),
                  pl.BlockSpec((1, 512), lambda i, j: (0, 0)),
                  pl.BlockSpec((1, LANE), lambda i, j: (0, 0)),
                  pl.BlockSpec((512, 512), lambda i, j: (0, 0)),
                  pl.BlockSpec((LANE, LANE), lambda i, j: (0, 0))],
        out_specs=pl.BlockSpec((1, tq, 512), lambda i, j: (i, j, 0)),
        out_shape=jax.ShapeDtypeStruct((b, s, GROUP_W), BF16),
        scratch_shapes=[pltpu.VMEM((ATT_HKV, s, ATT_HD), BF16),
                        pltpu.VMEM((ATT_HKV, s, 2 * ATT_HD), BF16)],
        compiler_params=_cparams(("parallel", "arbitrary")),
    )(z, z, cq, sq, ck, sk, gq, gk, bdq, bdk)


def _shift_rows(v, d, fill, row):
    n = v.shape[0]
    if d > 0:
        return jnp.where(row >= d, pltpu.roll(v, d, 0), fill)
    return jnp.where(row < n + d, pltpu.roll(v, n + d, 0), fill)


def _linear_scan(a, b, row, reverse):
    n = a.shape[0]
    d = 1
    while d < n:
        sd = -d if reverse else d
        b = a * _shift_rows(b, sd, 0.0, row) + b
        if 2 * d < n:
            a = a * _shift_rows(a, sd, 1.0, row)
        d *= 2
    return b


def _lru_kernel(x_ref, gate_ref, cw_ref, cb_ref, wah_ref, wal_ref, wxh_ref, wxl_ref,
                ba_ref, bx_ref, lam_ref, o_ref):
    x = x_ref[0].astype(F32)
    row = lax.broadcasted_iota(jnp.int32, x.shape, 0)
    xc = cb_ref[...]
    for j in range(CONV_W):
        d = CONV_LEFT - j
        xs = x if d == 0 else _shift_rows(x, d, 0.0, row)
        xc = xc + xs * cw_ref[j:j + 1, :]
    h = None
    for d in range(2):
        r = _sigmoid(_dot3(xc, wah_ref[d, 0], wal_ref[d, 0]) + ba_ref[d:d + 1, :])
        i = _sigmoid(_dot3(xc, wxh_ref[d, 0], wxl_ref[d, 0]) + bx_ref[d:d + 1, :])
        log_a = -LRU_C * r * _softplus(-lam_ref[d:d + 1, :])
        a = jnp.exp(log_a)
        bb = jnp.sqrt(-jnp.tanh(log_a) * (a * a + 1.0)) * (i * xc)
        hd = _linear_scan(a, bb, row, reverse=(d == 1))
        h = hd if h is None else h + hd
    gate = gate_ref[0].astype(F32)
    gelu = 0.5 * gate * (1.0 + jnp.tanh(0.7978845608028654 * (gate + 0.044715 * gate * gate * gate)))
    o_ref[0] = (h * gelu).astype(BF16)


def _rglru(z, cw, cb, wah, wal, wxh, wxl, ba, bx, lam):
    b, s, _ = z.shape
    nc = GROUP_W // LANE
    xo = OFF_B // LANE
    go = (OFF_B + GROUP_W) // LANE
    vec = lambda r: pl.BlockSpec((r, LANE), lambda i, c: (0, c))
    wsp = pl.BlockSpec((2, 1, LANE, LANE), lambda i, c: (0, c, 0, 0))
    return pl.pallas_call(
        _lru_kernel,
        name="rglru",
        grid=(b, nc),
        in_specs=[pl.BlockSpec((1, s, LANE), lambda i, c: (i, 0, xo + c)),
                  pl.BlockSpec((1, s, LANE), lambda i, c: (i, 0, go + c)),
                  vec(CONV_W), vec(1), wsp, wsp, wsp, wsp, vec(2), vec(2), vec(2)],
        out_specs=pl.BlockSpec((1, s, LANE), lambda i, c: (i, 0, c)),
        out_shape=jax.ShapeDtypeStruct((b, s, GROUP_W), BF16),
        compiler_params=_cparams(("parallel", "parallel")),
    )(z, z, cw, cb, wah, wal, wxh, wxl, ba, bx, lam)


def _ret_kernel(q_ref, k_ref, v_ref, g_ref, cos_ref, sin_ref, lg_ref, gain_ref, o_ref,
                q_s, k_s, v_s, acc, st):
    s_len = q_ref.shape[1]
    c = min(RET_CHUNK, s_len)
    nch = s_len // c
    half = RET_HD // 4
    q_s[...] = _rope(q_ref[0].astype(F32), cos_ref[...], sin_ref[...], half).astype(BF16)
    k_s[...] = (_rope(k_ref[0].astype(F32), cos_ref[...], sin_ref[...], half) * (RET_HD ** -0.5)).astype(BF16)
    v_s[...] = v_ref[0]

    lg = lg_ref[0]
    ii = lax.broadcasted_iota(jnp.int32, (c, c), 0)
    jj = lax.broadcasted_iota(jnp.int32, (c, c), 1)
    gam = jnp.exp(jnp.abs(ii - jj).astype(F32) * lg[:, :1])
    ri = lax.broadcasted_iota(jnp.int32, (c, LANE), 0).astype(F32)
    qw_f = jnp.exp((ri + 1.0) * lg)
    qw_b = jnp.exp((c - ri) * lg)
    kw_f = jnp.exp((c - 1.0 - ri) * lg)
    kw_b = jnp.exp(ri * lg)
    g_chunk = jnp.exp(c * lg)
    gain = gain_ref[...]

    def kv_inc(kc, vc, kw):
        return jnp.dot((kc.astype(F32) * kw).T.astype(BF16), vc, preferred_element_type=F32)

    st[...] = jnp.zeros_like(st)

    def fwd_body(n, carry):
        sl = pl.ds(pl.multiple_of(n * c, c), c)
        qc, kc, vc = q_s[sl, :], k_s[sl, :], v_s[sl, :]
        sc = _dot_nt(qc, kc) * gam
        o = jnp.dot(sc.astype(BF16), vc, preferred_element_type=F32)
        o = o + jnp.dot(qc, st[...].astype(BF16), preferred_element_type=F32) * qw_f
        acc[sl, :] = o
        st[...] = g_chunk * st[...] + kv_inc(kc, vc, kw_f)
        return carry

    lax.fori_loop(0, nch, fwd_body, 0)
    st[...] = jnp.zeros_like(st)

    def bwd_body(m, carry):
        n = nch - 1 - m
        sl = pl.ds(pl.multiple_of(n * c, c), c)
        qc, kc, vc = q_s[sl, :], k_s[sl, :], v_s[sl, :]
        o = acc[sl, :] + jnp.dot(qc, st[...].astype(BF16), preferred_element_type=F32) * qw_b
        st[...] = g_chunk * st[...] + kv_inc(kc, vc, kw_b)
        ms = jnp.mean(o * o, axis=-1, keepdims=True)
        o = o * lax.rsqrt(ms + EPS) * gain
        gate = g_ref[0, sl, :].astype(F32)
        o_ref[0, sl, :] = (o * (gate * _sigmoid(gate))).astype(BF16)
        return carry

    lax.fori_loop(0, nch, bwd_body, 0)


def _retention(z, cos, sin, lg, gain):
    b, s, _ = z.shape
    base = OFF_C // LANE
    zs = lambda o: pl.BlockSpec((1, s, LANE), lambda i, h: (i, 0, base + o + h))
    return pl.pallas_call(
        _ret_kernel,
        name="retention",
        grid=(b, RET_H),
        in_specs=[zs(0), zs(RET_H), zs(2 * RET_H), zs(3 * RET_H),
                  pl.BlockSpec((s, LANE), lambda i, h: (0, 0)),
                  pl.BlockSpec((s, LANE), lambda i, h: (0, 0)),
                  pl.BlockSpec((1, 1, LANE), lambda i, h: (h, 0, 0)),
                  pl.BlockSpec((1, LANE), lambda i, h: (0, h))],
        out_specs=pl.BlockSpec((1, s, LANE), lambda i, h: (i, 0, h)),
        out_shape=jax.ShapeDtypeStruct((b, s, GROUP_W), BF16),
        scratch_shapes=[pltpu.VMEM((s, LANE), BF16), pltpu.VMEM((s, LANE), BF16),
                        pltpu.VMEM((s, LANE), BF16), pltpu.VMEM((s, LANE), F32),
                        pltpu.VMEM((LANE, LANE), F32)],
        compiler_params=_cparams(("parallel", "parallel")),
    )(z, z, z, z, cos, sin, lg, gain)


def _rwkv_prep_kernel(z_ref, zp_ref, zn_ref, mu_ref, w2h_ref, w2l_ref, a2h_ref, a2l_ref,
                      g2h_ref, g2l_ref, w0_ref, a0_ref, kk_ref, ka_ref, bd_ref,
                      r_o, k_o, v_o, kk_o, b_o, lwf_o, lwb_o, g_o):
    i = pl.program_id(1)
    last = pl.num_programs(1) - 1
    z = z_ref[0].astype(F32)
    tt = z.shape[0]
    hr = zp_ref.shape[1]
    prev_row = jnp.where(i > 0, zp_ref[0].astype(F32)[hr - 1:hr, :], 0.0)
    next_row = jnp.where(i < last, zn_ref[0].astype(F32)[0:1, :], 0.0)
    row = lax.broadcasted_iota(jnp.int32, z.shape, 0)
    prev = jnp.where(row == 0, prev_row, pltpu.roll(z, 1, 0))
    nxt = jnp.where(row == tt - 1, next_row, pltpu.roll(z, tt - 1, 0))
    zs = z + (0.5 * (prev + nxt) - z) * mu_ref[...]
    g = GROUP_W
    r, k, v = zs[:, :g], zs[:, g:2 * g], zs[:, 2 * g:3 * g]
    lora_w = _dot3(jnp.tanh(zs[:, D_MAIN:D_MAIN + LANE]), w2h_ref[...], w2l_ref[...])
    lw = -jnp.exp(-_softplus(-(w0_ref[...] + lora_w)) - 0.5)
    a = _sigmoid(a0_ref[...] + _dot3(zs[:, D_MAIN + LANE:D_MAIN + 2 * LANE], a2h_ref[...], a2l_ref[...]))
    gg = _dot3(_sigmoid(zs[:, D_MAIN + 2 * LANE:]), g2h_ref[...], g2l_ref[...])
    kk = k * kk_ref[...]
    kk = kk * lax.rsqrt(_seg_sum(kk * kk, bd_ref[...]) + 1e-12)
    r_o[0] = r
    k_o[0] = k * (1.0 + (a - 1.0) * ka_ref[...])
    v_o[0] = v
    kk_o[0] = kk
    b_o[0] = kk * a
    lwf_o[0] = lw[:, :g]
    lwb_o[0] = lw[:, g:]
    g_o[0] = gg


def _rwkv_prep(z, mu, w2h, w2l, a2h, a2l, g2h, g2l, w0, a0, k_k, k_a, bd):
    b, s, _ = z.shape
    tt = min(256, s)
    hr = 16
    nbh = s // hr
    cb = OFF_D // D_COLS_PAD
    full = lambda a: pl.BlockSpec(a.shape, lambda i, j: (0,) * a.ndim)
    osp = pl.BlockSpec((1, tt, GROUP_W), lambda i, j: (i, j, 0))
    osh = jax.ShapeDtypeStruct((b, s, GROUP_W), F32)
    return pl.pallas_call(
        _rwkv_prep_kernel,
        name="rwkv_prep",
        grid=(b, s // tt),
        in_specs=[pl.BlockSpec((1, tt, D_COLS_PAD), lambda i, j: (i, j, cb)),
                  pl.BlockSpec((1, hr, D_COLS_PAD), lambda i, j: (i, jnp.maximum(j * (tt // hr) - 1, 0), cb)),
                  pl.BlockSpec((1, hr, D_COLS_PAD), lambda i, j: (i, jnp.minimum((j + 1) * (tt // hr), nbh - 1), cb)),
                  full(mu), full(w2h), full(w2l), full(a2h), full(a2l), full(g2h), full(g2l),
                  full(w0), full(a0), full(k_k), full(k_a), full(bd)],
        out_specs=[osp] * 8,
        out_shape=[osh] * 8,
        compiler_params=_cparams(("parallel", "parallel")),
    )(z, z, z, mu, w2h, w2l, a2h, a2l, g2h, g2l, w0, a0, k_k, k_a, bd)


def _rwkv_chunks(chains):
    c = RWKV_CHUNK
    hd = RWKV_HD
    row = lax.broadcasted_iota(jnp.int32, (c, LANE), 0)
    lane = lax.broadcasted_iota(jnp.int32, (c, LANE), 1)
    head0 = lane < hd
    i2 = lax.broadcasted_iota(jnp.int32, (2 * c, 2 * c), 0)
    j2 = lax.broadcasted_iota(jnp.int32, (2 * c, 2 * c), 1)
    same = (i2 // c) == (j2 // c)
    ti, tj = i2 % c, j2 % c
    m_strict = {True: same & (ti > tj), False: same & (ti < tj)}
    m_y = {True: same & (ti >= tj), False: m_strict[False]}
    eye = (i2 == j2).astype(F32)
    head_diag = (i2 // hd) == (j2 // hd)
    bf = lambda t: t.astype(BF16)
    cat = lambda ts: jnp.concatenate(ts, axis=0)
    unstack = lambda t: jnp.where(head0, t[:c], t[c:])
    fwds = [ch[7] for ch in chains]

    def decays(ch):
        r, lw, k, v, kk, b, _, fwd = ch
        cum = lw
        d = 1
        while d < c:
            cum = cum + _shift_rows(cum, d if fwd else -d, 0.0, row)
            d *= 2
        cum_ex = cum - lw
        mid = cum[c // 2:c // 2 + 1, :]
        tot = cum[c - 1:c, :] if fwd else cum[0:1, :]
        e_ex = jnp.exp(cum_ex)
        c_hat = jnp.exp(cum_ex - mid)
        c_til = jnp.exp(mid - cum)
        c_end = jnp.exp(tot - cum)
        kk_hat = kk * c_hat
        if fwd:
            r0 = r * jnp.exp(cum)
            r_hat = r * jnp.exp(cum - mid)
        else:
            r0 = r * e_ex
            r_hat = r * c_hat
        zero = jnp.zeros_like(r)
        lhs = cat([bf(jnp.where(head0, kk_hat, zero)), bf(jnp.where(head0, zero, kk_hat)),
                   bf(jnp.where(head0, r_hat, zero)), bf(jnp.where(head0, zero, r_hat))])
        kt, bt = bf(k * c_til), bf(b * c_til)
        rhs = cat([kt, kt, bt, bt])
        q0 = cat([bf(kk * e_ex), bf(r0)])
        rhs_e = cat([bf(k * c_end), bf(b * c_end)])
        vb = bf(v)
        return lhs, rhs, q0, rhs_e, cat([vb, vb]), jnp.exp(tot)

    st = [decays(ch) for ch in chains]
    gram = [lax.dot_general(s[0], s[1], (((1,), (1,)), ((), ())), preferred_element_type=F32) for s in st]
    n2 = 2 * c
    zf = jnp.zeros((n2, n2), F32)
    n_mat = [jnp.where(m_strict[f], g[:n2, n2:], zf) for g, f in zip(gram, fwds)]
    mk_ark = [cat([bf(jnp.where(m_strict[f], g[:n2, :n2], zf)), bf(jnp.where(m_y[f], g[n2:, :n2], zf))])
              for g, f in zip(gram, fwds)]
    a_rb = [bf(jnp.where(m_y[f], g[n2:, n2:], zf)) for g, f in zip(gram, fwds)]

    t_inv = [eye - n for n in n_mat]
    pw = [jnp.dot(bf(n), bf(n), preferred_element_type=F32) for n in n_mat]
    step = 4
    while step < c:
        tp = [jnp.dot(cat([bf(t), bf(p)]), bf(p), preferred_element_type=F32) for t, p in zip(t_inv, pw)]
        t_inv = [t + x[:n2] for t, x in zip(t_inv, tp)]
        pw = [x[n2:] for x in tp]
        step *= 2
    t_inv = [t + jnp.dot(bf(t), bf(p), preferred_element_type=F32) for t, p in zip(t_inv, pw)]

    hts = [ch[6] for ch in chains]
    hx = [lax.dot_general(s[2], bf(h), (((1,), (1,)), ((), ())), preferred_element_type=F32)
          for s, h in zip(st, hts)]
    mv = [jnp.dot(m, s[4], preferred_element_type=F32) for m, s in zip(mk_ark, st)]
    x_st = [cat([x[:c], x[:c]]) + m[:n2] for x, m in zip(hx, mv)]
    u_st = [jnp.dot(bf(t), bf(x), preferred_element_type=F32) for t, x in zip(t_inv, x_st)]
    y_st = [m[n2:] - jnp.dot(a, bf(u), preferred_element_type=F32) for m, a, u in zip(mv, a_rb, u_st)]
    ys = [unstack(y) + x[c:] for y, x in zip(y_st, hx)]
    lhs_t = [cat([ch[3], -unstack(u)]).T for ch, u in zip(chains, u_st)]
    inc = [jnp.dot(bf(l), s[3], preferred_element_type=F32) for l, s in zip(lhs_t, st)]
    h_new = [h * s[5] + jnp.where(head_diag, i, zf) for h, s, i in zip(hts, st, inc)]
    return list(zip(ys, h_new))


def _rwkv_scan_kernel(rf, kf, vf, kkf, bf, lwf, rb, kb, vb, kkb, bb, lwb, yf_ref, yb_ref, h_ref):
    @pl.when(pl.program_id(1) == 0)
    def _():
        h_ref[...] = jnp.zeros_like(h_ref)

    dirs = ((rf, lwf, kf, vf, kkf, bf, True), (rb, lwb, kb, vb, kkb, bb, False))
    n_pairs = GROUP_W // LANE
    chains = []
    for d, (r_, lw_, k_, v_, kk_, b_, fwd) in enumerate(dirs):
        for p in range(n_pairs):
            sl = slice(p * LANE, (p + 1) * LANE)
            chains.append((r_[0, :, sl], lw_[0, :, sl], k_[0, :, sl], v_[0, :, sl],
                           kk_[0, :, sl], b_[0, :, sl], h_ref[d, p], fwd))
    outs = _rwkv_chunks(chains)
    for idx, (y, hn) in enumerate(outs):
        d, p = divmod(idx, n_pairs)
        h_ref[d, p] = hn
        (yf_ref, yb_ref)[d][0, :, p * LANE:(p + 1) * LANE] = y


def _rwkv_scan(r, k, v, kk, b, lwf, lwb):
    bsz, s, _ = r.shape
    c = RWKV_CHUNK
    nch = s // c
    fs = pl.BlockSpec((1, c, GROUP_W), lambda i, n: (i, n, 0))
    bs = pl.BlockSpec((1, c, GROUP_W), lambda i, n: (i, nch - 1 - n, 0))
    osh = jax.ShapeDtypeStruct((bsz, s, GROUP_W), F32)
    return pl.pallas_call(
        _rwkv_scan_kernel,
        name="rwkv_scan",
        grid=(bsz, nch),
        in_specs=[fs] * 6 + [bs] * 6,
        out_specs=[fs, bs],
        out_shape=[osh, osh],
        scratch_shapes=[pltpu.VMEM((2, GROUP_W // LANE, LANE, LANE), F32)],
        compiler_params=_cparams(("parallel", "arbitrary")),
    )(r, k, v, kk, b, lwf, r, k, v, kk, b, lwb)


def _rwkv_post_kernel(yf_ref, yb_ref, r_ref, k_ref, v_ref, g_ref, rk_ref, lnw_ref, lnb_ref, bd_ref, o_ref):
    bd = bd_ref[...]
    v = v_ref[...]
    y = yf_ref[...] + yb_ref[...] + _seg_sum(r_ref[...] * k_ref[...] * rk_ref[...], bd) * v
    mean = _seg_sum(y, bd) * (1.0 / RWKV_HD)
    yc = y - mean
    var = _seg_sum(yc * yc, bd) * (1.0 / RWKV_HD)
    y = yc * lax.rsqrt(var + RWKV_LN_EPS) * lnw_ref[...] + lnb_ref[...]
    o_ref[...] = (y * g_ref[...]).astype(BF16)


def _rwkv_post(yf, yb, r, k, v, g, rk, lnw, lnb, bd):
    t = yf.shape[0]
    tm = min(1024, t)
    big = pl.BlockSpec((tm, GROUP_W), lambda i: (i, 0))
    vec = pl.BlockSpec((1, GROUP_W), lambda i: (0, 0))
    return pl.pallas_call(
        _rwkv_post_kernel,
        name="rwkv_post",
        grid=(t // tm,),
        in_specs=[big] * 6 + [vec] * 3 + [pl.BlockSpec((GROUP_W, GROUP_W), lambda i: (0, 0))],
        out_specs=big,
        out_shape=jax.ShapeDtypeStruct((t, GROUP_W), BF16),
        compiler_params=_cparams(("parallel",)),
    )(yf, yb, r, k, v, g, rk, lnw, lnb, bd)


def _outproj_kernel(ya_ref, yb_ref, yc_ref, yd_ref, ga_ref, gb_ref, w_ref, x_ref, o_ref):
    ya = ya_ref[...].astype(F32)
    ya = ya * lax.rsqrt(jnp.mean(ya * ya, axis=-1, keepdims=True) + EPS) * ga_ref[...]
    yb = yb_ref[...].astype(F32)
    yb = yb * lax.rsqrt(jnp.mean(yb * yb, axis=-1, keepdims=True) + EPS) * gb_ref[...]
    cat = jnp.concatenate([ya.astype(BF16), yb.astype(BF16), yc_ref[...], yd_ref[...]], axis=-1)
    o_ref[...] = x_ref[...] + jnp.dot(cat, w_ref[...], preferred_element_type=F32)


def _outproj(ya, yb, yc, yd, ga, gb, w, x2d):
    t, d = x2d.shape
    tm = min(512, t)
    ysp = pl.BlockSpec((tm, GROUP_W), lambda i: (i, 0))
    gsp = pl.BlockSpec((1, GROUP_W), lambda i: (0, 0))
    return pl.pallas_call(
        _outproj_kernel,
        name="outproj",
        grid=(t // tm,),
        in_specs=[ysp, ysp, ysp, ysp, gsp, gsp,
                  pl.BlockSpec((d, d), lambda i: (0, 0)),
                  pl.BlockSpec((tm, d), lambda i: (i, 0))],
        out_specs=pl.BlockSpec((tm, d), lambda i: (i, 0)),
        out_shape=jax.ShapeDtypeStruct((t, d), F32),
        compiler_params=_cparams(("parallel",)),
    )(ya, yb, yc, yd, ga, gb, w, x2d)


def _ffn_kernel(x_ref, g_ref, wg_ref, wu_ref, wd_ref, o_ref, h_ref, acc_ref):
    j = pl.program_id(1)

    @pl.when(j == 0)
    def _():
        x = x_ref[...]
        ms = jnp.mean(x * x, axis=-1, keepdims=True)
        h_ref[...] = (x * lax.rsqrt(ms + EPS) * g_ref[...]).astype(BF16)
        acc_ref[...] = x

    h = h_ref[...]
    gt = jnp.dot(h, wg_ref[...], preferred_element_type=F32)
    up = jnp.dot(h, wu_ref[...], preferred_element_type=F32)
    act = (gt * _sigmoid(gt) * up).astype(BF16)
    acc_ref[...] += jnp.dot(act, wd_ref[...], preferred_element_type=F32)

    @pl.when(j == pl.num_programs(1) - 1)
    def _():
        o_ref[...] = acc_ref[...]


def _ffn(x2d, gain, wg, wu, wd):
    t, d = x2d.shape
    f = wg.shape[1]
    tm = min(512, t)
    tf = 512
    return pl.pallas_call(
        _ffn_kernel,
        name="ffn",
        grid=(t // tm, f // tf),
        in_specs=[pl.BlockSpec((tm, d), lambda i, j: (i, 0)),
                  pl.BlockSpec((1, d), lambda i, j: (0, 0)),
                  pl.BlockSpec((d, tf), lambda i, j: (0, j)),
                  pl.BlockSpec((d, tf), lambda i, j: (0, j)),
                  pl.BlockSpec((tf, d), lambda i, j: (j, 0))],
        out_specs=pl.BlockSpec((tm, d), lambda i, j: (i, 0)),
        out_shape=jax.ShapeDtypeStruct((t, d), F32),
        scratch_shapes=[pltpu.VMEM((tm, d), BF16), pltpu.VMEM((tm, d), F32)],
        compiler_params=_cparams(("parallel", "arbitrary")),
    )(x2d, gain, wg, wu, wd)


def _block_diag_ones(n, seg):
    i = jnp.arange(n) // seg
    return (i[:, None] == i[None, :]).astype(BF16)


def _rope_tables(n_tok, head_dim, n_copies):
    half = head_dim // 2
    quarter = half // 2
    rows = (jnp.arange(n_tok, dtype=jnp.int32) // GRID_W).astype(F32)
    cols = (jnp.arange(n_tok, dtype=jnp.int32) % GRID_W).astype(F32)
    inv = 1.0 / (ROPE_BASE ** (jnp.arange(0, half, 2, dtype=F32) / half))
    ang_r = rows[:, None] * inv[None, :]
    ang_c = cols[:, None] * inv[None, :]
    cos = jnp.concatenate([jnp.cos(ang_r)] * 2 + [jnp.cos(ang_c)] * 2, axis=-1)
    sin = jnp.concatenate([-jnp.sin(ang_r), jnp.sin(ang_r), -jnp.sin(ang_c), jnp.sin(ang_c)], axis=-1)
    del quarter
    return jnp.tile(cos, (1, n_copies)), jnp.tile(sin, (1, n_copies))


def _pair_block_diag(w):
    lead = w.shape[:-3]
    w = w.reshape(lead + (4, 2, LRU_BD, LRU_BD))
    z = jnp.zeros_like(w[..., 0, :, :])
    top = jnp.concatenate([w[..., 0, :, :], z], axis=-1)
    bot = jnp.concatenate([z, w[..., 1, :, :]], axis=-1)
    return jnp.concatenate([top, bot], axis=-2)


def _split_w(w):
    hi = w.astype(BF16)
    return hi, (w - hi.astype(F32)).astype(BF16)


def _prep_layer(p):
    w_in = p["w_in"]
    d = w_in.shape[0]
    q = w_in[:, :512]
    kx = w_in[:, 512:640]
    vx = w_in[:, 640:768]
    zero64 = jnp.zeros((d, 64), F32)
    dm = w_in[:, OFF_D:]
    d_part = jnp.concatenate([dm[:, :D_MAIN + 192], zero64, dm[:, D_MAIN + 192:]], axis=1)
    w_z = jnp.concatenate([q, kx, vx, w_in[:, OFF_B:OFF_D], d_part], axis=1).astype(BF16)

    mu = p["rwkv_mu"]
    mu_pad = jnp.concatenate([mu[:D_MAIN + 192], jnp.zeros((64,), F32), mu[D_MAIN + 192:]])[None, :]
    w2 = p["rwkv_w2"]
    zl = jnp.zeros_like(w2[0])
    w2cat = jnp.concatenate([jnp.concatenate([w2[0], zl], axis=1),
                             jnp.concatenate([zl, w2[1]], axis=1)], axis=0)
    a2pad = jnp.concatenate([p["rwkv_a2"], jnp.zeros_like(p["rwkv_a2"])], axis=0)
    out = {
        "norm_mix": p["norm_mix"][None, :],
        "w_z": w_z,
        "gq": jnp.tile(p["attn_q_norm"] * (ATT_HD ** -0.5), ATT_HQ)[None, :],
        "gk": jnp.tile(p["attn_k_norm"], ATT_HKV)[None, :],
        "ga": p["attn_out_norm"][None, :],
        "cw": p["lru_conv_w"], "cb": p["lru_conv_b"][None, :],
        "wa": _split_w(_pair_block_diag(p["lru_wa"])), "wx": _split_w(_pair_block_diag(p["lru_wx"])),
        "ba": p["lru_ba"], "bx": p["lru_bx"], "lam": p["lru_lambda"],
        "gb": p["lru_out_norm"][None, :],
        "ret_gain": p["ret_norm"][None, :],
        "mu": mu_pad,
        "w2": _split_w(w2cat), "a2": _split_w(a2pad), "g2": _split_w(p["rwkv_g2"]),
        "w0": p["rwkv_w0"].reshape(1, 2 * GROUP_W), "a0": p["rwkv_a0"][None, :],
        "k_k": p["rwkv_k_k"][None, :], "k_a": p["rwkv_k_a"][None, :],
        "r_k": p["rwkv_r_k"].reshape(1, GROUP_W),
        "ln_w": p["rwkv_ln_w"][None, :], "ln_b": p["rwkv_ln_b"][None, :],
        "w_out": p["w_out"].astype(BF16),
        "norm_ffn": p["norm_ffn"][None, :],
        "w_gate": p["w_gate"].astype(BF16), "w_up": p["w_up"].astype(BF16),
        "w_down": p["w_down"].astype(BF16),
    }
    return out


def _layer(x, lp, consts):
    bsz, s, d = x.shape
    t = bsz * s
    x2d = x.reshape(t, d)
    z = _inproj(x2d, lp["norm_mix"], lp["w_z"]).reshape(bsz, s, Z_COLS)

    ya = _attention(z, consts["att_tabs"], lp["gq"], lp["gk"], consts["bd512"], consts["bd128"])
    yb = _rglru(z, lp["cw"], lp["cb"], lp["wa"][0], lp["wa"][1], lp["wx"][0], lp["wx"][1],
                lp["ba"], lp["bx"], lp["lam"])
    yc = _retention(z, consts["ret_cos"], consts["ret_sin"], consts["ret_lg"], lp["ret_gain"])
    r, k, v, kk, b, lwf, lwb, g = _rwkv_prep(
        z, lp["mu"], lp["w2"][0], lp["w2"][1], lp["a2"][0], lp["a2"][1], lp["g2"][0], lp["g2"][1],
        lp["w0"], lp["a0"], lp["k_k"], lp["k_a"], consts["bd512"])
    yf, ybw = _rwkv_scan(r, k, v, kk, b, lwf, lwb)
    f2 = lambda a: a.reshape(t, GROUP_W)
    yd = _rwkv_post(f2(yf), f2(ybw), f2(r), f2(k), f2(v), f2(g), lp["r_k"], lp["ln_w"], lp["ln_b"],
                    consts["bd512"])
    x2d = _outproj(f2(ya), f2(yb), f2(yc), yd, lp["ga"], lp["gb"], lp["w_out"], x2d)
    x2d = _ffn(x2d, lp["norm_ffn"], lp["w_gate"], lp["w_up"], lp["w_down"])
    return x2d.reshape(bsz, s, d)


def _consts(s):
    cq, sq = _rope_tables(s, ATT_HD, ATT_HQ)
    ck, sk = _rope_tables(s, ATT_HD, ATT_HKV)
    rc, rs = _rope_tables(s, RET_HD, 1)
    log_g = jnp.log1p(-jnp.exp2(-5.0 - jnp.arange(RET_H, dtype=F32)))
    return {
        "att_tabs": (cq, sq, ck, sk),
        "ret_cos": rc, "ret_sin": rs,
        "ret_lg": jnp.broadcast_to(log_g[:, None, None], (RET_H, 1, LANE)),
        "bd512": _block_diag_ones(GROUP_W, 64),
        "bd128": _block_diag_ones(LANE, 64),
    }


def _trunk(x, layers, consts):
    for lp in layers:
        x = _layer(x, lp, consts)
    return x


def kernel(x_prompt, x_sample, norm_mix, w_in, attn_q_norm, attn_k_norm, attn_out_norm, lru_conv_w, lru_conv_b, lru_wa, lru_ba, lru_wx, lru_bx, lru_lambda, lru_out_norm, ret_norm, rwkv_mu, rwkv_w0, rwkv_w2, rwkv_a0, rwkv_a2, rwkv_g2, rwkv_k_k, rwkv_k_a, rwkv_r_k, rwkv_ln_w, rwkv_ln_b, w_out, norm_ffn, w_gate, w_up, w_down):
    params = {
        "norm_mix": norm_mix, "w_in": w_in,
        "attn_q_norm": attn_q_norm, "attn_k_norm": attn_k_norm, "attn_out_norm": attn_out_norm,
        "lru_conv_w": lru_conv_w, "lru_conv_b": lru_conv_b, "lru_wa": lru_wa, "lru_ba": lru_ba,
        "lru_wx": lru_wx, "lru_bx": lru_bx, "lru_lambda": lru_lambda, "lru_out_norm": lru_out_norm,
        "ret_norm": ret_norm,
        "rwkv_mu": rwkv_mu, "rwkv_w0": rwkv_w0, "rwkv_w2": rwkv_w2, "rwkv_a0": rwkv_a0,
        "rwkv_a2": rwkv_a2, "rwkv_g2": rwkv_g2, "rwkv_k_k": rwkv_k_k, "rwkv_k_a": rwkv_k_a,
        "rwkv_r_k": rwkv_r_k, "rwkv_ln_w": rwkv_ln_w, "rwkv_ln_b": rwkv_ln_b,
        "w_out": w_out, "norm_ffn": norm_ffn, "w_gate": w_gate, "w_up": w_up, "w_down": w_down,
    }
    depth = w_in.shape[0]
    layers = [_prep_layer({name: arr[l] for name, arr in params.items()}) for l in range(depth)]
    y_prompt = _trunk(x_prompt, layers, _consts(x_prompt.shape[1]))
    y_sample = _trunk(x_sample, layers, _consts(x_sample.shape[1]))
    return (y_prompt, y_sample)
```

```python
import functools

import jax
import jax.numpy as jnp
from jax import lax
from jax.experimental import pallas as pl
from jax.experimental.pallas import tpu as pltpu

F32 = jnp.float32
BF16 = jnp.bfloat16

D_MODEL = 2048
GRID_W = 64
EPS = 1e-6
GROUP_W = 512
ATT_HD = 64
ATT_HQ = 8
ATT_HKV = 2
ROPE_BASE = 10000.0
LRU_BD = 64
LRU_C = 8.0
CONV_W = 4
CONV_LEFT = 2
RET_HD = 128
RET_H = 4
RWKV_HD = 64
RWKV_LN_EPS = 64e-5
D_FF = 5632

A_COLS = 768
B_COLS = 1024
C_COLS = 2048
D_MAIN = 1536
OFF_B = A_COLS
OFF_C = OFF_B + B_COLS
OFF_D = OFF_C + C_COLS
D_COLS_PAD = 1920
Z_COLS = OFF_D + D_COLS_PAD

LANE = 128
VMEM_LIMIT = 56 * 1024 * 1024

RET_CHUNK = 256
RWKV_CHUNK = 64
RWKV_SUBCHUNKS = 4


def _cparams(sem):
    return pltpu.CompilerParams(dimension_semantics=sem, vmem_limit_bytes=VMEM_LIMIT)


def _dot(a, b):
    return jnp.dot(a.astype(BF16), b.astype(BF16), preferred_element_type=F32)


def _dot_nt(a, b):
    return lax.dot_general(a.astype(BF16), b.astype(BF16), (((1,), (1,)), ((), ())),
                           preferred_element_type=F32)


def _split(x):
    hi = x.astype(BF16)
    lo = (x - hi.astype(F32)).astype(BF16)
    return hi, lo


def _dot3(x, wh, wl):
    xh, xl = _split(x)
    return (jnp.dot(xh, wh, preferred_element_type=F32)
            + jnp.dot(xl, wh, preferred_element_type=F32)
            + jnp.dot(xh, wl, preferred_element_type=F32))


def _seg_sum(x, bd):
    xh, xl = _split(x)
    return jnp.dot(xh, bd, preferred_element_type=F32) + jnp.dot(xl, bd, preferred_element_type=F32)


def _rope(x, cos, sin_signed, half):
    n = x.shape[-1]
    lane = lax.broadcasted_iota(jnp.int32, x.shape, 1)
    first = (lane % (2 * half)) < half
    rot = jnp.where(first, pltpu.roll(x, n - half, 1), pltpu.roll(x, half, 1))
    return x * cos + rot * sin_signed


def _softplus(x):
    return jnp.maximum(x, 0.0) + jnp.log1p(jnp.exp(-jnp.abs(x)))


def _sigmoid(x):
    return 0.5 * jnp.tanh(0.5 * x) + 0.5


def _rmsnorm_kernel(x_ref, g_ref, o_ref):
    x = x_ref[...]
    ms = jnp.mean(x * x, axis=-1, keepdims=True)
    o_ref[...] = (x * lax.rsqrt(ms + EPS) * g_ref[...]).astype(BF16)


def _rmsnorm(x2d, gain):
    t, d = x2d.shape
    tm = min(1024, t)
    return pl.pallas_call(
        _rmsnorm_kernel,
        name="rmsnorm",
        grid=(t // tm,),
        in_specs=[pl.BlockSpec((tm, d), lambda i: (i, 0)),
                  pl.BlockSpec((1, d), lambda i: (0, 0))],
        out_specs=pl.BlockSpec((tm, d), lambda i: (i, 0)),
        out_shape=jax.ShapeDtypeStruct((t, d), BF16),
        compiler_params=_cparams(("parallel",)),
    )(x2d, gain)


def _inproj_kernel(h_ref, w_ref, o_ref):
    o_ref[...] = jnp.dot(h_ref[...], w_ref[...], preferred_element_type=F32).astype(BF16)


def _inproj(x2d, gain, w):
    h = _rmsnorm(x2d, gain)
    t, d = h.shape
    n = w.shape[1]
    tm = min(1024, t)
    tn = 1920
    return pl.pallas_call(
        _inproj_kernel,
        name="inproj",
        grid=(t // tm, n // tn),
        in_specs=[pl.BlockSpec((tm, d), lambda i, j: (i, 0)),
                  pl.BlockSpec((d, tn), lambda i, j: (0, j))],
        out_specs=pl.BlockSpec((tm, tn), lambda i, j: (i, j)),
        out_shape=jax.ShapeDtypeStruct((t, n), BF16),
        compiler_params=_cparams(("parallel", "arbitrary")),
    )(h, w)


def _attn_kernel(q_ref, kv_ref, cq_ref, sq_ref, ck_ref, sk_ref, gq_ref, gk_ref, bdq_ref, bdk_ref,
                 o_ref, k_s, v_s):
    @pl.when(pl.program_id(1) == 0)
    def _():
        kv = kv_ref[0]
        k = kv[:, :LANE].astype(F32)
        ms = _seg_sum(k * k, bdk_ref[...]) * (1.0 / ATT_HD)
        k = k * lax.rsqrt(ms + EPS) * gk_ref[...]
        k = _rope(k, ck_ref[...], sk_ref[...], ATT_HD // 4)
        ones = jnp.ones((kv.shape[0], ATT_HD), BF16)
        for g in range(ATT_HKV):
            k_s[g] = k[:, g * ATT_HD:(g + 1) * ATT_HD].astype(BF16)
            v_s[g] = jnp.concatenate([kv[:, LANE + g * ATT_HD:LANE + (g + 1) * ATT_HD], ones], axis=-1)

    q = q_ref[0].astype(F32)
    ms = _seg_sum(q * q, bdq_ref[...]) * (1.0 / ATT_HD)
    q = q * lax.rsqrt(ms + EPS) * gq_ref[...]
    q = _rope(q, cq_ref[...], sq_ref[...], ATT_HD // 4).astype(BF16)
    n_rep = ATT_HQ // ATT_HKV
    for h in range(ATT_HQ):
        g = h // n_rep
        s = _dot_nt(q[:, h * ATT_HD:(h + 1) * ATT_HD], k_s[g])
        m = jnp.max(s, axis=-1, keepdims=True)
        p = jnp.exp((s - m).astype(BF16))
        o = jnp.dot(p, v_s[g], preferred_element_type=F32)
        o_ref[0, :, h * ATT_HD:(h + 1) * ATT_HD] = (o[:, :ATT_HD] / o[:, ATT_HD:]).astype(BF16)


def _attention(z, tabs, gq, gk, bdq, bdk):
    b, s, _ = z.shape
    tq = min(256, s)
    cq, sq, ck, sk = tabs
    return pl.pallas_call(
        _attn_kernel,
        name="attn",
        grid=(b, s // tq),
        in_specs=[pl.BlockSpec((1, tq, 512), lambda i, j: (i, j, 0)),
                  pl.BlockSpec((1, s, 256), lambda i, j: (i, 0, 2)),
                  pl.BlockSpec((tq, 512), lambda i, j: (j, 0)),
                  pl.BlockSpec((tq, 512), lambda i, j: (j, 0)),
                  pl.BlockSpec((s, LANE), lambda i, j: (0, 0)),
                  pl.BlockSpec((s, LANE), lambda i, j: (0, 0)),
                  pl.BlockSpec((1, 512), lambda i, j: (0, 0)),
                  pl.BlockSpec((1, LANE), lambda i, j: (0, 0)),
                  pl.BlockSpec((512, 512), lambda i, j: (0, 0)),
                  pl.BlockSpec((LANE, LANE), lambda i, j: (0, 0))],
        out_specs=pl.BlockSpec((1, tq, 512), lambda i, j: (i, j, 0)),
        out_shape=jax.ShapeDtypeStruct((b, s, GROUP_W), BF16),
        scratch_shapes=[pltpu.VMEM((ATT_HKV, s, ATT_HD), BF16),
                        pltpu.VMEM((ATT_HKV, s, 2 * ATT_HD), BF16)],
        compiler_params=_cparams(("parallel", "arbitrary")),
    )(z, z, cq, sq, ck, sk, gq, gk, bdq, bdk)


def _shift_rows(v, d, fill, row):
    n = v.shape[0]
    if d > 0:
        return jnp.where(row >= d, pltpu.roll(v, d, 0), fill)
    return jnp.where(row < n + d, pltpu.roll(v, n + d, 0), fill)


SUBLANES = 8


def _group_scan(a, b, h_in, sub, reverse):
    k = 1
    while k < SUBLANES:
        if reverse:
            keep = sub <= SUBLANES - 1 - k
            b = a * jnp.where(keep, pltpu.roll(b, SUBLANES - k, 0), 0.0) + b
            a = a * jnp.where(keep, pltpu.roll(a, SUBLANES - k, 0), 1.0)
        else:
            keep = sub >= k
            b = a * jnp.where(keep, pltpu.roll(b, k, 0), 0.0) + b
            a = a * jnp.where(keep, pltpu.roll(a, k, 0), 1.0)
        k *= 2
    return a * h_in + b


def _lru_kernel(x_ref, gate_ref, cw_ref, cb_ref, wgh_ref, wgl_ref,
                ba_ref, bx_ref, lam_ref, o_ref, a_s, b_s, h_s):
    x = x_ref[0].astype(F32)
    s_len = x.shape[0]
    n_grp = s_len // SUBLANES
    row = lax.broadcasted_iota(jnp.int32, x.shape, 0)
    xc = cb_ref[...]
    for j in range(CONV_W):
        d = CONV_LEFT - j
        xs = x if d == 0 else _shift_rows(x, d, 0.0, row)
        xc = xc + xs * cw_ref[j:j + 1, :]
    pre = _dot3(xc, wgh_ref[0], wgl_ref[0])
    for d in range(2):
        r = _sigmoid(pre[:, 2 * d * LANE:(2 * d + 1) * LANE] + ba_ref[d:d + 1, :])
        i = _sigmoid(pre[:, (2 * d + 1) * LANE:(2 * d + 2) * LANE] + bx_ref[d:d + 1, :])
        log_a = -LRU_C * r * _softplus(-lam_ref[d:d + 1, :])
        a = jnp.exp(log_a)
        om = -jnp.tanh(log_a) * (a * a + 1.0)
        a_s[d] = a
        b_s[d] = jnp.where(om > 0.0, om * lax.rsqrt(om), 0.0) * (i * xc)

    sub = lax.broadcasted_iota(jnp.int32, (SUBLANES, LANE), 0)

    def body(g, carry):
        hf_in, hb_in = carry
        gf = pl.ds(pl.multiple_of(g * SUBLANES, SUBLANES), SUBLANES)
        gb = pl.ds(pl.multiple_of((n_grp - 1 - g) * SUBLANES, SUBLANES), SUBLANES)
        hf = _group_scan(a_s[0, gf, :], b_s[0, gf, :], hf_in, sub, False)
        hb = _group_scan(a_s[1, gb, :], b_s[1, gb, :], hb_in, sub, True)
        h_s[0, gf, :] = hf
        h_s[1, gb, :] = hb
        return (jnp.broadcast_to(hf[SUBLANES - 1:SUBLANES], hf.shape), jnp.broadcast_to(hb[0:1], hb.shape))

    zero = jnp.zeros((SUBLANES, LANE), F32)
    lax.fori_loop(0, n_grp, body, (zero, zero), unroll=8)
    gate = gate_ref[0].astype(F32)
    gelu = 0.5 * gate * (1.0 + jnp.tanh(0.7978845608028654 * (gate + 0.044715 * gate * gate * gate)))
    o_ref[0] = ((h_s[0] + h_s[1]) * gelu).astype(BF16)


def _rglru(z, cw, cb, wgh, wgl, ba, bx, lam):
    b, s, _ = z.shape
    nc = GROUP_W // LANE
    xo = OFF_B // LANE
    go = (OFF_B + GROUP_W) // LANE
    vec = lambda r: pl.BlockSpec((r, LANE), lambda i, c: (0, c))
    wsp = pl.BlockSpec((1, LANE, 4 * LANE), lambda i, c: (c, 0, 0))
    return pl.pallas_call(
        _lru_kernel,
        name="rglru",
        grid=(b, nc),
        in_specs=[pl.BlockSpec((1, s, LANE), lambda i, c: (i, 0, xo + c)),
                  pl.BlockSpec((1, s, LANE), lambda i, c: (i, 0, go + c)),
                  vec(CONV_W), vec(1), wsp, wsp, vec(2), vec(2), vec(2)],
        out_specs=pl.BlockSpec((1, s, LANE), lambda i, c: (i, 0, c)),
        out_shape=jax.ShapeDtypeStruct((b, s, GROUP_W), BF16),
        scratch_shapes=[pltpu.VMEM((2, s, LANE), F32)] * 3,
        compiler_params=_cparams(("parallel", "parallel")),
    )(z, z, cw, cb, wgh, wgl, ba, bx, lam)


def _ret_kernel(q_ref, k_ref, v_ref, g_ref, cos_ref, sin_ref, lg_ref, gain_ref, o_ref,
                q_s, k_s, v_s, acc, st):
    s_len = q_ref.shape[1]
    c = min(RET_CHUNK, s_len)
    nch = s_len // c
    half = RET_HD // 4
    q_s[...] = _rope(q_ref[0].astype(F32), cos_ref[...], sin_ref[...], half).astype(BF16)
    k_s[...] = (_rope(k_ref[0].astype(F32), cos_ref[...], sin_ref[...], half) * (RET_HD ** -0.5)).astype(BF16)
    v_s[...] = v_ref[0]

    lg = lg_ref[0]
    ii = lax.broadcasted_iota(jnp.int32, (c, c), 0)
    jj = lax.broadcasted_iota(jnp.int32, (c, c), 1)
    gam = jnp.exp(jnp.abs(ii - jj).astype(F32) * lg[:, :1])
    ri = lax.broadcasted_iota(jnp.int32, (c, LANE), 0).astype(F32)
    qw_f = jnp.exp((ri + 1.0) * lg)
    qw_b = jnp.exp((c - ri) * lg)
    kw_f = jnp.exp((c - 1.0 - ri) * lg)
    kw_b = jnp.exp(ri * lg)
    g_chunk = jnp.exp(c * lg)
    gain = gain_ref[...]

    def kv_inc(kc, vc, kw):
        return jnp.dot((kc.astype(F32) * kw).T.astype(BF16), vc, preferred_element_type=F32)

    st[...] = jnp.zeros_like(st)

    def fwd_body(n, carry):
        sl = pl.ds(pl.multiple_of(n * c, c), c)
        qc, kc, vc = q_s[sl, :], k_s[sl, :], v_s[sl, :]
        sc = _dot_nt(qc, kc) * gam
        o = jnp.dot(sc.astype(BF16), vc, preferred_element_type=F32)
        o = o + jnp.dot(qc, st[...].astype(BF16), preferred_element_type=F32) * qw_f
        acc[sl, :] = o
        st[...] = g_chunk * st[...] + kv_inc(kc, vc, kw_f)
        return carry

    lax.fori_loop(0, nch, fwd_body, 0)
    st[...] = jnp.zeros_like(st)

    def bwd_body(m, carry):
        n = nch - 1 - m
        sl = pl.ds(pl.multiple_of(n * c, c), c)
        qc, kc, vc = q_s[sl, :], k_s[sl, :], v_s[sl, :]
        o = acc[sl, :] + jnp.dot(qc, st[...].astype(BF16), preferred_element_type=F32) * qw_b
        st[...] = g_chunk * st[...] + kv_inc(kc, vc, kw_b)
        ms = jnp.mean(o * o, axis=-1, keepdims=True)
        o = o * lax.rsqrt(ms + EPS) * gain
        gate = g_ref[0, sl, :].astype(F32)
        o_ref[0, sl, :] = (o * (gate * _sigmoid(gate))).astype(BF16)
        return carry

    lax.fori_loop(0, nch, bwd_body, 0)


def _retention(z, cos, sin, lg, gain):
    b, s, _ = z.shape
    base = OFF_C // LANE
    zs = lambda o: pl.BlockSpec((1, s, LANE), lambda i, h: (i, 0, base + o + h))
    return pl.pallas_call(
        _ret_kernel,
        name="retention",
        grid=(b, RET_H),
        in_specs=[zs(0), zs(RET_H), zs(2 * RET_H), zs(3 * RET_H),
                  pl.BlockSpec((s, LANE), lambda i, h: (0, 0)),
                  pl.BlockSpec((s, LANE), lambda i, h: (0, 0)),
                  pl.BlockSpec((1, 1, LANE), lambda i, h: (h, 0, 0)),
                  pl.BlockSpec((1, LANE), lambda i, h: (0, h))],
        out_specs=pl.BlockSpec((1, s, LANE), lambda i, h: (i, 0, h)),
        out_shape=jax.ShapeDtypeStruct((b, s, GROUP_W), BF16),
        scratch_shapes=[pltpu.VMEM((s, LANE), BF16), pltpu.VMEM((s, LANE), BF16),
                        pltpu.VMEM((s, LANE), BF16), pltpu.VMEM((s, LANE), F32),
                        pltpu.VMEM((LANE, LANE), F32)],
        compiler_params=_cparams(("parallel", "parallel")),
    )(z, z, z, z, cos, sin, lg, gain)


def _rwkv_prep_kernel(z_ref, zp_ref, zn_ref, mu_ref, w2h_ref, w2l_ref, a2h_ref, a2l_ref,
                      g2h_ref, g2l_ref, w0_ref, a0_ref, kk_ref, ka_ref, bd_ref,
                      r_o, k_o, v_o, kk_o, b_o, lwf_o, lwb_o, g_o):
    i = pl.program_id(1)
    last = pl.num_programs(1) - 1
    z = z_ref[0].astype(F32)
    tt = z.shape[0]
    hr = zp_ref.shape[1]
    prev_row = jnp.where(i > 0, zp_ref[0].astype(F32)[hr - 1:hr, :], 0.0)
    next_row = jnp.where(i < last, zn_ref[0].astype(F32)[0:1, :], 0.0)
    row = lax.broadcasted_iota(jnp.int32, z.shape, 0)
    prev = jnp.where(row == 0, prev_row, pltpu.roll(z, 1, 0))
    nxt = jnp.where(row == tt - 1, next_row, pltpu.roll(z, tt - 1, 0))
    zs = z + (0.5 * (prev + nxt) - z) * mu_ref[...]
    g = GROUP_W
    r, k, v = zs[:, :g], zs[:, g:2 * g], zs[:, 2 * g:3 * g]
    lora_w = _dot3(jnp.tanh(zs[:, D_MAIN:D_MAIN + LANE]), w2h_ref[...], w2l_ref[...])
    lw = -jnp.exp(-_softplus(-(w0_ref[...] + lora_w)) - 0.5)
    a = _sigmoid(a0_ref[...] + _dot3(zs[:, D_MAIN + LANE:D_MAIN + 2 * LANE], a2h_ref[...], a2l_ref[...]))
    gg = _dot3(_sigmoid(zs[:, D_MAIN + 2 * LANE:]), g2h_ref[...], g2l_ref[...])
    kk = k * kk_ref[...]
    kk = kk * lax.rsqrt(_seg_sum(kk * kk, bd_ref[...]) + 1e-12)
    r_o[0] = r
    k_o[0] = k * (1.0 + (a - 1.0) * ka_ref[...])
    v_o[0] = v
    kk_o[0] = kk
    b_o[0] = kk * a
    lwf_o[0] = lw[:, :g]
    lwb_o[0] = lw[:, g:]
    g_o[0] = gg


def _rwkv_prep(z, mu, w2h, w2l, a2h, a2l, g2h, g2l, w0, a0, k_k, k_a, bd):
    b, s, _ = z.shape
    tt = min(256, s)
    hr = 16
    nbh = s // hr
    cb = OFF_D // D_COLS_PAD
    full = lambda a: pl.BlockSpec(a.shape, lambda i, j: (0,) * a.ndim)
    osp = pl.BlockSpec((1, tt, GROUP_W), lambda i, j: (i, j, 0))
    osh = jax.ShapeDtypeStruct((b, s, GROUP_W), F32)
    return pl.pallas_call(
        _rwkv_prep_kernel,
        name="rwkv_prep",
        grid=(b, s // tt),
        in_specs=[pl.BlockSpec((1, tt, D_COLS_PAD), lambda i, j: (i, j, cb)),
                  pl.BlockSpec((1, hr, D_COLS_PAD), lambda i, j: (i, jnp.maximum(j * (tt // hr) - 1, 0), cb)),
                  pl.BlockSpec((1, hr, D_COLS_PAD), lambda i, j: (i, jnp.minimum((j + 1) * (tt // hr), nbh - 1), cb)),
                  full(mu), full(w2h), full(w2l), full(a2h), full(a2l), full(g2h), full(g2l),
                  full(w0), full(a0), full(k_k), full(k_a), full(bd)],
        out_specs=[osp] * 8,
        out_shape=[osh] * 8,
        compiler_params=_cparams(("parallel", "parallel")),
    )(z, z, z, mu, w2h, w2l, a2h, a2l, g2h, g2l, w0, a0, k_k, k_a, bd)


def _rwkv_prepare(chains):
    c = RWKV_CHUNK
    hd = RWKV_HD
    row = lax.broadcasted_iota(jnp.int32, (c, LANE), 0)
    lane = lax.broadcasted_iota(jnp.int32, (c, LANE), 1)
    head0 = lane < hd
    col = lane % c
    m_strict = {True: row > col, False: row < col}
    m_y = {True: row >= col, False: m_strict[False]}
    eye = (row == col).astype(F32)
    bf = lambda t: t.astype(BF16)
    cat = lambda ts: jnp.concatenate(ts, axis=0)
    per_head = functools.partial(_per_head, head0=head0)
    fwds = [ch[6] for ch in chains]

    def decays(ch):
        r, lw, k, v, kk, b, fwd = ch
        cum = lw
        d = 1
        while d < c:
            cum = cum + _shift_rows(cum, d if fwd else -d, 0.0, row)
            d *= 2
        cum_ex = cum - lw
        mid = cum[c // 2:c // 2 + 1, :]
        tot = cum[c - 1:c, :] if fwd else cum[0:1, :]
        e_ex = jnp.exp(cum_ex)
        c_hat = jnp.exp(cum_ex - mid)
        c_til = jnp.exp(mid - cum)
        c_end = jnp.exp(tot - cum)
        kk_hat = kk * c_hat
        if fwd:
            r0 = r * jnp.exp(cum)
            r_hat = r * jnp.exp(cum - mid)
        else:
            r0 = r * e_ex
            r_hat = r * c_hat
        lhs = cat([bf(kk_hat), bf(r_hat)])
        rhs = cat([per_head(bf(k * c_til)), per_head(bf(b * c_til))])
        q0 = cat([bf(kk * e_ex), bf(r0)])
        rhs_e = cat([bf(k * c_end), bf(b * c_end)])
        return lhs, rhs, q0, rhs_e, per_head(bf(v)), jnp.exp(tot)

    st = [decays(ch) for ch in chains]
    gram = [lax.dot_general(s[0], s[1], (((1,), (1,)), ((), ())), preferred_element_type=F32) for s in st]
    n2 = 2 * c
    zf = jnp.zeros((c, n2), F32)
    n_mat = [jnp.where(m_strict[f], g[:c, n2:], zf) for g, f in zip(gram, fwds)]
    mk_ark = [cat([bf(jnp.where(m_strict[f], g[:c, :n2], zf)), bf(jnp.where(m_y[f], g[c:, :n2], zf))])
              for g, f in zip(gram, fwds)]
    a_rb = [bf(jnp.where(m_y[f], g[c:, n2:], zf)) for g, f in zip(gram, fwds)]

    t_inv = [eye - n for n in n_mat]
    pw = [jnp.dot(bf(n), per_head(bf(n)), preferred_element_type=F32) for n in n_mat]
    step = 4
    while step < c:
        tp = [jnp.dot(cat([bf(t), bf(p)]), per_head(bf(p)), preferred_element_type=F32)
              for t, p in zip(t_inv, pw)]
        t_inv = [t + x[:c] for t, x in zip(t_inv, tp)]
        pw = [x[c:] for x in tp]
        step *= 2
    t_inv = [t + jnp.dot(bf(t), per_head(bf(p)), preferred_element_type=F32) for t, p in zip(t_inv, pw)]
    mv = [jnp.dot(m, s[4], preferred_element_type=F32) for m, s in zip(mk_ark, st)]
    return [(s[2], s[3], s[5], bf(t), m, a, ch[3]) for s, t, m, a, ch in zip(st, t_inv, mv, a_rb, chains)]


def _per_head(t, head0):
    zero = jnp.zeros_like(t)
    return jnp.concatenate([jnp.where(head0, t, zero), jnp.where(head0, zero, t)], axis=0)


def _rwkv_apply(prep, hts):
    c = RWKV_CHUNK
    lane = lax.broadcasted_iota(jnp.int32, (c, LANE), 1)
    per_head = functools.partial(_per_head, head0=lane < RWKV_HD)
    i2 = lax.broadcasted_iota(jnp.int32, (LANE, LANE), 0)
    j2 = lax.broadcasted_iota(jnp.int32, (LANE, LANE), 1)
    head_diag = (i2 // RWKV_HD) == (j2 // RWKV_HD)
    bf = lambda t: t.astype(BF16)
    hx = [lax.dot_general(pr[0], bf(h), (((1,), (1,)), ((), ())), preferred_element_type=F32)
          for pr, h in zip(prep, hts)]
    xs = [x[:c] + pr[4][:c] for x, pr in zip(hx, prep)]
    us = [jnp.dot(pr[3], per_head(bf(x)), preferred_element_type=F32) for pr, x in zip(prep, xs)]
    ys = [pr[4][c:] - jnp.dot(pr[5], per_head(bf(u)), preferred_element_type=F32) + x[c:]
          for pr, u, x in zip(prep, us, hx)]
    lhs_t = [jnp.concatenate([pr[6], -u], axis=0).T for pr, u in zip(prep, us)]
    inc = [jnp.dot(bf(l), pr[1], preferred_element_type=F32) for l, pr in zip(lhs_t, prep)]
    zs = jnp.zeros((LANE, LANE), F32)
    h_new = [h * pr[2] + jnp.where(head_diag, i, zs) for h, pr, i in zip(hts, prep, inc)]
    return list(zip(ys, h_new))


def _rwkv_scan_kernel(rf, kf, vf, kkf, bf, lwf, rb, kb, vb, kkb, bb, lwb, yf_ref, yb_ref, h_ref):
    @pl.when(pl.program_id(1) == 0)
    def _():
        h_ref[...] = jnp.zeros_like(h_ref)

    c = RWKV_CHUNK
    n_sub = rf.shape[1] // c
    dirs = ((rf, lwf, kf, vf, kkf, bf, True), (rb, lwb, kb, vb, kkb, bb, False))
    n_pairs = GROUP_W // LANE
    keys, chains = [], []
    for d, (r_, lw_, k_, v_, kk_, b_, fwd) in enumerate(dirs):
        for p in range(n_pairs):
            sl = slice(p * LANE, (p + 1) * LANE)
            for j in range(n_sub):
                rows = slice(j * c, (j + 1) * c)
                keys.append((d, p, j))
                chains.append((r_[0, rows, sl], lw_[0, rows, sl], k_[0, rows, sl], v_[0, rows, sl],
                               kk_[0, rows, sl], b_[0, rows, sl], fwd))
    prep = dict(zip(keys, _rwkv_prepare(chains)))
    dp = [(d, p) for d in range(2) for p in range(n_pairs)]
    hts = [h_ref[d, p] for d, p in dp]
    for step in range(n_sub):
        subs = [step if d == 0 else n_sub - 1 - step for d, _ in dp]
        outs = _rwkv_apply([prep[(d, p, j)] for (d, p), j in zip(dp, subs)], hts)
        hts = [hn for _, hn in outs]
        for (d, p), j, (y, _) in zip(dp, subs, outs):
            (yf_ref, yb_ref)[d][0, j * c:(j + 1) * c, p * LANE:(p + 1) * LANE] = y
    for (d, p), hn in zip(dp, hts):
        h_ref[d, p] = hn


def _rwkv_scan(r, k, v, kk, b, lwf, lwb):
    bsz, s, _ = r.shape
    c = RWKV_CHUNK
    rows = min(RWKV_SUBCHUNKS * c, s)
    nch = s // rows
    fs = pl.BlockSpec((1, rows, GROUP_W), lambda i, n: (i, n, 0))
    bs = pl.BlockSpec((1, rows, GROUP_W), lambda i, n: (i, nch - 1 - n, 0))
    osh = jax.ShapeDtypeStruct((bsz, s, GROUP_W), F32)
    return pl.pallas_call(
        _rwkv_scan_kernel,
        name="rwkv_scan",
        grid=(bsz, nch),
        in_specs=[fs] * 6 + [bs] * 6,
        out_specs=[fs, bs],
        out_shape=[osh, osh],
        scratch_shapes=[pltpu.VMEM((2, GROUP_W // LANE, LANE, LANE), F32)],
        compiler_params=_cparams(("parallel", "arbitrary")),
    )(r, k, v, kk, b, lwf, r, k, v, kk, b, lwb)


def _rwkv_post_kernel(yf_ref, yb_ref, r_ref, k_ref, v_ref, g_ref, rk_ref, lnw_ref, lnb_ref, bd_ref, o_ref):
    bd = bd_ref[...]
    v = v_ref[...]
    y = yf_ref[...] + yb_ref[...] + _seg_sum(r_ref[...] * k_ref[...] * rk_ref[...], bd) * v
    mean = _seg_sum(y, bd) * (1.0 / RWKV_HD)
    yc = y - mean
    var = _seg_sum(yc * yc, bd) * (1.0 / RWKV_HD)
    y = yc * lax.rsqrt(var + RWKV_LN_EPS) * lnw_ref[...] + lnb_ref[...]
    o_ref[...] = (y * g_ref[...]).astype(BF16)


def _rwkv_post(yf, yb, r, k, v, g, rk, lnw, lnb, bd):
    t = yf.shape[0]
    tm = min(1024, t)
    big = pl.BlockSpec((tm, GROUP_W), lambda i: (i, 0))
    vec = pl.BlockSpec((1, GROUP_W), lambda i: (0, 0))
    return pl.pallas_call(
        _rwkv_post_kernel,
        name="rwkv_post",
        grid=(t // tm,),
        in_specs=[big] * 6 + [vec] * 3 + [pl.BlockSpec((GROUP_W, GROUP_W), lambda i: (0, 0))],
        out_specs=big,
        out_shape=jax.ShapeDtypeStruct((t, GROUP_W), BF16),
        compiler_params=_cparams(("parallel",)),
    )(yf, yb, r, k, v, g, rk, lnw, lnb, bd)


def _outproj_kernel(ya_ref, yb_ref, yc_ref, yd_ref, ga_ref, gb_ref, w_ref, x_ref, o_ref):
    ya = ya_ref[...].astype(F32)
    ya = ya * lax.rsqrt(jnp.mean(ya * ya, axis=-1, keepdims=True) + EPS) * ga_ref[...]
    yb = yb_ref[...].astype(F32)
    yb = yb * lax.rsqrt(jnp.mean(yb * yb, axis=-1, keepdims=True) + EPS) * gb_ref[...]
    cat = jnp.concatenate([ya.astype(BF16), yb.astype(BF16), yc_ref[...], yd_ref[...]], axis=-1)
    o_ref[...] = x_ref[...] + jnp.dot(cat, w_ref[...], preferred_element_type=F32)


def _outproj(ya, yb, yc, yd, ga, gb, w, x2d):
    t, d = x2d.shape
    tm = min(512, t)
    ysp = pl.BlockSpec((tm, GROUP_W), lambda i: (i, 0))
    gsp = pl.BlockSpec((1, GROUP_W), lambda i: (0, 0))
    return pl.pallas_call(
        _outproj_kernel,
        name="outproj",
        grid=(t // tm,),
        in_specs=[ysp, ysp, ysp, ysp, gsp, gsp,
                  pl.BlockSpec((d, d), lambda i: (0, 0)),
                  pl.BlockSpec((tm, d), lambda i: (i, 0))],
        out_specs=pl.BlockSpec((tm, d), lambda i: (i, 0)),
        out_shape=jax.ShapeDtypeStruct((t, d), F32),
        compiler_params=_cparams(("parallel",)),
    )(ya, yb, yc, yd, ga, gb, w, x2d)


def _ffn_kernel(x_ref, g_ref, wg_ref, wu_ref, wd_ref, o_ref, h_ref, acc_ref):
    j = pl.program_id(1)

    @pl.when(j == 0)
    def _():
        x = x_ref[...]
        ms = jnp.mean(x * x, axis=-1, keepdims=True)
        h_ref[...] = (x * lax.rsqrt(ms + EPS) * g_ref[...]).astype(BF16)
        acc_ref[...] = x

    h = h_ref[...]
    gt = jnp.dot(h, wg_ref[...], preferred_element_type=F32)
    up = jnp.dot(h, wu_ref[...], preferred_element_type=F32)
    act = (gt * _sigmoid(gt) * up).astype(BF16)
    acc_ref[...] += jnp.dot(act, wd_ref[...], preferred_element_type=F32)

    @pl.when(j == pl.num_programs(1) - 1)
    def _():
        o_ref[...] = acc_ref[...]


def _ffn(x2d, gain, wg, wu, wd):
    t, d = x2d.shape
    f = wg.shape[1]
    tm = min(512, t)
    tf = 512
    return pl.pallas_call(
        _ffn_kernel,
        name="ffn",
        grid=(t // tm, f // tf),
        in_specs=[pl.BlockSpec((tm, d), lambda i, j: (i, 0)),
                  pl.BlockSpec((1, d), lambda i, j: (0, 0)),
                  pl.BlockSpec((d, tf), lambda i, j: (0, j)),
                  pl.BlockSpec((d, tf), lambda i, j: (0, j)),
                  pl.BlockSpec((tf, d), lambda i, j: (j, 0))],
        out_specs=pl.BlockSpec((tm, d), lambda i, j: (i, 0)),
        out_shape=jax.ShapeDtypeStruct((t, d), F32),
        scratch_shapes=[pltpu.VMEM((tm, d), BF16), pltpu.VMEM((tm, d), F32)],
        compiler_params=_cparams(("parallel", "arbitrary")),
    )(x2d, gain, wg, wu, wd)


def _block_diag_ones(n, seg):
    i = jnp.arange(n) // seg
    return (i[:, None] == i[None, :]).astype(BF16)


def _rope_tables(n_tok, head_dim, n_copies):
    half = head_dim // 2
    quarter = half // 2
    rows = (jnp.arange(n_tok, dtype=jnp.int32) // GRID_W).astype(F32)
    cols = (jnp.arange(n_tok, dtype=jnp.int32) % GRID_W).astype(F32)
    inv = 1.0 / (ROPE_BASE ** (jnp.arange(0, half, 2, dtype=F32) / half))
    ang_r = rows[:, None] * inv[None, :]
    ang_c = cols[:, None] * inv[None, :]
    cos = jnp.concatenate([jnp.cos(ang_r)] * 2 + [jnp.cos(ang_c)] * 2, axis=-1)
    sin = jnp.concatenate([-jnp.sin(ang_r), jnp.sin(ang_r), -jnp.sin(ang_c), jnp.sin(ang_c)], axis=-1)
    del quarter
    return jnp.tile(cos, (1, n_copies)), jnp.tile(sin, (1, n_copies))


def _pair_block_diag(w):
    lead = w.shape[:-3]
    w = w.reshape(lead + (4, 2, LRU_BD, LRU_BD))
    z = jnp.zeros_like(w[..., 0, :, :])
    top = jnp.concatenate([w[..., 0, :, :], z], axis=-1)
    bot = jnp.concatenate([z, w[..., 1, :, :]], axis=-1)
    return jnp.concatenate([top, bot], axis=-2)


def _lru_gate_weights(wa, wx):
    pa, px = _pair_block_diag(wa), _pair_block_diag(wx)
    return jnp.concatenate([pa[0], px[0], pa[1], px[1]], axis=-1)


def _split_w(w):
    hi = w.astype(BF16)
    return hi, (w - hi.astype(F32)).astype(BF16)


def _prep_layer(p):
    w_in = p["w_in"]
    d = w_in.shape[0]
    q = w_in[:, :512]
    kx = w_in[:, 512:640]
    vx = w_in[:, 640:768]
    zero64 = jnp.zeros((d, 64), F32)
    dm = w_in[:, OFF_D:]
    d_part = jnp.concatenate([dm[:, :D_MAIN + 192], zero64, dm[:, D_MAIN + 192:]], axis=1)
    w_z = jnp.concatenate([q, kx, vx, w_in[:, OFF_B:OFF_D], d_part], axis=1).astype(BF16)

    mu = p["rwkv_mu"]
    mu_pad = jnp.concatenate([mu[:D_MAIN + 192], jnp.zeros((64,), F32), mu[D_MAIN + 192:]])[None, :]
    w2 = p["rwkv_w2"]
    zl = jnp.zeros_like(w2[0])
    w2cat = jnp.concatenate([jnp.concatenate([w2[0], zl], axis=1),
                             jnp.concatenate([zl, w2[1]], axis=1)], axis=0)
    a2pad = jnp.concatenate([p["rwkv_a2"], jnp.zeros_like(p["rwkv_a2"])], axis=0)
    out = {
        "norm_mix": p["norm_mix"][None, :],
        "w_z": w_z,
        "gq": jnp.tile(p["attn_q_norm"] * (ATT_HD ** -0.5), ATT_HQ)[None, :],
        "gk": jnp.tile(p["attn_k_norm"], ATT_HKV)[None, :],
        "ga": p["attn_out_norm"][None, :],
        "cw": p["lru_conv_w"], "cb": p["lru_conv_b"][None, :],
        "w_gates": _split_w(_lru_gate_weights(p["lru_wa"], p["lru_wx"])),
        "ba": p["lru_ba"], "bx": p["lru_bx"], "lam": p["lru_lambda"],
        "gb": p["lru_out_norm"][None, :],
        "ret_gain": p["ret_norm"][None, :],
        "mu": mu_pad,
        "w2": _split_w(w2cat), "a2": _split_w(a2pad), "g2": _split_w(p["rwkv_g2"]),
        "w0": p["rwkv_w0"].reshape(1, 2 * GROUP_W), "a0": p["rwkv_a0"][None, :],
        "k_k": p["rwkv_k_k"][None, :], "k_a": p["rwkv_k_a"][None, :],
        "r_k": p["rwkv_r_k"].reshape(1, GROUP_W),
        "ln_w": p["rwkv_ln_w"][None, :], "ln_b": p["rwkv_ln_b"][None, :],
        "w_out": p["w_out"].astype(BF16),
        "norm_ffn": p["norm_ffn"][None, :],
        "w_gate": p["w_gate"].astype(BF16), "w_up": p["w_up"].astype(BF16),
        "w_down": p["w_down"].astype(BF16),
    }
    return out


def _layer(x, lp, consts):
    bsz, s, d = x.shape
    t = bsz * s
    x2d = x.reshape(t, d)
    z = _inproj(x2d, lp["norm_mix"], lp["w_z"]).reshape(bsz, s, Z_COLS)

    ya = _attention(z, consts["att_tabs"], lp["gq"], lp["gk"], consts["bd512"], consts["bd128"])
    yb = _rglru(z, lp["cw"], lp["cb"], lp["w_gates"][0], lp["w_gates"][1], lp["ba"], lp["bx"], lp["lam"])
    yc = _retention(z, consts["ret_cos"], consts["ret_sin"], consts["ret_lg"], lp["ret_gain"])
    r, k, v, kk, b, lwf, lwb, g = _rwkv_prep(
        z, lp["mu"], lp["w2"][0], lp["w2"][1], lp["a2"][0], lp["a2"][1], lp["g2"][0], lp["g2"][1],
        lp["w0"], lp["a0"], lp["k_k"], lp["k_a"], consts["bd512"])
    yf, ybw = _rwkv_scan(r, k, v, kk, b, lwf, lwb)
    f2 = lambda a: a.reshape(t, GROUP_W)
    yd = _rwkv_post(f2(yf), f2(ybw), f2(r), f2(k), f2(v), f2(g), lp["r_k"], lp["ln_w"], lp["ln_b"],
                    consts["bd512"])
    x2d = _outproj(f2(ya), f2(yb), f2(yc), yd, lp["ga"], lp["gb"], lp["w_out"], x2d)
    x2d = _ffn(x2d, lp["norm_ffn"], lp["w_gate"], lp["w_up"], lp["w_down"])
    return x2d.reshape(bsz, s, d)


def _consts(s):
    cq, sq = _rope_tables(s, ATT_HD, ATT_HQ)
    ck, sk = _rope_tables(s, ATT_HD, ATT_HKV)
    rc, rs = _rope_tables(s, RET_HD, 1)
    log_g = jnp.log1p(-jnp.exp2(-5.0 - jnp.arange(RET_H, dtype=F32)))
    return {
        "att_tabs": (cq, sq, ck, sk),
        "ret_cos": rc, "ret_sin": rs,
        "ret_lg": jnp.broadcast_to(log_g[:, None, None], (RET_H, 1, LANE)),
        "bd512": _block_diag_ones(GROUP_W, 64),
        "bd128": _block_diag_ones(LANE, 64),
    }


def _trunk(x, layers, consts):
    for lp in layers:
        x = _layer(x, lp, consts)
    return x


def kernel(x_prompt, x_sample, norm_mix, w_in, attn_q_norm, attn_k_norm, attn_out_norm, lru_conv_w, lru_conv_b, lru_wa, lru_ba, lru_wx, lru_bx, lru_lambda, lru_out_norm, ret_norm, rwkv_mu, rwkv_w0, rwkv_w2, rwkv_a0, rwkv_a2, rwkv_g2, rwkv_k_k, rwkv_k_a, rwkv_r_k, rwkv_ln_w, rwkv_ln_b, w_out, norm_ffn, w_gate, w_up, w_down):
    params = {
        "norm_mix": norm_mix, "w_in": w_in,
        "attn_q_norm": attn_q_norm, "attn_k_norm": attn_k_norm, "attn_out_norm": attn_out_norm,
        "lru_conv_w": lru_conv_w, "lru_conv_b": lru_conv_b, "lru_wa": lru_wa, "lru_ba": lru_ba,
        "lru_wx": lru_wx, "lru_bx": lru_bx, "lru_lambda": lru_lambda, "lru_out_norm": lru_out_norm,
        "ret_norm": ret_norm,
        "rwkv_mu": rwkv_mu, "rwkv_w0": rwkv_w0, "rwkv_w2": rwkv_w2, "rwkv_a0": rwkv_a0,
        "rwkv_a2": rwkv_a2, "rwkv_g2": rwkv_g2, "rwkv_k_k": rwkv_k_k, "rwkv_k_a": rwkv_k_a,
        "rwkv_r_k": rwkv_r_k, "rwkv_ln_w": rwkv_ln_w, "rwkv_ln_b": rwkv_ln_b,
        "w_out": w_out, "norm_ffn": norm_ffn, "w_gate": w_gate, "w_up": w_up, "w_down": w_down,
    }
    depth = w_in.shape[0]
    layers = [_prep_layer({name: arr[l] for name, arr in params.items()}) for l in range(depth)]
    y_prompt = _trunk(x_prompt, layers, _consts(x_prompt.shape[1]))
    y_sample = _trunk(x_sample, layers, _consts(x_sample.shape[1]))
    return (y_prompt, y_sample)
```

```python
import functools

import jax
import jax.numpy as jnp
from jax import lax
from jax.experimental import pallas as pl
from jax.experimental.pallas import tpu as pltpu

F32 = jnp.float32
BF16 = jnp.bfloat16

D_MODEL = 2048
GRID_W = 64
EPS = 1e-6
GROUP_W = 512
ATT_HD = 64
ATT_HQ = 8
ATT_HKV = 2
ROPE_BASE = 10000.0
LRU_BD = 64
LRU_C = 8.0
CONV_W = 4
CONV_LEFT = 2
RET_HD = 128
RET_H = 4
RWKV_HD = 64
RWKV_LN_EPS = 64e-5
D_FF = 5632

A_COLS = 768
B_COLS = 1024
C_COLS = 2048
D_MAIN = 1536
OFF_B = A_COLS
OFF_C = OFF_B + B_COLS
OFF_D = OFF_C + C_COLS
D_COLS_PAD = 1920
Z_COLS = OFF_D + D_COLS_PAD

LANE = 128
VMEM_LIMIT = 56 * 1024 * 1024

RET_CHUNK = 256
RWKV_CHUNK = 64
RWKV_SUBCHUNKS = 4


def _cparams(sem):
    return pltpu.CompilerParams(dimension_semantics=sem, vmem_limit_bytes=VMEM_LIMIT)


def _dot(a, b):
    return jnp.dot(a.astype(BF16), b.astype(BF16), preferred_element_type=F32)


def _dot_nt(a, b):
    return lax.dot_general(a.astype(BF16), b.astype(BF16), (((1,), (1,)), ((), ())),
                           preferred_element_type=F32)


def _split(x):
    hi = x.astype(BF16)
    lo = (x - hi.astype(F32)).astype(BF16)
    return hi, lo


def _dot3(x, wh, wl):
    xh, xl = _split(x)
    return (jnp.dot(xh, wh, preferred_element_type=F32)
            + jnp.dot(xl, wh, preferred_element_type=F32)
            + jnp.dot(xh, wl, preferred_element_type=F32))


def _seg_sum(x, bd):
    xh, xl = _split(x)
    return jnp.dot(xh, bd, preferred_element_type=F32) + jnp.dot(xl, bd, preferred_element_type=F32)


def _rope(x, cos, sin_signed, half):
    n = x.shape[-1]
    lane = lax.broadcasted_iota(jnp.int32, x.shape, 1)
    first = (lane % (2 * half)) < half
    rot = jnp.where(first, pltpu.roll(x, n - half, 1), pltpu.roll(x, half, 1))
    return x * cos + rot * sin_signed


def _softplus(x):
    return jnp.maximum(x, 0.0) + jnp.log1p(jnp.exp(-jnp.abs(x)))


def _sigmoid(x):
    return 0.5 * jnp.tanh(0.5 * x) + 0.5


def _rmsnorm_kernel(x_ref, g_ref, o_ref):
    x = x_ref[...]
    ms = jnp.mean(x * x, axis=-1, keepdims=True)
    o_ref[...] = (x * lax.rsqrt(ms + EPS) * g_ref[...]).astype(BF16)


def _rmsnorm(x2d, gain):
    t, d = x2d.shape
    tm = min(1024, t)
    return pl.pallas_call(
        _rmsnorm_kernel,
        name="rmsnorm",
        grid=(t // tm,),
        in_specs=[pl.BlockSpec((tm, d), lambda i: (i, 0)),
                  pl.BlockSpec((1, d), lambda i: (0, 0))],
        out_specs=pl.BlockSpec((tm, d), lambda i: (i, 0)),
        out_shape=jax.ShapeDtypeStruct((t, d), BF16),
        compiler_params=_cparams(("parallel",)),
    )(x2d, gain)


def _inproj_kernel(h_ref, w_ref, o_ref):
    o_ref[...] = jnp.dot(h_ref[...], w_ref[...], preferred_element_type=F32).astype(BF16)


def _inproj(x2d, gain, w):
    h = _rmsnorm(x2d, gain)
    t, d = h.shape
    n = w.shape[1]
    tm = min(1024, t)
    tn = 1920
    return pl.pallas_call(
        _inproj_kernel,
        name="inproj",
        grid=(t // tm, n // tn),
        in_specs=[pl.BlockSpec((tm, d), lambda i, j: (i, 0)),
                  pl.BlockSpec((d, tn), lambda i, j: (0, j))],
        out_specs=pl.BlockSpec((tm, tn), lambda i, j: (i, j)),
        out_shape=jax.ShapeDtypeStruct((t, n), BF16),
        compiler_params=_cparams(("parallel", "arbitrary")),
    )(h, w)


def _attn_kernel(q_ref, kv_ref, cq_ref, sq_ref, ck_ref, sk_ref, gq_ref, gk_ref, bdq_ref, bdk_ref,
                 o_ref, k_s, v_s):
    @pl.when(pl.program_id(1) == 0)
    def _():
        kv = kv_ref[0]
        k = kv[:, :LANE].astype(F32)
        ms = _seg_sum(k * k, bdk_ref[...]) * (1.0 / ATT_HD)
        k = k * lax.rsqrt(ms + EPS) * gk_ref[...]
        k = _rope(k, ck_ref[...], sk_ref[...], ATT_HD // 4)
        ones = jnp.ones((kv.shape[0], ATT_HD), BF16)
        for g in range(ATT_HKV):
            k_s[g] = k[:, g * ATT_HD:(g + 1) * ATT_HD].astype(BF16)
            v_s[g] = jnp.concatenate([kv[:, LANE + g * ATT_HD:LANE + (g + 1) * ATT_HD], ones], axis=-1)

    q = q_ref[0].astype(F32)
    ms = _seg_sum(q * q, bdq_ref[...]) * (1.0 / ATT_HD)
    q = q * lax.rsqrt(ms + EPS) * gq_ref[...]
    q = _rope(q, cq_ref[...], sq_ref[...], ATT_HD // 4).astype(BF16)
    n_rep = ATT_HQ // ATT_HKV
    scores = lambda h: _dot_nt(q[:, h * ATT_HD:(h + 1) * ATT_HD], k_s[h // n_rep])
    s_next = scores(0)
    for h in range(ATT_HQ):
        g = h // n_rep
        s = s_next
        if h + 1 < ATT_HQ:
            s_next = scores(h + 1)
        m = jnp.max(s, axis=-1, keepdims=True)
        p = jnp.exp((s - m).astype(BF16))
        o = jnp.dot(p, v_s[g], preferred_element_type=F32)
        o_ref[0, :, h * ATT_HD:(h + 1) * ATT_HD] = (o[:, :ATT_HD] / o[:, ATT_HD:]).astype(BF16)


def _attention(z, tabs, gq, gk, bdq, bdk):
    b, s, _ = z.shape
    tq = min(256, s)
    cq, sq, ck, sk = tabs
    return pl.pallas_call(
        _attn_kernel,
        name="attn",
        grid=(b, s // tq),
        in_specs=[pl.BlockSpec((1, tq, 512), lambda i, j: (i, j, 0)),
                  pl.BlockSpec((1, s, 256), lambda i, j: (i, 0, 2)),
                  pl.BlockSpec((tq, 512), lambda i, j: (j, 0)),
                  pl.BlockSpec((tq, 512), lambda i, j: (j, 0)),
                  pl.BlockSpec((s, LANE), lambda i, j: (0, 0)),
                  pl.BlockSpec((s, LANE), lambda i, j: (0, 0)),
                  pl.BlockSpec((1, 512), lambda i, j: (0, 0)),
                  pl.BlockSpec((1, LANE), lambda i, j: (0, 0)),
                  pl.BlockSpec((512, 512), lambda i, j: (0, 0)),
                  pl.BlockSpec((LANE, LANE), lambda i, j: (0, 0))],
        out_specs=pl.BlockSpec((1, tq, 512), lambda i, j: (i, j, 0)),
        out_shape=jax.ShapeDtypeStruct((b, s, GROUP_W), BF16),
        scratch_shapes=[pltpu.VMEM((ATT_HKV, s, ATT_HD), BF16),
                        pltpu.VMEM((ATT_HKV, s, 2 * ATT_HD), BF16)],
        compiler_params=_cparams(("parallel", "arbitrary")),
    )(z, z, cq, sq, ck, sk, gq, gk, bdq, bdk)


def _shift_rows(v, d, fill, row):
    n = v.shape[0]
    if d > 0:
        return jnp.where(row >= d, pltpu.roll(v, d, 0), fill)
    return jnp.where(row < n + d, pltpu.roll(v, n + d, 0), fill)


SUBLANES = 8


def _group_scan(a, b, h_in, sub, reverse):
    k = 1
    while k < SUBLANES:
        if reverse:
            keep = sub <= SUBLANES - 1 - k
            b = a * jnp.where(keep, pltpu.roll(b, SUBLANES - k, 0), 0.0) + b
            a = a * jnp.where(keep, pltpu.roll(a, SUBLANES - k, 0), 1.0)
        else:
            keep = sub >= k
            b = a * jnp.where(keep, pltpu.roll(b, k, 0), 0.0) + b
            a = a * jnp.where(keep, pltpu.roll(a, k, 0), 1.0)
        k *= 2
    return a * h_in + b


def _lru_kernel(x_ref, gate_ref, cw_ref, cb_ref, wgh_ref, wgl_ref,
                ba_ref, bx_ref, lam_ref, o_ref, a_s, b_s, h_s):
    x = x_ref[0].astype(F32)
    s_len = x.shape[0]
    n_grp = s_len // SUBLANES
    row = lax.broadcasted_iota(jnp.int32, x.shape, 0)
    xc = cb_ref[...]
    for j in range(CONV_W):
        d = CONV_LEFT - j
        xs = x if d == 0 else _shift_rows(x, d, 0.0, row)
        xc = xc + xs * cw_ref[j:j + 1, :]
    pre = _dot3(xc, wgh_ref[0], wgl_ref[0])
    for d in range(2):
        r = _sigmoid(pre[:, 2 * d * LANE:(2 * d + 1) * LANE] + ba_ref[d:d + 1, :])
        i = _sigmoid(pre[:, (2 * d + 1) * LANE:(2 * d + 2) * LANE] + bx_ref[d:d + 1, :])
        log_a = -LRU_C * r * _softplus(-lam_ref[d:d + 1, :])
        a = jnp.exp(log_a)
        om = -jnp.tanh(log_a) * (a * a + 1.0)
        a_s[d] = a
        b_s[d] = jnp.where(om > 0.0, om * lax.rsqrt(om), 0.0) * (i * xc)

    sub = lax.broadcasted_iota(jnp.int32, (SUBLANES, LANE), 0)

    def body(g, carry):
        hf_in, hb_in = carry
        gf = pl.ds(pl.multiple_of(g * SUBLANES, SUBLANES), SUBLANES)
        gb = pl.ds(pl.multiple_of((n_grp - 1 - g) * SUBLANES, SUBLANES), SUBLANES)
        hf = _group_scan(a_s[0, gf, :], b_s[0, gf, :], hf_in, sub, False)
        hb = _group_scan(a_s[1, gb, :], b_s[1, gb, :], hb_in, sub, True)
        h_s[0, gf, :] = hf
        h_s[1, gb, :] = hb
        return (jnp.broadcast_to(hf[SUBLANES - 1:SUBLANES], hf.shape), jnp.broadcast_to(hb[0:1], hb.shape))

    zero = jnp.zeros((SUBLANES, LANE), F32)
    lax.fori_loop(0, n_grp, body, (zero, zero), unroll=8)
    gate = gate_ref[0].astype(F32)
    gelu = 0.5 * gate * (1.0 + jnp.tanh(0.7978845608028654 * (gate + 0.044715 * gate * gate * gate)))
    o_ref[0] = ((h_s[0] + h_s[1]) * gelu).astype(BF16)


def _rglru(z, cw, cb, wgh, wgl, ba, bx, lam):
    b, s, _ = z.shape
    nc = GROUP_W // LANE
    xo = OFF_B // LANE
    go = (OFF_B + GROUP_W) // LANE
    vec = lambda r: pl.BlockSpec((r, LANE), lambda i, c: (0, c))
    wsp = pl.BlockSpec((1, LANE, 4 * LANE), lambda i, c: (c, 0, 0))
    return pl.pallas_call(
        _lru_kernel,
        name="rglru",
        grid=(b, nc),
        in_specs=[pl.BlockSpec((1, s, LANE), lambda i, c: (i, 0, xo + c)),
                  pl.BlockSpec((1, s, LANE), lambda i, c: (i, 0, go + c)),
                  vec(CONV_W), vec(1), wsp, wsp, vec(2), vec(2), vec(2)],
        out_specs=pl.BlockSpec((1, s, LANE), lambda i, c: (i, 0, c)),
        out_shape=jax.ShapeDtypeStruct((b, s, GROUP_W), BF16),
        scratch_shapes=[pltpu.VMEM((2, s, LANE), F32)] * 3,
        compiler_params=_cparams(("parallel", "parallel")),
    )(z, z, cw, cb, wgh, wgl, ba, bx, lam)


def _ret_kernel(q_ref, k_ref, v_ref, g_ref, cos_ref, sin_ref, lg_ref, gain_ref, o_ref,
                q_s, k_s, v_s, acc, st):
    s_len = q_ref.shape[1]
    c = min(RET_CHUNK, s_len)
    nch = s_len // c
    half = RET_HD // 4
    q_s[...] = _rope(q_ref[0].astype(F32), cos_ref[...], sin_ref[...], half).astype(BF16)
    k_s[...] = (_rope(k_ref[0].astype(F32), cos_ref[...], sin_ref[...], half) * (RET_HD ** -0.5)).astype(BF16)
    v_s[...] = v_ref[0]

    lg = lg_ref[0]
    ii = lax.broadcasted_iota(jnp.int32, (c, c), 0)
    jj = lax.broadcasted_iota(jnp.int32, (c, c), 1)
    gam = jnp.exp(jnp.abs(ii - jj).astype(F32) * lg[:, :1])
    ri = lax.broadcasted_iota(jnp.int32, (c, LANE), 0).astype(F32)
    qw_f = jnp.exp((ri + 1.0) * lg)
    qw_b = jnp.exp((c - ri) * lg)
    kw_f = jnp.exp((c - 1.0 - ri) * lg)
    kw_b = jnp.exp(ri * lg)
    g_chunk = jnp.exp(c * lg)
    gain = gain_ref[...]

    def kv_inc(kc, vc, kw):
        return jnp.dot((kc.astype(F32) * kw).T.astype(BF16), vc, preferred_element_type=F32)

    st[...] = jnp.zeros_like(st)

    def fwd_body(n, carry):
        sl = pl.ds(pl.multiple_of(n * c, c), c)
        qc, kc, vc = q_s[sl, :], k_s[sl, :], v_s[sl, :]
        sc = _dot_nt(qc, kc) * gam
        o = jnp.dot(sc.astype(BF16), vc, preferred_element_type=F32)
        o = o + jnp.dot(qc, st[...].astype(BF16), preferred_element_type=F32) * qw_f
        acc[sl, :] = o
        st[...] = g_chunk * st[...] + kv_inc(kc, vc, kw_f)
        return carry

    lax.fori_loop(0, nch, fwd_body, 0, unroll=8)
    st[...] = jnp.zeros_like(st)

    def bwd_body(m, carry):
        n = nch - 1 - m
        sl = pl.ds(pl.multiple_of(n * c, c), c)
        qc, kc, vc = q_s[sl, :], k_s[sl, :], v_s[sl, :]
        o = acc[sl, :] + jnp.dot(qc, st[...].astype(BF16), preferred_element_type=F32) * qw_b
        st[...] = g_chunk * st[...] + kv_inc(kc, vc, kw_b)
        ms = jnp.mean(o * o, axis=-1, keepdims=True)
        o = o * lax.rsqrt(ms + EPS) * gain
        gate = g_ref[0, sl, :].astype(F32)
        o_ref[0, sl, :] = (o * (gate * _sigmoid(gate))).astype(BF16)
        return carry

    lax.fori_loop(0, nch, bwd_body, 0, unroll=8)


def _retention(z, cos, sin, lg, gain):
    b, s, _ = z.shape
    base = OFF_C // LANE
    zs = lambda o: pl.BlockSpec((1, s, LANE), lambda i, h: (i, 0, base + o + h))
    return pl.pallas_call(
        _ret_kernel,
        name="retention",
        grid=(b, RET_H),
        in_specs=[zs(0), zs(RET_H), zs(2 * RET_H), zs(3 * RET_H),
                  pl.BlockSpec((s, LANE), lambda i, h: (0, 0)),
                  pl.BlockSpec((s, LANE), lambda i, h: (0, 0)),
                  pl.BlockSpec((1, 1, LANE), lambda i, h: (h, 0, 0)),
                  pl.BlockSpec((1, LANE), lambda i, h: (0, h))],
        out_specs=pl.BlockSpec((1, s, LANE), lambda i, h: (i, 0, h)),
        out_shape=jax.ShapeDtypeStruct((b, s, GROUP_W), BF16),
        scratch_shapes=[pltpu.VMEM((s, LANE), BF16), pltpu.VMEM((s, LANE), BF16),
                        pltpu.VMEM((s, LANE), BF16), pltpu.VMEM((s, LANE), F32),
                        pltpu.VMEM((LANE, LANE), F32)],
        compiler_params=_cparams(("parallel", "parallel")),
    )(z, z, z, z, cos, sin, lg, gain)


def _rwkv_prep_kernel(z_ref, zp_ref, zn_ref, mu_ref, w2h_ref, w2l_ref, a2h_ref, a2l_ref,
                      g2h_ref, g2l_ref, w0_ref, a0_ref, kk_ref, ka_ref, bd_ref,
                      r_o, k_o, v_o, kk_o, b_o, lwf_o, lwb_o, g_o):
    i = pl.program_id(1)
    last = pl.num_programs(1) - 1
    z = z_ref[0].astype(F32)
    tt = z.shape[0]
    hr = zp_ref.shape[1]
    prev_row = jnp.where(i > 0, zp_ref[0].astype(F32)[hr - 1:hr, :], 0.0)
    next_row = jnp.where(i < last, zn_ref[0].astype(F32)[0:1, :], 0.0)
    row = lax.broadcasted_iota(jnp.int32, z.shape, 0)
    prev = jnp.where(row == 0, prev_row, pltpu.roll(z, 1, 0))
    nxt = jnp.where(row == tt - 1, next_row, pltpu.roll(z, tt - 1, 0))
    zs = z + (0.5 * (prev + nxt) - z) * mu_ref[...]
    g = GROUP_W
    r, k, v = zs[:, :g], zs[:, g:2 * g], zs[:, 2 * g:3 * g]
    lora_w = _dot3(jnp.tanh(zs[:, D_MAIN:D_MAIN + LANE]), w2h_ref[...], w2l_ref[...])
    lw = -0.6065306597126334 * _sigmoid(w0_ref[...] + lora_w)
    a = _sigmoid(a0_ref[...] + _dot3(zs[:, D_MAIN + LANE:D_MAIN + 2 * LANE], a2h_ref[...], a2l_ref[...]))
    gg = _dot3(_sigmoid(zs[:, D_MAIN + 2 * LANE:]), g2h_ref[...], g2l_ref[...])
    kk = k * kk_ref[...]
    kk = kk * lax.rsqrt(_seg_sum(kk * kk, bd_ref[...]) + 1e-12)
    r_o[0] = r
    k_o[0] = k * (1.0 + (a - 1.0) * ka_ref[...])
    v_o[0] = v
    kk_o[0] = kk
    b_o[0] = kk * a
    lwf_o[0] = lw[:, :g]
    lwb_o[0] = lw[:, g:]
    g_o[0] = gg


def _rwkv_prep(z, mu, w2h, w2l, a2h, a2l, g2h, g2l, w0, a0, k_k, k_a, bd):
    b, s, _ = z.shape
    tt = min(256, s)
    hr = 16
    nbh = s // hr
    cb = OFF_D // D_COLS_PAD
    full = lambda a: pl.BlockSpec(a.shape, lambda i, j: (0,) * a.ndim)
    osp = pl.BlockSpec((1, tt, GROUP_W), lambda i, j: (i, j, 0))
    osh = jax.ShapeDtypeStruct((b, s, GROUP_W), F32)
    return pl.pallas_call(
        _rwkv_prep_kernel,
        name="rwkv_prep",
        grid=(b, s // tt),
        in_specs=[pl.BlockSpec((1, tt, D_COLS_PAD), lambda i, j: (i, j, cb)),
                  pl.BlockSpec((1, hr, D_COLS_PAD), lambda i, j: (i, jnp.maximum(j * (tt // hr) - 1, 0), cb)),
                  pl.BlockSpec((1, hr, D_COLS_PAD), lambda i, j: (i, jnp.minimum((j + 1) * (tt // hr), nbh - 1), cb)),
                  full(mu), full(w2h), full(w2l), full(a2h), full(a2l), full(g2h), full(g2l),
                  full(w0), full(a0), full(k_k), full(k_a), full(bd)],
        out_specs=[osp] * 8,
        out_shape=[osh] * 8,
        compiler_params=_cparams(("parallel", "parallel")),
    )(z, z, z, mu, w2h, w2l, a2h, a2l, g2h, g2l, w0, a0, k_k, k_a, bd)


def _rwkv_prepare(chains):
    c = RWKV_CHUNK
    hd = RWKV_HD
    row = lax.broadcasted_iota(jnp.int32, (c, LANE), 0)
    lane = lax.broadcasted_iota(jnp.int32, (c, LANE), 1)
    head0 = lane < hd
    col = lane % c
    m_strict = {True: row > col, False: row < col}
    m_y = {True: row >= col, False: m_strict[False]}
    eye = (row == col).astype(F32)
    bf = lambda t: t.astype(BF16)
    cat = lambda ts: jnp.concatenate(ts, axis=0)
    per_head = functools.partial(_per_head, head0=head0)
    fwds = [ch[6] for ch in chains]

    def decays(ch):
        r, lw, k, v, kk, b, fwd = ch
        cum = lw
        d = 1
        while d < c:
            cum = cum + _shift_rows(cum, d if fwd else -d, 0.0, row)
            d *= 2
        cum_ex = cum - lw
        mid = cum[c // 2:c // 2 + 1, :]
        tot = cum[c - 1:c, :] if fwd else cum[0:1, :]
        e_ex = jnp.exp(cum_ex)
        c_hat = jnp.exp(cum_ex - mid)
        c_til = jnp.exp(mid - cum)
        c_end = jnp.exp(tot - cum)
        kk_hat = kk * c_hat
        if fwd:
            r0 = r * jnp.exp(cum)
            r_hat = r * jnp.exp(cum - mid)
        else:
            r0 = r * e_ex
            r_hat = r * c_hat
        lhs = cat([bf(kk_hat), bf(r_hat)])
        rhs = cat([per_head(bf(k * c_til)), per_head(bf(b * c_til))])
        q0 = cat([bf(kk * e_ex), bf(r0)])
        rhs_e = cat([bf(k * c_end), bf(b * c_end)])
        return lhs, rhs, q0, rhs_e, per_head(bf(v)), jnp.exp(tot)

    st = [decays(ch) for ch in chains]
    gram = [lax.dot_general(s[0], s[1], (((1,), (1,)), ((), ())), preferred_element_type=F32) for s in st]
    n2 = 2 * c
    zf = jnp.zeros((c, n2), F32)
    n_mat = [jnp.where(m_strict[f], g[:c, n2:], zf) for g, f in zip(gram, fwds)]
    mk_ark = [cat([bf(jnp.where(m_strict[f], g[:c, :n2], zf)), bf(jnp.where(m_y[f], g[c:, :n2], zf))])
              for g, f in zip(gram, fwds)]
    a_rb = [bf(jnp.where(m_y[f], g[c:, n2:], zf)) for g, f in zip(gram, fwds)]

    t_inv = [eye - n for n in n_mat]
    pw = [jnp.dot(bf(n), per_head(bf(n)), preferred_element_type=F32) for n in n_mat]
    step = 4
    while step < c:
        tp = [jnp.dot(cat([bf(t), bf(p)]), per_head(bf(p)), preferred_element_type=F32)
              for t, p in zip(t_inv, pw)]
        t_inv = [t + x[:c] for t, x in zip(t_inv, tp)]
        pw = [x[c:] for x in tp]
        step *= 2
    t_inv = [t + jnp.dot(bf(t), per_head(bf(p)), preferred_element_type=F32) for t, p in zip(t_inv, pw)]
    mv = [jnp.dot(m, s[4], preferred_element_type=F32) for m, s in zip(mk_ark, st)]
    return [(s[2], s[3], s[5], bf(t), m, a, ch[3]) for s, t, m, a, ch in zip(st, t_inv, mv, a_rb, chains)]


def _per_head(t, head0):
    zero = jnp.zeros_like(t)
    return jnp.concatenate([jnp.where(head0, t, zero), jnp.where(head0, zero, t)], axis=0)


def _rwkv_apply(prep, hts):
    c = RWKV_CHUNK
    lane = lax.broadcasted_iota(jnp.int32, (c, LANE), 1)
    per_head = functools.partial(_per_head, head0=lane < RWKV_HD)
    i2 = lax.broadcasted_iota(jnp.int32, (LANE, LANE), 0)
    j2 = lax.broadcasted_iota(jnp.int32, (LANE, LANE), 1)
    head_diag = (i2 // RWKV_HD) == (j2 // RWKV_HD)
    bf = lambda t: t.astype(BF16)
    hx = [lax.dot_general(pr[0], bf(h), (((1,), (1,)), ((), ())), preferred_element_type=F32)
          for pr, h in zip(prep, hts)]
    xs = [x[:c] + pr[4][:c] for x, pr in zip(hx, prep)]
    us = [jnp.dot(pr[3], per_head(bf(x)), preferred_element_type=F32) for pr, x in zip(prep, xs)]
    ys = [pr[4][c:] - jnp.dot(pr[5], per_head(bf(u)), preferred_element_type=F32) + x[c:]
          for pr, u, x in zip(prep, us, hx)]
    lhs_t = [jnp.concatenate([pr[6], -u], axis=0).T for pr, u in zip(prep, us)]
    inc = [jnp.dot(bf(l), pr[1], preferred_element_type=F32) for l, pr in zip(lhs_t, prep)]
    zs = jnp.zeros((LANE, LANE), F32)
    h_new = [h * pr[2] + jnp.where(head_diag, i, zs) for h, pr, i in zip(hts, prep, inc)]
    return list(zip(ys, h_new))


def _rwkv_scan_kernel(rf, kf, vf, kkf, bf, lwf, rb, kb, vb, kkb, bb, lwb, yf_ref, yb_ref, h_ref):
    @pl.when(pl.program_id(1) == 0)
    def _():
        h_ref[...] = jnp.zeros_like(h_ref)

    c = RWKV_CHUNK
    n_sub = rf.shape[1] // c
    dirs = ((rf, lwf, kf, vf, kkf, bf, True), (rb, lwb, kb, vb, kkb, bb, False))
    n_pairs = GROUP_W // LANE
    keys, chains = [], []
    for d, (r_, lw_, k_, v_, kk_, b_, fwd) in enumerate(dirs):
        for p in range(n_pairs):
            sl = slice(p * LANE, (p + 1) * LANE)
            for j in range(n_sub):
                rows = slice(j * c, (j + 1) * c)
                keys.append((d, p, j))
                chains.append((r_[0, rows, sl], lw_[0, rows, sl], k_[0, rows, sl], v_[0, rows, sl],
                               kk_[0, rows, sl], b_[0, rows, sl], fwd))
    prep = dict(zip(keys, _rwkv_prepare(chains)))
    dp = [(d, p) for d in range(2) for p in range(n_pairs)]
    hts = [h_ref[d, p] for d, p in dp]
    for step in range(n_sub):
        subs = [step if d == 0 else n_sub - 1 - step for d, _ in dp]
        outs = _rwkv_apply([prep[(d, p, j)] for (d, p), j in zip(dp, subs)], hts)
        hts = [hn for _, hn in outs]
        for (d, p), j, (y, _) in zip(dp, subs, outs):
            (yf_ref, yb_ref)[d][0, j * c:(j + 1) * c, p * LANE:(p + 1) * LANE] = y
    for (d, p), hn in zip(dp, hts):
        h_ref[d, p] = hn


def _rwkv_scan(r, k, v, kk, b, lwf, lwb):
    bsz, s, _ = r.shape
    c = RWKV_CHUNK
    rows = min(RWKV_SUBCHUNKS * c, s)
    nch = s // rows
    fs = pl.BlockSpec((1, rows, GROUP_W), lambda i, n: (i, n, 0))
    bs = pl.BlockSpec((1, rows, GROUP_W), lambda i, n: (i, nch - 1 - n, 0))
    osh = jax.ShapeDtypeStruct((bsz, s, GROUP_W), F32)
    return pl.pallas_call(
        _rwkv_scan_kernel,
        name="rwkv_scan",
        grid=(bsz, nch),
        in_specs=[fs] * 6 + [bs] * 6,
        out_specs=[fs, bs],
        out_shape=[osh, osh],
        scratch_shapes=[pltpu.VMEM((2, GROUP_W // LANE, LANE, LANE), F32)],
        compiler_params=_cparams(("parallel", "arbitrary")),
    )(r, k, v, kk, b, lwf, r, k, v, kk, b, lwb)


def _rwkv_post_kernel(yf_ref, yb_ref, r_ref, k_ref, v_ref, g_ref, rk_ref, lnw_ref, lnb_ref, bd_ref, o_ref):
    bd = bd_ref[...]
    v = v_ref[...]
    y = yf_ref[...] + yb_ref[...] + _seg_sum(r_ref[...] * k_ref[...] * rk_ref[...], bd) * v
    mean = _seg_sum(y, bd) * (1.0 / RWKV_HD)
    yc = y - mean
    var = _seg_sum(yc * yc, bd) * (1.0 / RWKV_HD)
    y = yc * lax.rsqrt(var + RWKV_LN_EPS) * lnw_ref[...] + lnb_ref[...]
    o_ref[...] = (y * g_ref[...]).astype(BF16)


def _rwkv_post(yf, yb, r, k, v, g, rk, lnw, lnb, bd):
    t = yf.shape[0]
    tm = min(1024, t)
    big = pl.BlockSpec((tm, GROUP_W), lambda i: (i, 0))
    vec = pl.BlockSpec((1, GROUP_W), lambda i: (0, 0))
    return pl.pallas_call(
        _rwkv_post_kernel,
        name="rwkv_post",
        grid=(t // tm,),
        in_specs=[big] * 6 + [vec] * 3 + [pl.BlockSpec((GROUP_W, GROUP_W), lambda i: (0, 0))],
        out_specs=big,
        out_shape=jax.ShapeDtypeStruct((t, GROUP_W), BF16),
        compiler_params=_cparams(("parallel",)),
    )(yf, yb, r, k, v, g, rk, lnw, lnb, bd)


def _outproj_kernel(ya_ref, yb_ref, yc_ref, yd_ref, ga_ref, gb_ref, w_ref, x_ref, o_ref):
    ya = ya_ref[...].astype(F32)
    ya = ya * lax.rsqrt(jnp.mean(ya * ya, axis=-1, keepdims=True) + EPS) * ga_ref[...]
    yb = yb_ref[...].astype(F32)
    yb = yb * lax.rsqrt(jnp.mean(yb * yb, axis=-1, keepdims=True) + EPS) * gb_ref[...]
    cat = jnp.concatenate([ya.astype(BF16), yb.astype(BF16), yc_ref[...], yd_ref[...]], axis=-1)
    o_ref[...] = x_ref[...] + jnp.dot(cat, w_ref[...], preferred_element_type=F32)


def _outproj(ya, yb, yc, yd, ga, gb, w, x2d):
    t, d = x2d.shape
    tm = min(512, t)
    ysp = pl.BlockSpec((tm, GROUP_W), lambda i: (i, 0))
    gsp = pl.BlockSpec((1, GROUP_W), lambda i: (0, 0))
    return pl.pallas_call(
        _outproj_kernel,
        name="outproj",
        grid=(t // tm,),
        in_specs=[ysp, ysp, ysp, ysp, gsp, gsp,
                  pl.BlockSpec((d, d), lambda i: (0, 0)),
                  pl.BlockSpec((tm, d), lambda i: (i, 0))],
        out_specs=pl.BlockSpec((tm, d), lambda i: (i, 0)),
        out_shape=jax.ShapeDtypeStruct((t, d), F32),
        compiler_params=_cparams(("parallel",)),
    )(ya, yb, yc, yd, ga, gb, w, x2d)


def _ffn_kernel(x_ref, g_ref, wg_ref, wu_ref, wd_ref, o_ref, h_ref, acc_ref):
    j = pl.program_id(1)

    @pl.when(j == 0)
    def _():
        x = x_ref[...]
        ms = jnp.mean(x * x, axis=-1, keepdims=True)
        h_ref[...] = (x * lax.rsqrt(ms + EPS) * g_ref[...]).astype(BF16)
        acc_ref[...] = x

    h = h_ref[...]
    gt = jnp.dot(h, wg_ref[...], preferred_element_type=F32)
    up = jnp.dot(h, wu_ref[...], preferred_element_type=F32)
    act = (gt * _sigmoid(gt) * up).astype(BF16)
    acc_ref[...] += jnp.dot(act, wd_ref[...], preferred_element_type=F32)

    @pl.when(j == pl.num_programs(1) - 1)
    def _():
        o_ref[...] = acc_ref[...]


def _ffn(x2d, gain, wg, wu, wd):
    t, d = x2d.shape
    f = wg.shape[1]
    tm = min(512, t)
    tf = 512
    return pl.pallas_call(
        _ffn_kernel,
        name="ffn",
        grid=(t // tm, f // tf),
        in_specs=[pl.BlockSpec((tm, d), lambda i, j: (i, 0)),
                  pl.BlockSpec((1, d), lambda i, j: (0, 0)),
                  pl.BlockSpec((d, tf), lambda i, j: (0, j)),
                  pl.BlockSpec((d, tf), lambda i, j: (0, j)),
                  pl.BlockSpec((tf, d), lambda i, j: (j, 0))],
        out_specs=pl.BlockSpec((tm, d), lambda i, j: (i, 0)),
        out_shape=jax.ShapeDtypeStruct((t, d), F32),
        scratch_shapes=[pltpu.VMEM((tm, d), BF16), pltpu.VMEM((tm, d), F32)],
        compiler_params=_cparams(("parallel", "arbitrary")),
    )(x2d, gain, wg, wu, wd)


def _block_diag_ones(n, seg):
    i = jnp.arange(n) // seg
    return (i[:, None] == i[None, :]).astype(BF16)


def _rope_tables(n_tok, head_dim, n_copies):
    half = head_dim // 2
    quarter = half // 2
    rows = (jnp.arange(n_tok, dtype=jnp.int32) // GRID_W).astype(F32)
    cols = (jnp.arange(n_tok, dtype=jnp.int32) % GRID_W).astype(F32)
    inv = 1.0 / (ROPE_BASE ** (jnp.arange(0, half, 2, dtype=F32) / half))
    ang_r = rows[:, None] * inv[None, :]
    ang_c = cols[:, None] * inv[None, :]
    cos = jnp.concatenate([jnp.cos(ang_r)] * 2 + [jnp.cos(ang_c)] * 2, axis=-1)
    sin = jnp.concatenate([-jnp.sin(ang_r), jnp.sin(ang_r), -jnp.sin(ang_c), jnp.sin(ang_c)], axis=-1)
    del quarter
    return jnp.tile(cos, (1, n_copies)), jnp.tile(sin, (1, n_copies))


def _pair_block_diag(w):
    lead = w.shape[:-3]
    w = w.reshape(lead + (4, 2, LRU_BD, LRU_BD))
    z = jnp.zeros_like(w[..., 0, :, :])
    top = jnp.concatenate([w[..., 0, :, :], z], axis=-1)
    bot = jnp.concatenate([z, w[..., 1, :, :]], axis=-1)
    return jnp.concatenate([top, bot], axis=-2)


def _lru_gate_weights(wa, wx):
    pa, px = _pair_block_diag(wa), _pair_block_diag(wx)
    return jnp.concatenate([pa[0], px[0], pa[1], px[1]], axis=-1)


def _split_w(w):
    hi = w.astype(BF16)
    return hi, (w - hi.astype(F32)).astype(BF16)


def _prep_layer(p):
    w_in = p["w_in"]
    d = w_in.shape[0]
    q = w_in[:, :512]
    kx = w_in[:, 512:640]
    vx = w_in[:, 640:768]
    zero64 = jnp.zeros((d, 64), F32)
    dm = w_in[:, OFF_D:]
    d_part = jnp.concatenate([dm[:, :D_MAIN + 192], zero64, dm[:, D_MAIN + 192:]], axis=1)
    w_z = jnp.concatenate([q, kx, vx, w_in[:, OFF_B:OFF_D], d_part], axis=1).astype(BF16)

    mu = p["rwkv_mu"]
    mu_pad = jnp.concatenate([mu[:D_MAIN + 192], jnp.zeros((64,), F32), mu[D_MAIN + 192:]])[None, :]
    w2 = p["rwkv_w2"]
    zl = jnp.zeros_like(w2[0])
    w2cat = jnp.concatenate([jnp.concatenate([w2[0], zl], axis=1),
                             jnp.concatenate([zl, w2[1]], axis=1)], axis=0)
    a2pad = jnp.concatenate([p["rwkv_a2"], jnp.zeros_like(p["rwkv_a2"])], axis=0)
    out = {
        "norm_mix": p["norm_mix"][None, :],
        "w_z": w_z,
        "gq": jnp.tile(p["attn_q_norm"] * (ATT_HD ** -0.5), ATT_HQ)[None, :],
        "gk": jnp.tile(p["attn_k_norm"], ATT_HKV)[None, :],
        "ga": p["attn_out_norm"][None, :],
        "cw": p["lru_conv_w"], "cb": p["lru_conv_b"][None, :],
        "w_gates": _split_w(_lru_gate_weights(p["lru_wa"], p["lru_wx"])),
        "ba": p["lru_ba"], "bx": p["lru_bx"], "lam": p["lru_lambda"],
        "gb": p["lru_out_norm"][None, :],
        "ret_gain": p["ret_norm"][None, :],
        "mu": mu_pad,
        "w2": _split_w(w2cat), "a2": _split_w(a2pad), "g2": _split_w(p["rwkv_g2"]),
        "w0": p["rwkv_w0"].reshape(1, 2 * GROUP_W), "a0": p["rwkv_a0"][None, :],
        "k_k": p["rwkv_k_k"][None, :], "k_a": p["rwkv_k_a"][None, :],
        "r_k": p["rwkv_r_k"].reshape(1, GROUP_W),
        "ln_w": p["rwkv_ln_w"][None, :], "ln_b": p["rwkv_ln_b"][None, :],
        "w_out": p["w_out"].astype(BF16),
        "norm_ffn": p["norm_ffn"][None, :],
        "w_gate": p["w_gate"].astype(BF16), "w_up": p["w_up"].astype(BF16),
        "w_down": p["w_down"].astype(BF16),
    }
    return out


def _layer(x, lp, consts):
    bsz, s, d = x.shape
    t = bsz * s
    x2d = x.reshape(t, d)
    z = _inproj(x2d, lp["norm_mix"], lp["w_z"]).reshape(bsz, s, Z_COLS)

    ya = _attention(z, consts["att_tabs"], lp["gq"], lp["gk"], consts["bd512"], consts["bd128"])
    yb = _rglru(z, lp["cw"], lp["cb"], lp["w_gates"][0], lp["w_gates"][1], lp["ba"], lp["bx"], lp["lam"])
    yc = _retention(z, consts["ret_cos"], consts["ret_sin"], consts["ret_lg"], lp["ret_gain"])
    r, k, v, kk, b, lwf, lwb, g = _rwkv_prep(
        z, lp["mu"], lp["w2"][0], lp["w2"][1], lp["a2"][0], lp["a2"][1], lp["g2"][0], lp["g2"][1],
        lp["w0"], lp["a0"], lp["k_k"], lp["k_a"], consts["bd512"])
    yf, ybw = _rwkv_scan(r, k, v, kk, b, lwf, lwb)
    f2 = lambda a: a.reshape(t, GROUP_W)
    yd = _rwkv_post(f2(yf), f2(ybw), f2(r), f2(k), f2(v), f2(g), lp["r_k"], lp["ln_w"], lp["ln_b"],
                    consts["bd512"])
    x2d = _outproj(f2(ya), f2(yb), f2(yc), yd, lp["ga"], lp["gb"], lp["w_out"], x2d)
    x2d = _ffn(x2d, lp["norm_ffn"], lp["w_gate"], lp["w_up"], lp["w_down"])
    return x2d.reshape(bsz, s, d)


def _consts(s):
    cq, sq = _rope_tables(s, ATT_HD, ATT_HQ)
    ck, sk = _rope_tables(s, ATT_HD, ATT_HKV)
    rc, rs = _rope_tables(s, RET_HD, 1)
    log_g = jnp.log1p(-jnp.exp2(-5.0 - jnp.arange(RET_H, dtype=F32)))
    return {
        "att_tabs": (cq, sq, ck, sk),
        "ret_cos": rc, "ret_sin": rs,
        "ret_lg": jnp.broadcast_to(log_g[:, None, None], (RET_H, 1, LANE)),
        "bd512": _block_diag_ones(GROUP_W, 64),
        "bd128": _block_diag_ones(LANE, 64),
    }


def _trunk(x, layers, consts):
    for lp in layers:
        x = _layer(x, lp, consts)
    return x


def kernel(x_prompt, x_sample, norm_mix, w_in, attn_q_norm, attn_k_norm, attn_out_norm, lru_conv_w, lru_conv_b, lru_wa, lru_ba, lru_wx, lru_bx, lru_lambda, lru_out_norm, ret_norm, rwkv_mu, rwkv_w0, rwkv_w2, rwkv_a0, rwkv_a2, rwkv_g2, rwkv_k_k, rwkv_k_a, rwkv_r_k, rwkv_ln_w, rwkv_ln_b, w_out, norm_ffn, w_gate, w_up, w_down):
    params = {
        "norm_mix": norm_mix, "w_in": w_in,
        "attn_q_norm": attn_q_norm, "attn_k_norm": attn_k_norm, "attn_out_norm": attn_out_norm,
        "lru_conv_w": lru_conv_w, "lru_conv_b": lru_conv_b, "lru_wa": lru_wa, "lru_ba": lru_ba,
        "lru_wx": lru_wx, "lru_bx": lru_bx, "lru_lambda": lru_lambda, "lru_out_norm": lru_out_norm,
        "ret_norm": ret_norm,
        "rwkv_mu": rwkv_mu, "rwkv_w0": rwkv_w0, "rwkv_w2": rwkv_w2, "rwkv_a0": rwkv_a0,
        "rwkv_a2": rwkv_a2, "rwkv_g2": rwkv_g2, "rwkv_k_k": rwkv_k_k, "rwkv_k_a": rwkv_k_a,
        "rwkv_r_k": rwkv_r_k, "rwkv_ln_w": rwkv_ln_w, "rwkv_ln_b": rwkv_ln_b,
        "w_out": w_out, "norm_ffn": norm_ffn, "w_gate": w_gate, "w_up": w_up, "w_down": w_down,
    }
    depth = w_in.shape[0]
    layers = [_prep_layer({name: arr[l] for name, arr in params.items()}) for l in range(depth)]
    y_prompt = _trunk(x_prompt, layers, _consts(x_prompt.shape[1]))
    y_sample = _trunk(x_sample, layers, _consts(x_sample.shape[1]))
    return (y_prompt, y_sample)
```

```python
import functools

import jax
import jax.numpy as jnp
from jax import lax
from jax.experimental import pallas as pl
from jax.experimental.pallas import tpu as pltpu

F32 = jnp.float32
BF16 = jnp.bfloat16

D_MODEL = 2048
GRID_W = 64
EPS = 1e-6
GROUP_W = 512
ATT_HD = 64
ATT_HQ = 8
ATT_HKV = 2
ROPE_BASE = 10000.0
LRU_BD = 64
LRU_C = 8.0
CONV_W = 4
CONV_LEFT = 2
RET_HD = 128
RET_H = 4
RWKV_HD = 64
RWKV_LN_EPS = 64e-5
D_FF = 5632

A_COLS = 768
B_COLS = 1024
C_COLS = 2048
D_MAIN = 1536
OFF_B = A_COLS
OFF_C = OFF_B + B_COLS
OFF_D = OFF_C + C_COLS
D_COLS_PAD = 1920
Z_COLS = OFF_D + D_COLS_PAD

LANE = 128
VMEM_LIMIT = 56 * 1024 * 1024

RET_CHUNK = 256
RWKV_CHUNK = 64
RWKV_SUBCHUNKS = 4


def _cparams(sem):
    return pltpu.CompilerParams(dimension_semantics=sem, vmem_limit_bytes=VMEM_LIMIT)


def _dot(a, b):
    return jnp.dot(a.astype(BF16), b.astype(BF16), preferred_element_type=F32)


def _dot_nt(a, b):
    return lax.dot_general(a.astype(BF16), b.astype(BF16), (((1,), (1,)), ((), ())),
                           preferred_element_type=F32)


def _split(x):
    hi = x.astype(BF16)
    lo = (x - hi.astype(F32)).astype(BF16)
    return hi, lo


def _dot3(x, wh, wl):
    xh, xl = _split(x)
    return (jnp.dot(xh, wh, preferred_element_type=F32)
            + jnp.dot(xl, wh, preferred_element_type=F32)
            + jnp.dot(xh, wl, preferred_element_type=F32))


def _seg_sum(x, bd):
    xh, xl = _split(x)
    return jnp.dot(xh, bd, preferred_element_type=F32) + jnp.dot(xl, bd, preferred_element_type=F32)


def _rope(x, cos, sin_signed, half):
    n = x.shape[-1]
    lane = lax.broadcasted_iota(jnp.int32, x.shape, 1)
    first = (lane % (2 * half)) < half
    rot = jnp.where(first, pltpu.roll(x, n - half, 1), pltpu.roll(x, half, 1))
    return x * cos + rot * sin_signed


def _softplus(x):
    return jnp.maximum(x, 0.0) + jnp.log1p(jnp.exp(-jnp.abs(x)))


def _sigmoid(x):
    return 0.5 * jnp.tanh(0.5 * x) + 0.5


def _rmsnorm_kernel(x_ref, g_ref, o_ref):
    x = x_ref[...]
    ms = jnp.mean(x * x, axis=-1, keepdims=True)
    o_ref[...] = (x * lax.rsqrt(ms + EPS) * g_ref[...]).astype(BF16)


def _rmsnorm(x2d, gain):
    t, d = x2d.shape
    tm = min(1024, t)
    return pl.pallas_call(
        _rmsnorm_kernel,
        name="rmsnorm",
        grid=(t // tm,),
        in_specs=[pl.BlockSpec((tm, d), lambda i: (i, 0)),
                  pl.BlockSpec((1, d), lambda i: (0, 0))],
        out_specs=pl.BlockSpec((tm, d), lambda i: (i, 0)),
        out_shape=jax.ShapeDtypeStruct((t, d), BF16),
        compiler_params=_cparams(("parallel",)),
    )(x2d, gain)


def _inproj_kernel(h_ref, w_ref, o_ref):
    o_ref[...] = jnp.dot(h_ref[...], w_ref[...], preferred_element_type=F32).astype(BF16)


def _inproj(x2d, gain, w):
    h = _rmsnorm(x2d, gain)
    t, d = h.shape
    n = w.shape[1]
    tm = min(1024, t)
    tn = 1920
    return pl.pallas_call(
        _inproj_kernel,
        name="inproj",
        grid=(t // tm, n // tn),
        in_specs=[pl.BlockSpec((tm, d), lambda i, j: (i, 0)),
                  pl.BlockSpec((d, tn), lambda i, j: (0, j))],
        out_specs=pl.BlockSpec((tm, tn), lambda i, j: (i, j)),
        out_shape=jax.ShapeDtypeStruct((t, n), BF16),
        compiler_params=_cparams(("parallel", "arbitrary")),
    )(h, w)


def _attn_kernel(q_ref, kv_ref, cq_ref, sq_ref, ck_ref, sk_ref, gq_ref, gk_ref, bdq_ref, bdk_ref,
                 o_ref, k_s, v_s):
    @pl.when(pl.program_id(1) == 0)
    def _():
        kv = kv_ref[0]
        k = kv[:, :LANE].astype(F32)
        ms = _seg_sum(k * k, bdk_ref[...]) * (1.0 / ATT_HD)
        k = k * lax.rsqrt(ms + EPS) * gk_ref[...]
        k = _rope(k, ck_ref[...], sk_ref[...], ATT_HD // 4)
        ones = jnp.ones((kv.shape[0], ATT_HD), BF16)
        for g in range(ATT_HKV):
            k_s[g] = k[:, g * ATT_HD:(g + 1) * ATT_HD].astype(BF16)
            v_s[g] = jnp.concatenate([kv[:, LANE + g * ATT_HD:LANE + (g + 1) * ATT_HD], ones], axis=-1)

    q = q_ref[0].astype(F32)
    ms = _seg_sum(q * q, bdq_ref[...]) * (1.0 / ATT_HD)
    q = q * lax.rsqrt(ms + EPS) * gq_ref[...]
    q = _rope(q, cq_ref[...], sq_ref[...], ATT_HD // 4).astype(BF16)
    n_rep = ATT_HQ // ATT_HKV
    scores = lambda h: _dot_nt(q[:, h * ATT_HD:(h + 1) * ATT_HD], k_s[h // n_rep])
    s_next = scores(0)
    for h in range(ATT_HQ):
        g = h // n_rep
        s = s_next
        if h + 1 < ATT_HQ:
            s_next = scores(h + 1)
        m = jnp.max(s, axis=-1, keepdims=True)
        p = jnp.exp((s - m).astype(BF16))
        o = jnp.dot(p, v_s[g], preferred_element_type=F32)
        o_ref[0, :, h * ATT_HD:(h + 1) * ATT_HD] = (o[:, :ATT_HD] / o[:, ATT_HD:]).astype(BF16)


def _attention(z, tabs, gq, gk, bdq, bdk):
    b, s, _ = z.shape
    tq = min(512, s)
    cq, sq, ck, sk = tabs
    return pl.pallas_call(
        _attn_kernel,
        name="attn",
        grid=(b, s // tq),
        in_specs=[pl.BlockSpec((1, tq, 512), lambda i, j: (i, j, 0)),
                  pl.BlockSpec((1, s, 256), lambda i, j: (i, 0, 2)),
                  pl.BlockSpec((tq, 512), lambda i, j: (j, 0)),
                  pl.BlockSpec((tq, 512), lambda i, j: (j, 0)),
                  pl.BlockSpec((s, LANE), lambda i, j: (0, 0)),
                  pl.BlockSpec((s, LANE), lambda i, j: (0, 0)),
                  pl.BlockSpec((1, 512), lambda i, j: (0, 0)),
                  pl.BlockSpec((1, LANE), lambda i, j: (0, 0)),
                  pl.BlockSpec((512, 512), lambda i, j: (0, 0)),
                  pl.BlockSpec((LANE, LANE), lambda i, j: (0, 0))],
        out_specs=pl.BlockSpec((1, tq, 512), lambda i, j: (i, j, 0)),
        out_shape=jax.ShapeDtypeStruct((b, s, GROUP_W), BF16),
        scratch_shapes=[pltpu.VMEM((ATT_HKV, s, ATT_HD), BF16),
                        pltpu.VMEM((ATT_HKV, s, 2 * ATT_HD), BF16)],
        compiler_params=_cparams(("parallel", "arbitrary")),
    )(z, z, cq, sq, ck, sk, gq, gk, bdq, bdk)


def _shift_rows(v, d, fill, row):
    n = v.shape[0]
    if d > 0:
        return jnp.where(row >= d, pltpu.roll(v, d, 0), fill)
    return jnp.where(row < n + d, pltpu.roll(v, n + d, 0), fill)


SUBLANES = 8


def _group_scan(a, b, h_in, sub, reverse):
    k = 1
    while k < SUBLANES:
        if reverse:
            keep = sub <= SUBLANES - 1 - k
            b = a * jnp.where(keep, pltpu.roll(b, SUBLANES - k, 0), 0.0) + b
            a = a * jnp.where(keep, pltpu.roll(a, SUBLANES - k, 0), 1.0)
        else:
            keep = sub >= k
            b = a * jnp.where(keep, pltpu.roll(b, k, 0), 0.0) + b
            a = a * jnp.where(keep, pltpu.roll(a, k, 0), 1.0)
        k *= 2
    return a * h_in + b


def _lru_kernel(x_ref, gate_ref, cw_ref, cb_ref, wgh_ref, wgl_ref,
                ba_ref, bx_ref, lam_ref, o_ref, a_s, b_s, h_s):
    x = x_ref[0].astype(F32)
    s_len = x.shape[0]
    n_grp = s_len // SUBLANES
    row = lax.broadcasted_iota(jnp.int32, x.shape, 0)
    xc = cb_ref[...]
    for j in range(CONV_W):
        d = CONV_LEFT - j
        xs = x if d == 0 else _shift_rows(x, d, 0.0, row)
        xc = xc + xs * cw_ref[j:j + 1, :]
    pre = _dot3(xc, wgh_ref[0], wgl_ref[0])
    for d in range(2):
        r = _sigmoid(pre[:, 2 * d * LANE:(2 * d + 1) * LANE] + ba_ref[d:d + 1, :])
        i = _sigmoid(pre[:, (2 * d + 1) * LANE:(2 * d + 2) * LANE] + bx_ref[d:d + 1, :])
        log_a = -LRU_C * r * _softplus(-lam_ref[d:d + 1, :])
        a = jnp.exp(log_a)
        om = -jnp.tanh(log_a) * (a * a + 1.0)
        a_s[d] = a
        b_s[d] = jnp.where(om > 0.0, om * lax.rsqrt(om), 0.0) * (i * xc)

    sub = lax.broadcasted_iota(jnp.int32, (SUBLANES, LANE), 0)

    def body(g, carry):
        hf_in, hb_in = carry
        gf = pl.ds(pl.multiple_of(g * SUBLANES, SUBLANES), SUBLANES)
        gb = pl.ds(pl.multiple_of((n_grp - 1 - g) * SUBLANES, SUBLANES), SUBLANES)
        hf = _group_scan(a_s[0, gf, :], b_s[0, gf, :], hf_in, sub, False)
        hb = _group_scan(a_s[1, gb, :], b_s[1, gb, :], hb_in, sub, True)
        h_s[0, gf, :] = hf
        h_s[1, gb, :] = hb
        return (jnp.broadcast_to(hf[SUBLANES - 1:SUBLANES], hf.shape), jnp.broadcast_to(hb[0:1], hb.shape))

    zero = jnp.zeros((SUBLANES, LANE), F32)
    lax.fori_loop(0, n_grp, body, (zero, zero), unroll=8)
    gate = gate_ref[0].astype(F32)
    gelu = 0.5 * gate * (1.0 + jnp.tanh(0.7978845608028654 * (gate + 0.044715 * gate * gate * gate)))
    o_ref[0] = ((h_s[0] + h_s[1]) * gelu).astype(BF16)


def _rglru(z, cw, cb, wgh, wgl, ba, bx, lam):
    b, s, _ = z.shape
    nc = GROUP_W // LANE
    xo = OFF_B // LANE
    go = (OFF_B + GROUP_W) // LANE
    vec = lambda r: pl.BlockSpec((r, LANE), lambda i, c: (0, c))
    wsp = pl.BlockSpec((1, LANE, 4 * LANE), lambda i, c: (c, 0, 0))
    return pl.pallas_call(
        _lru_kernel,
        name="rglru",
        grid=(b, nc),
        in_specs=[pl.BlockSpec((1, s, LANE), lambda i, c: (i, 0, xo + c)),
                  pl.BlockSpec((1, s, LANE), lambda i, c: (i, 0, go + c)),
                  vec(CONV_W), vec(1), wsp, wsp, vec(2), vec(2), vec(2)],
        out_specs=pl.BlockSpec((1, s, LANE), lambda i, c: (i, 0, c)),
        out_shape=jax.ShapeDtypeStruct((b, s, GROUP_W), BF16),
        scratch_shapes=[pltpu.VMEM((2, s, LANE), F32)] * 3,
        compiler_params=_cparams(("parallel", "parallel")),
    )(z, z, cw, cb, wgh, wgl, ba, bx, lam)


def _ret_kernel(q_ref, k_ref, v_ref, g_ref, cos_ref, sin_ref, lg_ref, gain_ref, o_ref,
                q_s, k_s, v_s, acc, st):
    s_len = q_ref.shape[1]
    c = min(RET_CHUNK, s_len)
    nch = s_len // c
    half = RET_HD // 4
    q_s[...] = _rope(q_ref[0].astype(F32), cos_ref[...], sin_ref[...], half).astype(BF16)
    k_s[...] = (_rope(k_ref[0].astype(F32), cos_ref[...], sin_ref[...], half) * (RET_HD ** -0.5)).astype(BF16)
    v_s[...] = v_ref[0]

    lg = lg_ref[0]
    ii = lax.broadcasted_iota(jnp.int32, (c, c), 0)
    jj = lax.broadcasted_iota(jnp.int32, (c, c), 1)
    gam = jnp.exp(jnp.abs(ii - jj).astype(F32) * lg[:, :1])
    ri = lax.broadcasted_iota(jnp.int32, (c, LANE), 0).astype(F32)
    qw_f = jnp.exp((ri + 1.0) * lg)
    qw_b = jnp.exp((c - ri) * lg)
    kw_f = jnp.exp((c - 1.0 - ri) * lg)
    kw_b = jnp.exp(ri * lg)
    g_chunk = jnp.exp(c * lg)
    gain = gain_ref[...]

    def kv_inc(kc, vc, kw):
        return jnp.dot((kc.astype(F32) * kw).T.astype(BF16), vc, preferred_element_type=F32)

    st[...] = jnp.zeros_like(st)

    def fwd_body(n, carry):
        sl = pl.ds(pl.multiple_of(n * c, c), c)
        qc, kc, vc = q_s[sl, :], k_s[sl, :], v_s[sl, :]
        sc = _dot_nt(qc, kc) * gam
        o = jnp.dot(sc.astype(BF16), vc, preferred_element_type=F32)
        o = o + jnp.dot(qc, st[...].astype(BF16), preferred_element_type=F32) * qw_f
        acc[sl, :] = o
        st[...] = g_chunk * st[...] + kv_inc(kc, vc, kw_f)
        return carry

    lax.fori_loop(0, nch, fwd_body, 0, unroll=8)
    st[...] = jnp.zeros_like(st)

    def bwd_body(m, carry):
        n = nch - 1 - m
        sl = pl.ds(pl.multiple_of(n * c, c), c)
        qc, kc, vc = q_s[sl, :], k_s[sl, :], v_s[sl, :]
        o = acc[sl, :] + jnp.dot(qc, st[...].astype(BF16), preferred_element_type=F32) * qw_b
        st[...] = g_chunk * st[...] + kv_inc(kc, vc, kw_b)
        ms = jnp.mean(o * o, axis=-1, keepdims=True)
        o = o * lax.rsqrt(ms + EPS) * gain
        gate = g_ref[0, sl, :].astype(F32)
        o_ref[0, sl, :] = (o * (gate * _sigmoid(gate))).astype(BF16)
        return carry

    lax.fori_loop(0, nch, bwd_body, 0, unroll=8)


def _retention(z, cos, sin, lg, gain):
    b, s, _ = z.shape
    base = OFF_C // LANE
    zs = lambda o: pl.BlockSpec((1, s, LANE), lambda i, h: (i, 0, base + o + h))
    return pl.pallas_call(
        _ret_kernel,
        name="retention",
        grid=(b, RET_H),
        in_specs=[zs(0), zs(RET_H), zs(2 * RET_H), zs(3 * RET_H),
                  pl.BlockSpec((s, LANE), lambda i, h: (0, 0)),
                  pl.BlockSpec((s, LANE), lambda i, h: (0, 0)),
                  pl.BlockSpec((1, 1, LANE), lambda i, h: (h, 0, 0)),
                  pl.BlockSpec((1, LANE), lambda i, h: (0, h))],
        out_specs=pl.BlockSpec((1, s, LANE), lambda i, h: (i, 0, h)),
        out_shape=jax.ShapeDtypeStruct((b, s, GROUP_W), BF16),
        scratch_shapes=[pltpu.VMEM((s, LANE), BF16), pltpu.VMEM((s, LANE), BF16),
                        pltpu.VMEM((s, LANE), BF16), pltpu.VMEM((s, LANE), F32),
                        pltpu.VMEM((LANE, LANE), F32)],
        compiler_params=_cparams(("parallel", "parallel")),
    )(z, z, z, z, cos, sin, lg, gain)


def _rwkv_prep_kernel(z_ref, zp_ref, zn_ref, mu_ref, w2h_ref, w2l_ref, a2h_ref, a2l_ref,
                      g2h_ref, g2l_ref, w0_ref, a0_ref, kk_ref, ka_ref, bd_ref,
                      r_o, k_o, v_o, kk_o, b_o, lwf_o, lwb_o, g_o):
    i = pl.program_id(1)
    last = pl.num_programs(1) - 1
    z = z_ref[0].astype(F32)
    tt = z.shape[0]
    hr = zp_ref.shape[1]
    prev_row = jnp.where(i > 0, zp_ref[0].astype(F32)[hr - 1:hr, :], 0.0)
    next_row = jnp.where(i < last, zn_ref[0].astype(F32)[0:1, :], 0.0)
    row = lax.broadcasted_iota(jnp.int32, z.shape, 0)
    prev = jnp.where(row == 0, prev_row, pltpu.roll(z, 1, 0))
    nxt = jnp.where(row == tt - 1, next_row, pltpu.roll(z, tt - 1, 0))
    zs = z + (0.5 * (prev + nxt) - z) * mu_ref[...]
    g = GROUP_W
    r, k, v = zs[:, :g], zs[:, g:2 * g], zs[:, 2 * g:3 * g]
    lora_w = _dot3(jnp.tanh(zs[:, D_MAIN:D_MAIN + LANE]), w2h_ref[...], w2l_ref[...])
    lw = -0.6065306597126334 * _sigmoid(w0_ref[...] + lora_w)
    a = _sigmoid(a0_ref[...] + _dot3(zs[:, D_MAIN + LANE:D_MAIN + 2 * LANE], a2h_ref[...], a2l_ref[...]))
    gg = _dot3(_sigmoid(zs[:, D_MAIN + 2 * LANE:]), g2h_ref[...], g2l_ref[...])
    kk = k * kk_ref[...]
    kk = kk * lax.rsqrt(_seg_sum(kk * kk, bd_ref[...]) + 1e-12)
    r_o[0] = r.astype(BF16)
    k_o[0] = (k * (1.0 + (a - 1.0) * ka_ref[...])).astype(BF16)
    v_o[0] = v.astype(BF16)
    kk_o[0] = kk
    b_o[0] = kk * a
    lwf_o[0] = lw[:, :g]
    lwb_o[0] = lw[:, g:]
    g_o[0] = gg.astype(BF16)


def _rwkv_prep(z, mu, w2h, w2l, a2h, a2l, g2h, g2l, w0, a0, k_k, k_a, bd):
    b, s, _ = z.shape
    tt = min(256, s)
    hr = 16
    nbh = s // hr
    cb = OFF_D // D_COLS_PAD
    full = lambda a: pl.BlockSpec(a.shape, lambda i, j: (0,) * a.ndim)
    osp = pl.BlockSpec((1, tt, GROUP_W), lambda i, j: (i, j, 0))
    osh = jax.ShapeDtypeStruct((b, s, GROUP_W), BF16)
    osh32 = jax.ShapeDtypeStruct((b, s, GROUP_W), F32)
    return pl.pallas_call(
        _rwkv_prep_kernel,
        name="rwkv_prep",
        grid=(b, s // tt),
        in_specs=[pl.BlockSpec((1, tt, D_COLS_PAD), lambda i, j: (i, j, cb)),
                  pl.BlockSpec((1, hr, D_COLS_PAD), lambda i, j: (i, jnp.maximum(j * (tt // hr) - 1, 0), cb)),
                  pl.BlockSpec((1, hr, D_COLS_PAD), lambda i, j: (i, jnp.minimum((j + 1) * (tt // hr), nbh - 1), cb)),
                  full(mu), full(w2h), full(w2l), full(a2h), full(a2l), full(g2h), full(g2l),
                  full(w0), full(a0), full(k_k), full(k_a), full(bd)],
        out_specs=[osp] * 8,
        out_shape=[osh] * 3 + [osh32] * 4 + [osh],
        compiler_params=_cparams(("parallel", "parallel")),
    )(z, z, z, mu, w2h, w2l, a2h, a2l, g2h, g2l, w0, a0, k_k, k_a, bd)


def _rwkv_prepare(chains):
    c = RWKV_CHUNK
    hd = RWKV_HD
    row = lax.broadcasted_iota(jnp.int32, (c, LANE), 0)
    lane = lax.broadcasted_iota(jnp.int32, (c, LANE), 1)
    head0 = lane < hd
    col = lane % c
    m_strict = {True: row > col, False: row < col}
    m_y = {True: row >= col, False: m_strict[False]}
    eye = (row == col).astype(F32)
    bf = lambda t: t.astype(BF16)
    cat = lambda ts: jnp.concatenate(ts, axis=0)
    per_head = functools.partial(_per_head, head0=head0)
    fwds = [ch[6] for ch in chains]

    def decays(ch):
        r, lw, k, v, kk, b, fwd = ch
        cum = lw
        d = 1
        while d < c:
            cum = cum + _shift_rows(cum, d if fwd else -d, 0.0, row)
            d *= 2
        cum_ex = cum - lw
        mid = cum[c // 2:c // 2 + 1, :]
        tot = cum[c - 1:c, :] if fwd else cum[0:1, :]
        e_ex = jnp.exp(cum_ex)
        c_hat = jnp.exp(cum_ex - mid)
        c_til = jnp.exp(mid - cum)
        c_end = jnp.exp(tot - cum)
        kk_hat = kk * c_hat
        if fwd:
            r0 = r * jnp.exp(cum)
            r_hat = r * jnp.exp(cum - mid)
        else:
            r0 = r * e_ex
            r_hat = r * c_hat
        lhs = cat([bf(kk_hat), bf(r_hat)])
        rhs = cat([per_head(bf(k * c_til)), per_head(bf(b * c_til))])
        q0 = cat([bf(kk * e_ex), bf(r0)])
        rhs_e = cat([bf(k * c_end), bf(b * c_end)])
        return lhs, rhs, q0, rhs_e, per_head(bf(v)), jnp.exp(tot)

    st = [decays(ch) for ch in chains]
    gram = [lax.dot_general(s[0], s[1], (((1,), (1,)), ((), ())), preferred_element_type=F32) for s in st]
    n2 = 2 * c
    zf = jnp.zeros((c, n2), F32)
    n_mat = [jnp.where(m_strict[f], g[:c, n2:], zf) for g, f in zip(gram, fwds)]
    mk_ark = [cat([bf(jnp.where(m_strict[f], g[:c, :n2], zf)), bf(jnp.where(m_y[f], g[c:, :n2], zf))])
              for g, f in zip(gram, fwds)]
    a_rb = [bf(jnp.where(m_y[f], g[c:, n2:], zf)) for g, f in zip(gram, fwds)]

    t_inv = [eye - n for n in n_mat]
    pw = [jnp.dot(bf(n), per_head(bf(n)), preferred_element_type=F32) for n in n_mat]
    step = 4
    while step < c:
        tp = [jnp.dot(cat([bf(t), bf(p)]), per_head(bf(p)), preferred_element_type=F32)
              for t, p in zip(t_inv, pw)]
        t_inv = [t + x[:c] for t, x in zip(t_inv, tp)]
        pw = [x[c:] for x in tp]
        step *= 2
    t_inv = [t + jnp.dot(bf(t), per_head(bf(p)), preferred_element_type=F32) for t, p in zip(t_inv, pw)]
    mv = [jnp.dot(m, s[4], preferred_element_type=F32) for m, s in zip(mk_ark, st)]
    return [(s[2], s[3], s[5], bf(t), m, a, ch[3]) for s, t, m, a, ch in zip(st, t_inv, mv, a_rb, chains)]


def _per_head(t, head0):
    zero = jnp.zeros_like(t)
    return jnp.concatenate([jnp.where(head0, t, zero), jnp.where(head0, zero, t)], axis=0)


def _rwkv_apply(prep, hts):
    c = RWKV_CHUNK
    lane = lax.broadcasted_iota(jnp.int32, (c, LANE), 1)
    per_head = functools.partial(_per_head, head0=lane < RWKV_HD)
    i2 = lax.broadcasted_iota(jnp.int32, (LANE, LANE), 0)
    j2 = lax.broadcasted_iota(jnp.int32, (LANE, LANE), 1)
    head_diag = (i2 // RWKV_HD) == (j2 // RWKV_HD)
    bf = lambda t: t.astype(BF16)
    hx = [lax.dot_general(pr[0], bf(h), (((1,), (1,)), ((), ())), preferred_element_type=F32)
          for pr, h in zip(prep, hts)]
    xs = [x[:c] + pr[4][:c] for x, pr in zip(hx, prep)]
    us = [jnp.dot(pr[3], per_head(bf(x)), preferred_element_type=F32) for pr, x in zip(prep, xs)]
    ys = [pr[4][c:] - jnp.dot(pr[5], per_head(bf(u)), preferred_element_type=F32) + x[c:]
          for pr, u, x in zip(prep, us, hx)]
    lhs_t = [jnp.concatenate([pr[6], -u], axis=0).T for pr, u in zip(prep, us)]
    inc = [jnp.dot(bf(l), pr[1], preferred_element_type=F32) for l, pr in zip(lhs_t, prep)]
    zs = jnp.zeros((LANE, LANE), F32)
    h_new = [h * pr[2] + jnp.where(head_diag, i, zs) for h, pr, i in zip(hts, prep, inc)]
    return list(zip(ys, h_new))


def _rwkv_scan_kernel(rf, kf, vf, kkf, bf, lwf, rb, kb, vb, kkb, bb, lwb, yf_ref, yb_ref, h_ref):
    @pl.when(pl.program_id(1) == 0)
    def _():
        h_ref[...] = jnp.zeros_like(h_ref)

    c = RWKV_CHUNK
    n_sub = rf.shape[1] // c
    dirs = ((rf, lwf, kf, vf, kkf, bf, True), (rb, lwb, kb, vb, kkb, bb, False))
    n_pairs = GROUP_W // LANE
    keys, chains = [], []
    for d, (r_, lw_, k_, v_, kk_, b_, fwd) in enumerate(dirs):
        for p in range(n_pairs):
            sl = slice(p * LANE, (p + 1) * LANE)
            for j in range(n_sub):
                rows = slice(j * c, (j + 1) * c)
                keys.append((d, p, j))
                ld = lambda ref: ref[0, rows, sl].astype(F32)
                chains.append((ld(r_), lw_[0, rows, sl], ld(k_), ld(v_), ld(kk_), ld(b_), fwd))
    prep = dict(zip(keys, _rwkv_prepare(chains)))
    dp = [(d, p) for d in range(2) for p in range(n_pairs)]
    hts = [h_ref[d, p] for d, p in dp]
    for step in range(n_sub):
        subs = [step if d == 0 else n_sub - 1 - step for d, _ in dp]
        outs = _rwkv_apply([prep[(d, p, j)] for (d, p), j in zip(dp, subs)], hts)
        hts = [hn for _, hn in outs]
        for (d, p), j, (y, _) in zip(dp, subs, outs):
            (yf_ref, yb_ref)[d][0, j * c:(j + 1) * c, p * LANE:(p + 1) * LANE] = y.astype(BF16)
    for (d, p), hn in zip(dp, hts):
        h_ref[d, p] = hn


def _rwkv_scan(r, k, v, kk, b, lwf, lwb):
    bsz, s, _ = r.shape
    c = RWKV_CHUNK
    rows = min(RWKV_SUBCHUNKS * c, s)
    nch = s // rows
    fs = pl.BlockSpec((1, rows, GROUP_W), lambda i, n: (i, n, 0))
    bs = pl.BlockSpec((1, rows, GROUP_W), lambda i, n: (i, nch - 1 - n, 0))
    osh = jax.ShapeDtypeStruct((bsz, s, GROUP_W), BF16)
    return pl.pallas_call(
        _rwkv_scan_kernel,
        name="rwkv_scan",
        grid=(bsz, nch),
        in_specs=[fs] * 6 + [bs] * 6,
        out_specs=[fs, bs],
        out_shape=[osh, osh],
        scratch_shapes=[pltpu.VMEM((2, GROUP_W // LANE, LANE, LANE), F32)],
        compiler_params=_cparams(("parallel", "arbitrary")),
    )(r, k, v, kk, b, lwf, r, k, v, kk, b, lwb)


def _rwkv_post_kernel(yf_ref, yb_ref, r_ref, k_ref, v_ref, g_ref, rk_ref, lnw_ref, lnb_ref, bd_ref, o_ref):
    bd = bd_ref[...]
    f32 = lambda ref: ref[...].astype(F32)
    y = f32(yf_ref) + f32(yb_ref) + _seg_sum(f32(r_ref) * f32(k_ref) * rk_ref[...], bd) * f32(v_ref)
    mean = _seg_sum(y, bd) * (1.0 / RWKV_HD)
    yc = y - mean
    var = _seg_sum(yc * yc, bd) * (1.0 / RWKV_HD)
    y = yc * lax.rsqrt(var + RWKV_LN_EPS) * lnw_ref[...] + lnb_ref[...]
    o_ref[...] = (y * f32(g_ref)).astype(BF16)


def _rwkv_post(yf, yb, r, k, v, g, rk, lnw, lnb, bd):
    t = yf.shape[0]
    tm = min(1024, t)
    big = pl.BlockSpec((tm, GROUP_W), lambda i: (i, 0))
    vec = pl.BlockSpec((1, GROUP_W), lambda i: (0, 0))
    return pl.pallas_call(
        _rwkv_post_kernel,
        name="rwkv_post",
        grid=(t // tm,),
        in_specs=[big] * 6 + [vec] * 3 + [pl.BlockSpec((GROUP_W, GROUP_W), lambda i: (0, 0))],
        out_specs=big,
        out_shape=jax.ShapeDtypeStruct((t, GROUP_W), BF16),
        compiler_params=_cparams(("parallel",)),
    )(yf, yb, r, k, v, g, rk, lnw, lnb, bd)


def _outproj_kernel(ya_ref, yb_ref, yc_ref, yd_ref, ga_ref, gb_ref, w_ref, x_ref, o_ref):
    ya = ya_ref[...].astype(F32)
    ya = ya * lax.rsqrt(jnp.mean(ya * ya, axis=-1, keepdims=True) + EPS) * ga_ref[...]
    yb = yb_ref[...].astype(F32)
    yb = yb * lax.rsqrt(jnp.mean(yb * yb, axis=-1, keepdims=True) + EPS) * gb_ref[...]
    cat = jnp.concatenate([ya.astype(BF16), yb.astype(BF16), yc_ref[...], yd_ref[...]], axis=-1)
    o_ref[...] = x_ref[...] + jnp.dot(cat, w_ref[...], preferred_element_type=F32)


def _outproj(ya, yb, yc, yd, ga, gb, w, x2d):
    t, d = x2d.shape
    tm = min(512, t)
    ysp = pl.BlockSpec((tm, GROUP_W), lambda i: (i, 0))
    gsp = pl.BlockSpec((1, GROUP_W), lambda i: (0, 0))
    return pl.pallas_call(
        _outproj_kernel,
        name="outproj",
        grid=(t // tm,),
        in_specs=[ysp, ysp, ysp, ysp, gsp, gsp,
                  pl.BlockSpec((d, d), lambda i: (0, 0)),
                  pl.BlockSpec((tm, d), lambda i: (i, 0))],
        out_specs=pl.BlockSpec((tm, d), lambda i: (i, 0)),
        out_shape=jax.ShapeDtypeStruct((t, d), F32),
        compiler_params=_cparams(("parallel",)),
    )(ya, yb, yc, yd, ga, gb, w, x2d)


def _ffn_kernel(x_ref, g_ref, wg_ref, wu_ref, wd_ref, o_ref, h_ref):
    @pl.when(pl.program_id(1) == 0)
    def _():
        x = x_ref[...]
        ms = jnp.mean(x * x, axis=-1, keepdims=True)
        h_ref[...] = (x * lax.rsqrt(ms + EPS) * g_ref[...]).astype(BF16)
        o_ref[...] = x

    h = h_ref[...]
    gt = jnp.dot(h, wg_ref[...], preferred_element_type=F32)
    up = jnp.dot(h, wu_ref[...], preferred_element_type=F32)
    act = (gt * _sigmoid(gt) * up).astype(BF16)
    o_ref[...] += jnp.dot(act, wd_ref[...], preferred_element_type=F32)


def _ffn(x2d, gain, wg, wu, wd):
    t, d = x2d.shape
    f = wg.shape[1]
    tm = min(512, t)
    tf = 512
    return pl.pallas_call(
        _ffn_kernel,
        name="ffn",
        grid=(t // tm, f // tf),
        in_specs=[pl.BlockSpec((tm, d), lambda i, j: (i, 0)),
                  pl.BlockSpec((1, d), lambda i, j: (0, 0)),
                  pl.BlockSpec((d, tf), lambda i, j: (0, j)),
                  pl.BlockSpec((d, tf), lambda i, j: (0, j)),
                  pl.BlockSpec((tf, d), lambda i, j: (j, 0))],
        out_specs=pl.BlockSpec((tm, d), lambda i, j: (i, 0)),
        out_shape=jax.ShapeDtypeStruct((t, d), F32),
        scratch_shapes=[pltpu.VMEM((tm, d), BF16)],
        compiler_params=_cparams(("parallel", "arbitrary")),
    )(x2d, gain, wg, wu, wd)


def _block_diag_ones(n, seg):
    i = jnp.arange(n) // seg
    return (i[:, None] == i[None, :]).astype(BF16)


def _rope_tables(n_tok, head_dim, n_copies):
    half = head_dim // 2
    quarter = half // 2
    rows = (jnp.arange(n_tok, dtype=jnp.int32) // GRID_W).astype(F32)
    cols = (jnp.arange(n_tok, dtype=jnp.int32) % GRID_W).astype(F32)
    inv = 1.0 / (ROPE_BASE ** (jnp.arange(0, half, 2, dtype=F32) / half))
    ang_r = rows[:, None] * inv[None, :]
    ang_c = cols[:, None] * inv[None, :]
    cos = jnp.concatenate([jnp.cos(ang_r)] * 2 + [jnp.cos(ang_c)] * 2, axis=-1)
    sin = jnp.concatenate([-jnp.sin(ang_r), jnp.sin(ang_r), -jnp.sin(ang_c), jnp.sin(ang_c)], axis=-1)
    del quarter
    return jnp.tile(cos, (1, n_copies)), jnp.tile(sin, (1, n_copies))


def _pair_block_diag(w):
    lead = w.shape[:-3]
    w = w.reshape(lead + (4, 2, LRU_BD, LRU_BD))
    z = jnp.zeros_like(w[..., 0, :, :])
    top = jnp.concatenate([w[..., 0, :, :], z], axis=-1)
    bot = jnp.concatenate([z, w[..., 1, :, :]], axis=-1)
    return jnp.concatenate([top, bot], axis=-2)


def _lru_gate_weights(wa, wx):
    pa, px = _pair_block_diag(wa), _pair_block_diag(wx)
    return jnp.concatenate([pa[0], px[0], pa[1], px[1]], axis=-1)


def _split_w(w):
    hi = w.astype(BF16)
    return hi, (w - hi.astype(F32)).astype(BF16)


def _prep_layer(p):
    w_in = p["w_in"]
    d = w_in.shape[0]
    q = w_in[:, :512]
    kx = w_in[:, 512:640]
    vx = w_in[:, 640:768]
    zero64 = jnp.zeros((d, 64), F32)
    dm = w_in[:, OFF_D:]
    d_part = jnp.concatenate([dm[:, :D_MAIN + 192], zero64, dm[:, D_MAIN + 192:]], axis=1)
    w_z = jnp.concatenate([q, kx, vx, w_in[:, OFF_B:OFF_D], d_part], axis=1).astype(BF16)

    mu = p["rwkv_mu"]
    mu_pad = jnp.concatenate([mu[:D_MAIN + 192], jnp.zeros((64,), F32), mu[D_MAIN + 192:]])[None, :]
    w2 = p["rwkv_w2"]
    zl = jnp.zeros_like(w2[0])
    w2cat = jnp.concatenate([jnp.concatenate([w2[0], zl], axis=1),
                             jnp.concatenate([zl, w2[1]], axis=1)], axis=0)
    a2pad = jnp.concatenate([p["rwkv_a2"], jnp.zeros_like(p["rwkv_a2"])], axis=0)
    out = {
        "norm_mix": p["norm_mix"][None, :],
        "w_z": w_z,
        "gq": jnp.tile(p["attn_q_norm"] * (ATT_HD ** -0.5), ATT_HQ)[None, :],
        "gk": jnp.tile(p["attn_k_norm"], ATT_HKV)[None, :],
        "ga": p["attn_out_norm"][None, :],
        "cw": p["lru_conv_w"], "cb": p["lru_conv_b"][None, :],
        "w_gates": _split_w(_lru_gate_weights(p["lru_wa"], p["lru_wx"])),
        "ba": p["lru_ba"], "bx": p["lru_bx"], "lam": p["lru_lambda"],
        "gb": p["lru_out_norm"][None, :],
        "ret_gain": p["ret_norm"][None, :],
        "mu": mu_pad,
        "w2": _split_w(w2cat), "a2": _split_w(a2pad), "g2": _split_w(p["rwkv_g2"]),
        "w0": p["rwkv_w0"].reshape(1, 2 * GROUP_W), "a0": p["rwkv_a0"][None, :],
        "k_k": p["rwkv_k_k"][None, :], "k_a": p["rwkv_k_a"][None, :],
        "r_k": p["rwkv_r_k"].reshape(1, GROUP_W),
        "ln_w": p["rwkv_ln_w"][None, :], "ln_b": p["rwkv_ln_b"][None, :],
        "w_out": p["w_out"].astype(BF16),
        "norm_ffn": p["norm_ffn"][None, :],
        "w_gate": p["w_gate"].astype(BF16), "w_up": p["w_up"].astype(BF16),
        "w_down": p["w_down"].astype(BF16),
    }
    return out


def _layer(x, lp, consts):
    bsz, s, d = x.shape
    t = bsz * s
    x2d = x.reshape(t, d)
    z = _inproj(x2d, lp["norm_mix"], lp["w_z"]).reshape(bsz, s, Z_COLS)

    ya = _attention(z, consts["att_tabs"], lp["gq"], lp["gk"], consts["bd512"], consts["bd128"])
    yb = _rglru(z, lp["cw"], lp["cb"], lp["w_gates"][0], lp["w_gates"][1], lp["ba"], lp["bx"], lp["lam"])
    yc = _retention(z, consts["ret_cos"], consts["ret_sin"], consts["ret_lg"], lp["ret_gain"])
    r, k, v, kk, b, lwf, lwb, g = _rwkv_prep(
        z, lp["mu"], lp["w2"][0], lp["w2"][1], lp["a2"][0], lp["a2"][1], lp["g2"][0], lp["g2"][1],
        lp["w0"], lp["a0"], lp["k_k"], lp["k_a"], consts["bd512"])
    yf, ybw = _rwkv_scan(r, k, v, kk, b, lwf, lwb)
    f2 = lambda a: a.reshape(t, GROUP_W)
    yd = _rwkv_post(f2(yf), f2(ybw), f2(r), f2(k), f2(v), f2(g), lp["r_k"], lp["ln_w"], lp["ln_b"],
                    consts["bd512"])
    x2d = _outproj(f2(ya), f2(yb), f2(yc), yd, lp["ga"], lp["gb"], lp["w_out"], x2d)
    x2d = _ffn(x2d, lp["norm_ffn"], lp["w_gate"], lp["w_up"], lp["w_down"])
    return x2d.reshape(bsz, s, d)


def _consts(s):
    cq, sq = _rope_tables(s, ATT_HD, ATT_HQ)
    ck, sk = _rope_tables(s, ATT_HD, ATT_HKV)
    rc, rs = _rope_tables(s, RET_HD, 1)
    log_g = jnp.log1p(-jnp.exp2(-5.0 - jnp.arange(RET_H, dtype=F32)))
    return {
        "att_tabs": (cq, sq, ck, sk),
        "ret_cos": rc, "ret_sin": rs,
        "ret_lg": jnp.broadcast_to(log_g[:, None, None], (RET_H, 1, LANE)),
        "bd512": _block_diag_ones(GROUP_W, 64),
        "bd128": _block_diag_ones(LANE, 64),
    }


def _trunk(x, layers, consts):
    for lp in layers:
        x = _layer(x, lp, consts)
    return x


def kernel(x_prompt, x_sample, norm_mix, w_in, attn_q_norm, attn_k_norm, attn_out_norm, lru_conv_w, lru_conv_b, lru_wa, lru_ba, lru_wx, lru_bx, lru_lambda, lru_out_norm, ret_norm, rwkv_mu, rwkv_w0, rwkv_w2, rwkv_a0, rwkv_a2, rwkv_g2, rwkv_k_k, rwkv_k_a, rwkv_r_k, rwkv_ln_w, rwkv_ln_b, w_out, norm_ffn, w_gate, w_up, w_down):
    params = {
        "norm_mix": norm_mix, "w_in": w_in,
        "attn_q_norm": attn_q_norm, "attn_k_norm": attn_k_norm, "attn_out_norm": attn_out_norm,
        "lru_conv_w": lru_conv_w, "lru_conv_b": lru_conv_b, "lru_wa": lru_wa, "lru_ba": lru_ba,
        "lru_wx": lru_wx, "lru_bx": lru_bx, "lru_lambda": lru_lambda, "lru_out_norm": lru_out_norm,
        "ret_norm": ret_norm,
        "rwkv_mu": rwkv_mu, "rwkv_w0": rwkv_w0, "rwkv_w2": rwkv_w2, "rwkv_a0": rwkv_a0,
        "rwkv_a2": rwkv_a2, "rwkv_g2": rwkv_g2, "rwkv_k_k": rwkv_k_k, "rwkv_k_a": rwkv_k_a,
        "rwkv_r_k": rwkv_r_k, "rwkv_ln_w": rwkv_ln_w, "rwkv_ln_b": rwkv_ln_b,
        "w_out": w_out, "norm_ffn": norm_ffn, "w_gate": w_gate, "w_up": w_up, "w_down": w_down,
    }
    depth = w_in.shape[0]
    layers = [_prep_layer({name: arr[l] for name, arr in params.items()}) for l in range(depth)]
    y_prompt = _trunk(x_prompt, layers, _consts(x_prompt.shape[1]))
    y_sample = _trunk(x_sample, layers, _consts(x_sample.shape[1]))
    return (y_prompt, y_sample)
```

```python
import functools

import jax
import jax.numpy as jnp
from jax import lax
from jax.experimental import pallas as pl
from jax.experimental.pallas import tpu as pltpu

F32 = jnp.float32
BF16 = jnp.bfloat16

D_MODEL = 2048
GRID_W = 64
EPS = 1e-6
GROUP_W = 512
ATT_HD = 64
ATT_HQ = 8
ATT_HKV = 2
ROPE_BASE = 10000.0
LRU_BD = 64
LRU_C = 8.0
CONV_W = 4
CONV_LEFT = 2
RET_HD = 128
RET_H = 4
RWKV_HD = 64
RWKV_LN_EPS = 64e-5
D_FF = 5632

A_COLS = 768
B_COLS = 1024
C_COLS = 2048
D_MAIN = 1536
OFF_B = A_COLS
OFF_C = OFF_B + B_COLS
OFF_D = OFF_C + C_COLS
D_COLS_PAD = 1920
Z_COLS = OFF_D + D_COLS_PAD

LANE = 128
VMEM_LIMIT = 56 * 1024 * 1024

RET_CHUNK = 256
RWKV_CHUNK = 64
RWKV_SUBCHUNKS = 4


def _cparams(sem):
    return pltpu.CompilerParams(dimension_semantics=sem, vmem_limit_bytes=VMEM_LIMIT)


def _dot(a, b):
    return jnp.dot(a.astype(BF16), b.astype(BF16), preferred_element_type=F32)


def _dot_nt(a, b):
    return lax.dot_general(a.astype(BF16), b.astype(BF16), (((1,), (1,)), ((), ())),
                           preferred_element_type=F32)


def _split(x):
    hi = x.astype(BF16)
    lo = (x - hi.astype(F32)).astype(BF16)
    return hi, lo


def _dot3(x, wh, wl):
    xh, xl = _split(x)
    return (jnp.dot(xh, wh, preferred_element_type=F32)
            + jnp.dot(xl, wh, preferred_element_type=F32)
            + jnp.dot(xh, wl, preferred_element_type=F32))


def _seg_sum(x, bd):
    xh, xl = _split(x)
    return jnp.dot(xh, bd, preferred_element_type=F32) + jnp.dot(xl, bd, preferred_element_type=F32)


def _rope(x, cos, sin_signed, half):
    n = x.shape[-1]
    lane = lax.broadcasted_iota(jnp.int32, x.shape, 1)
    first = (lane % (2 * half)) < half
    rot = jnp.where(first, pltpu.roll(x, n - half, 1), pltpu.roll(x, half, 1))
    return x * cos + rot * sin_signed


def _softplus(x):
    return jnp.maximum(x, 0.0) + jnp.log1p(jnp.exp(-jnp.abs(x)))


def _sigmoid(x):
    return 0.5 * jnp.tanh(0.5 * x) + 0.5


def _inproj_kernel(x_ref, g_ref, w_ref, o_ref, h_ref):
    @pl.when(pl.program_id(1) == 0)
    def _():
        x = x_ref[...]
        ms = jnp.mean(x * x, axis=-1, keepdims=True)
        h_ref[...] = (x * lax.rsqrt(ms + EPS) * g_ref[...]).astype(BF16)

    o_ref[...] = jnp.dot(h_ref[...], w_ref[...], preferred_element_type=F32).astype(BF16)


def _inproj(x2d, gain, w):
    t, d = x2d.shape
    n = w.shape[1]
    tm = min(1024, t)
    tn = 1920
    return pl.pallas_call(
        _inproj_kernel,
        name="inproj",
        grid=(t // tm, n // tn),
        in_specs=[pl.BlockSpec((tm, d), lambda i, j: (i, 0)),
                  pl.BlockSpec((1, d), lambda i, j: (0, 0)),
                  pl.BlockSpec((d, tn), lambda i, j: (0, j))],
        out_specs=pl.BlockSpec((tm, tn), lambda i, j: (i, j)),
        out_shape=jax.ShapeDtypeStruct((t, n), BF16),
        scratch_shapes=[pltpu.VMEM((tm, d), BF16)],
        compiler_params=_cparams(("parallel", "arbitrary")),
    )(x2d, gain, w)


def _attn_kernel(q_ref, kv_ref, cq_ref, sq_ref, ck_ref, sk_ref, gq_ref, gk_ref, bdq_ref, bdk_ref,
                 o_ref, k_s, v_s):
    @pl.when(pl.program_id(1) == 0)
    def _():
        kv = kv_ref[0]
        k = kv[:, :LANE].astype(F32)
        ms = _seg_sum(k * k, bdk_ref[...]) * (1.0 / ATT_HD)
        k = k * lax.rsqrt(ms + EPS) * gk_ref[...]
        k = _rope(k, ck_ref[...], sk_ref[...], ATT_HD // 4)
        ones = jnp.ones((kv.shape[0], ATT_HD), BF16)
        for g in range(ATT_HKV):
            k_s[g] = k[:, g * ATT_HD:(g + 1) * ATT_HD].astype(BF16)
            v_s[g] = jnp.concatenate([kv[:, LANE + g * ATT_HD:LANE + (g + 1) * ATT_HD], ones], axis=-1)

    q = q_ref[0].astype(F32)
    ms = _seg_sum(q * q, bdq_ref[...]) * (1.0 / ATT_HD)
    q = q * lax.rsqrt(ms + EPS) * gq_ref[...]
    q = _rope(q, cq_ref[...], sq_ref[...], ATT_HD // 4).astype(BF16)
    n_rep = ATT_HQ // ATT_HKV
    scores = lambda h: _dot_nt(q[:, h * ATT_HD:(h + 1) * ATT_HD], k_s[h // n_rep])
    s_next = scores(0)
    for h in range(ATT_HQ):
        g = h // n_rep
        s = s_next
        if h + 1 < ATT_HQ:
            s_next = scores(h + 1)
        m = jnp.max(s, axis=-1, keepdims=True)
        p = jnp.exp((s - m).astype(BF16))
        o = jnp.dot(p, v_s[g], preferred_element_type=F32)
        o_ref[0, :, h * ATT_HD:(h + 1) * ATT_HD] = (o[:, :ATT_HD] / o[:, ATT_HD:]).astype(BF16)


def _attention(z, tabs, gq, gk, bdq, bdk):
    b, s, _ = z.shape
    tq = min(512, s)
    cq, sq, ck, sk = tabs
    return pl.pallas_call(
        _attn_kernel,
        name="attn",
        grid=(b, s // tq),
        in_specs=[pl.BlockSpec((1, tq, 512), lambda i, j: (i, j, 0)),
                  pl.BlockSpec((1, s, 256), lambda i, j: (i, 0, 2)),
                  pl.BlockSpec((tq, 512), lambda i, j: (j, 0)),
                  pl.BlockSpec((tq, 512), lambda i, j: (j, 0)),
                  pl.BlockSpec((s, LANE), lambda i, j: (0, 0)),
                  pl.BlockSpec((s, LANE), lambda i, j: (0, 0)),
                  pl.BlockSpec((1, 512), lambda i, j: (0, 0)),
                  pl.BlockSpec((1, LANE), lambda i, j: (0, 0)),
                  pl.BlockSpec((512, 512), lambda i, j: (0, 0)),
                  pl.BlockSpec((LANE, LANE), lambda i, j: (0, 0))],
        out_specs=pl.BlockSpec((1, tq, 512), lambda i, j: (i, j, 0)),
        out_shape=jax.ShapeDtypeStruct((b, s, GROUP_W), BF16),
        scratch_shapes=[pltpu.VMEM((ATT_HKV, s, ATT_HD), BF16),
                        pltpu.VMEM((ATT_HKV, s, 2 * ATT_HD), BF16)],
        compiler_params=_cparams(("parallel", "arbitrary")),
    )(z, z, cq, sq, ck, sk, gq, gk, bdq, bdk)


def _shift_rows(v, d, fill, row):
    n = v.shape[0]
    if d > 0:
        return jnp.where(row >= d, pltpu.roll(v, d, 0), fill)
    return jnp.where(row < n + d, pltpu.roll(v, n + d, 0), fill)


SUBLANES = 8


def _group_scan(a, b, h_in, sub, reverse):
    k = 1
    while k < SUBLANES:
        if reverse:
            keep = sub <= SUBLANES - 1 - k
            b = a * jnp.where(keep, pltpu.roll(b, SUBLANES - k, 0), 0.0) + b
            a = a * jnp.where(keep, pltpu.roll(a, SUBLANES - k, 0), 1.0)
        else:
            keep = sub >= k
            b = a * jnp.where(keep, pltpu.roll(b, k, 0), 0.0) + b
            a = a * jnp.where(keep, pltpu.roll(a, k, 0), 1.0)
        k *= 2
    return a * h_in + b


def _lru_kernel(x_ref, gate_ref, cw_ref, cb_ref, wgh_ref, wgl_ref,
                ba_ref, bx_ref, lam_ref, o_ref, a_s, b_s, h_s):
    x = x_ref[0].astype(F32)
    s_len = x.shape[0]
    n_grp = s_len // SUBLANES
    row = lax.broadcasted_iota(jnp.int32, x.shape, 0)
    xc = cb_ref[...]
    for j in range(CONV_W):
        d = CONV_LEFT - j
        xs = x if d == 0 else _shift_rows(x, d, 0.0, row)
        xc = xc + xs * cw_ref[j:j + 1, :]
    pre = _dot3(xc, wgh_ref[0], wgl_ref[0])
    for d in range(2):
        r = _sigmoid(pre[:, 2 * d * LANE:(2 * d + 1) * LANE] + ba_ref[d:d + 1, :])
        i = _sigmoid(pre[:, (2 * d + 1) * LANE:(2 * d + 2) * LANE] + bx_ref[d:d + 1, :])
        log_a = -LRU_C * r * _softplus(-lam_ref[d:d + 1, :])
        a = jnp.exp(log_a)
        om = -jnp.tanh(log_a) * (a * a + 1.0)
        a_s[d] = a
        b_s[d] = jnp.where(om > 0.0, om * lax.rsqrt(om), 0.0) * (i * xc)

    sub = lax.broadcasted_iota(jnp.int32, (SUBLANES, LANE), 0)

    def body(g, carry):
        hf_in, hb_in = carry
        gf = pl.ds(pl.multiple_of(g * SUBLANES, SUBLANES), SUBLANES)
        gb = pl.ds(pl.multiple_of((n_grp - 1 - g) * SUBLANES, SUBLANES), SUBLANES)
        hf = _group_scan(a_s[0, gf, :], b_s[0, gf, :], hf_in, sub, False)
        hb = _group_scan(a_s[1, gb, :], b_s[1, gb, :], hb_in, sub, True)
        h_s[0, gf, :] = hf
        h_s[1, gb, :] = hb
        return (jnp.broadcast_to(hf[SUBLANES - 1:SUBLANES], hf.shape), jnp.broadcast_to(hb[0:1], hb.shape))

    zero = jnp.zeros((SUBLANES, LANE), F32)
    lax.fori_loop(0, n_grp, body, (zero, zero), unroll=8)
    gate = gate_ref[0].astype(F32)
    gelu = 0.5 * gate * (1.0 + jnp.tanh(0.7978845608028654 * (gate + 0.044715 * gate * gate * gate)))
    o_ref[0] = ((h_s[0] + h_s[1]) * gelu).astype(BF16)


def _rglru(z, cw, cb, wgh, wgl, ba, bx, lam):
    b, s, _ = z.shape
    nc = GROUP_W // LANE
    xo = OFF_B // LANE
    go = (OFF_B + GROUP_W) // LANE
    vec = lambda r: pl.BlockSpec((r, LANE), lambda i, c: (0, c))
    wsp = pl.BlockSpec((1, LANE, 4 * LANE), lambda i, c: (c, 0, 0))
    return pl.pallas_call(
        _lru_kernel,
        name="rglru",
        grid=(b, nc),
        in_specs=[pl.BlockSpec((1, s, LANE), lambda i, c: (i, 0, xo + c)),
                  pl.BlockSpec((1, s, LANE), lambda i, c: (i, 0, go + c)),
                  vec(CONV_W), vec(1), wsp, wsp, vec(2), vec(2), vec(2)],
        out_specs=pl.BlockSpec((1, s, LANE), lambda i, c: (i, 0, c)),
        out_shape=jax.ShapeDtypeStruct((b, s, GROUP_W), BF16),
        scratch_shapes=[pltpu.VMEM((2, s, LANE), F32)] * 3,
        compiler_params=_cparams(("parallel", "parallel")),
    )(z, z, cw, cb, wgh, wgl, ba, bx, lam)


def _ret_kernel(q_ref, k_ref, v_ref, g_ref, cos_ref, sin_ref, lg_ref, gain_ref, o_ref,
                q_s, k_s, v_s, acc, st):
    s_len = q_ref.shape[1]
    c = min(RET_CHUNK, s_len)
    nch = s_len // c
    half = RET_HD // 4
    q_s[...] = _rope(q_ref[0].astype(F32), cos_ref[...], sin_ref[...], half).astype(BF16)
    k_s[...] = (_rope(k_ref[0].astype(F32), cos_ref[...], sin_ref[...], half) * (RET_HD ** -0.5)).astype(BF16)
    v_s[...] = v_ref[0]

    lg = lg_ref[0]
    ii = lax.broadcasted_iota(jnp.int32, (c, c), 0)
    jj = lax.broadcasted_iota(jnp.int32, (c, c), 1)
    gam = jnp.exp(jnp.abs(ii - jj).astype(F32) * lg[:, :1])
    ri = lax.broadcasted_iota(jnp.int32, (c, LANE), 0).astype(F32)
    qw_f = jnp.exp((ri + 1.0) * lg)
    qw_b = jnp.exp((c - ri) * lg)
    kw_f = jnp.exp((c - 1.0 - ri) * lg)
    kw_b = jnp.exp(ri * lg)
    g_chunk = jnp.exp(c * lg)
    gain = gain_ref[...]

    def kv_inc(kc, vc, kw):
        return jnp.dot((kc.astype(F32) * kw).T.astype(BF16), vc, preferred_element_type=F32)

    st[...] = jnp.zeros_like(st)

    def fwd_body(n, carry):
        sl = pl.ds(pl.multiple_of(n * c, c), c)
        qc, kc, vc = q_s[sl, :], k_s[sl, :], v_s[sl, :]
        sc = _dot_nt(qc, kc) * gam
        o = jnp.dot(sc.astype(BF16), vc, preferred_element_type=F32)
        o = o + jnp.dot(qc, st[...].astype(BF16), preferred_element_type=F32) * qw_f
        acc[sl, :] = o
        st[...] = g_chunk * st[...] + kv_inc(kc, vc, kw_f)
        return carry

    lax.fori_loop(0, nch, fwd_body, 0, unroll=8)
    st[...] = jnp.zeros_like(st)

    def bwd_body(m, carry):
        n = nch - 1 - m
        sl = pl.ds(pl.multiple_of(n * c, c), c)
        qc, kc, vc = q_s[sl, :], k_s[sl, :], v_s[sl, :]
        o = acc[sl, :] + jnp.dot(qc, st[...].astype(BF16), preferred_element_type=F32) * qw_b
        st[...] = g_chunk * st[...] + kv_inc(kc, vc, kw_b)
        ms = jnp.mean(o * o, axis=-1, keepdims=True)
        o = o * lax.rsqrt(ms + EPS) * gain
        gate = g_ref[0, sl, :].astype(F32)
        o_ref[0, sl, :] = (o * (gate * _sigmoid(gate))).astype(BF16)
        return carry

    lax.fori_loop(0, nch, bwd_body, 0, unroll=8)


def _retention(z, cos, sin, lg, gain):
    b, s, _ = z.shape
    base = OFF_C // LANE
    zs = lambda o: pl.BlockSpec((1, s, LANE), lambda i, h: (i, 0, base + o + h))
    return pl.pallas_call(
        _ret_kernel,
        name="retention",
        grid=(b, RET_H),
        in_specs=[zs(0), zs(RET_H), zs(2 * RET_H), zs(3 * RET_H),
                  pl.BlockSpec((s, LANE), lambda i, h: (0, 0)),
                  pl.BlockSpec((s, LANE), lambda i, h: (0, 0)),
                  pl.BlockSpec((1, 1, LANE), lambda i, h: (h, 0, 0)),
                  pl.BlockSpec((1, LANE), lambda i, h: (0, h))],
        out_specs=pl.BlockSpec((1, s, LANE), lambda i, h: (i, 0, h)),
        out_shape=jax.ShapeDtypeStruct((b, s, GROUP_W), BF16),
        scratch_shapes=[pltpu.VMEM((s, LANE), BF16), pltpu.VMEM((s, LANE), BF16),
                        pltpu.VMEM((s, LANE), BF16), pltpu.VMEM((s, LANE), F32),
                        pltpu.VMEM((LANE, LANE), F32)],
        compiler_params=_cparams(("parallel", "parallel")),
    )(z, z, z, z, cos, sin, lg, gain)


def _rwkv_prep_kernel(z_ref, zp_ref, zn_ref, mu_ref, w2h_ref, w2l_ref, a2h_ref, a2l_ref,
                      g2h_ref, g2l_ref, w0_ref, a0_ref, kk_ref, ka_ref, bd_ref,
                      r_o, k_o, v_o, kk_o, b_o, lwf_o, lwb_o, g_o):
    i = pl.program_id(1)
    last = pl.num_programs(1) - 1
    z = z_ref[0].astype(F32)
    tt = z.shape[0]
    hr = zp_ref.shape[1]
    prev_row = jnp.where(i > 0, zp_ref[0].astype(F32)[hr - 1:hr, :], 0.0)
    next_row = jnp.where(i < last, zn_ref[0].astype(F32)[0:1, :], 0.0)
    row = lax.broadcasted_iota(jnp.int32, z.shape, 0)
    prev = jnp.where(row == 0, prev_row, pltpu.roll(z, 1, 0))
    nxt = jnp.where(row == tt - 1, next_row, pltpu.roll(z, tt - 1, 0))
    zs = z + (0.5 * (prev + nxt) - z) * mu_ref[...]
    g = GROUP_W
    r, k, v = zs[:, :g], zs[:, g:2 * g], zs[:, 2 * g:3 * g]
    lora_w = _dot3(jnp.tanh(zs[:, D_MAIN:D_MAIN + LANE]), w2h_ref[...], w2l_ref[...])
    lw = -0.6065306597126334 * _sigmoid(w0_ref[...] + lora_w)
    a = _sigmoid(a0_ref[...] + _dot3(zs[:, D_MAIN + LANE:D_MAIN + 2 * LANE], a2h_ref[...], a2l_ref[...]))
    gg = _dot3(_sigmoid(zs[:, D_MAIN + LANE:]), g2h_ref[...], g2l_ref[...])
    kk = k * kk_ref[...]
    kk = kk * lax.rsqrt(_seg_sum(kk * kk, bd_ref[...]) + 1e-12)
    r_o[0] = r.astype(BF16)
    k_o[0] = (k * (1.0 + (a - 1.0) * ka_ref[...])).astype(BF16)
    v_o[0] = v.astype(BF16)
    kk_o[0] = kk
    b_o[0] = kk * a
    lwf_o[0] = lw[:, :g]
    lwb_o[0] = lw[:, g:]
    g_o[0] = gg.astype(BF16)


def _rwkv_prep(z, mu, w2h, w2l, a2h, a2l, g2h, g2l, w0, a0, k_k, k_a, bd):
    b, s, _ = z.shape
    tt = min(256, s)
    hr = 16
    nbh = s // hr
    cb = OFF_D // D_COLS_PAD
    full = lambda a: pl.BlockSpec(a.shape, lambda i, j: (0,) * a.ndim)
    osp = pl.BlockSpec((1, tt, GROUP_W), lambda i, j: (i, j, 0))
    osh = jax.ShapeDtypeStruct((b, s, GROUP_W), BF16)
    osh32 = jax.ShapeDtypeStruct((b, s, GROUP_W), F32)
    return pl.pallas_call(
        _rwkv_prep_kernel,
        name="rwkv_prep",
        grid=(b, s // tt),
        in_specs=[pl.BlockSpec((1, tt, D_COLS_PAD), lambda i, j: (i, j, cb)),
                  pl.BlockSpec((1, hr, D_COLS_PAD), lambda i, j: (i, jnp.maximum(j * (tt // hr) - 1, 0), cb)),
                  pl.BlockSpec((1, hr, D_COLS_PAD), lambda i, j: (i, jnp.minimum((j + 1) * (tt // hr), nbh - 1), cb)),
                  full(mu), full(w2h), full(w2l), full(a2h), full(a2l), full(g2h), full(g2l),
                  full(w0), full(a0), full(k_k), full(k_a), full(bd)],
        out_specs=[osp] * 8,
        out_shape=[osh] * 3 + [osh32] * 4 + [osh],
        compiler_params=_cparams(("parallel", "parallel")),
    )(z, z, z, mu, w2h, w2l, a2h, a2l, g2h, g2l, w0, a0, k_k, k_a, bd)


def _rwkv_prepare(chains):
    c = RWKV_CHUNK
    hd = RWKV_HD
    row = lax.broadcasted_iota(jnp.int32, (c, LANE), 0)
    lane = lax.broadcasted_iota(jnp.int32, (c, LANE), 1)
    head0 = lane < hd
    col = lane % c
    m_strict = {True: row > col, False: row < col}
    m_y = {True: row >= col, False: m_strict[False]}
    eye = (row == col).astype(F32)
    bf = lambda t: t.astype(BF16)
    cat = lambda ts: jnp.concatenate(ts, axis=0)
    per_head = functools.partial(_per_head, head0=head0)
    fwds = [ch[6] for ch in chains]

    def decays(ch):
        r, lw, k, v, kk, b, fwd = ch
        cum = lw
        d = 1
        while d < c:
            cum = cum + _shift_rows(cum, d if fwd else -d, 0.0, row)
            d *= 2
        cum_ex = cum - lw
        mid = cum[c // 2:c // 2 + 1, :]
        tot = cum[c - 1:c, :] if fwd else cum[0:1, :]
        e_ex = jnp.exp(cum_ex)
        c_hat = jnp.exp(cum_ex - mid)
        c_til = jnp.exp(mid - cum)
        c_end = jnp.exp(tot - cum)
        kk_hat = kk * c_hat
        if fwd:
            r0 = r * jnp.exp(cum)
            r_hat = r * jnp.exp(cum - mid)
        else:
            r0 = r * e_ex
            r_hat = r * c_hat
        lhs = cat([bf(kk_hat), bf(r_hat)])
        rhs = cat([per_head(bf(k * c_til)), per_head(bf(b * c_til))])
        q0 = cat([bf(kk * e_ex), bf(r0)])
        rhs_e = cat([bf(k * c_end), bf(b * c_end)])
        return lhs, rhs, q0, rhs_e, per_head(bf(v)), jnp.exp(tot)

    st = [decays(ch) for ch in chains]
    gram = [lax.dot_general(s[0], s[1], (((1,), (1,)), ((), ())), preferred_element_type=F32) for s in st]
    n2 = 2 * c
    zf = jnp.zeros((c, n2), F32)
    n_mat = [jnp.where(m_strict[f], g[:c, n2:], zf) for g, f in zip(gram, fwds)]
    mk_ark = [cat([bf(jnp.where(m_strict[f], g[:c, :n2], zf)), bf(jnp.where(m_y[f], g[c:, :n2], zf))])
              for g, f in zip(gram, fwds)]
    a_rb = [bf(jnp.where(m_y[f], g[c:, n2:], zf)) for g, f in zip(gram, fwds)]

    t_inv = [eye - n for n in n_mat]
    pw = [jnp.dot(bf(n), per_head(bf(n)), preferred_element_type=F32) for n in n_mat]
    step = 4
    while step < c:
        tp = [jnp.dot(cat([bf(t), bf(p)]), per_head(bf(p)), preferred_element_type=F32)
              for t, p in zip(t_inv, pw)]
        t_inv = [t + x[:c] for t, x in zip(t_inv, tp)]
        pw = [x[c:] for x in tp]
        step *= 2
    t_inv = [t + jnp.dot(bf(t), per_head(bf(p)), preferred_element_type=F32) for t, p in zip(t_inv, pw)]
    mv = [jnp.dot(m, s[4], preferred_element_type=F32) for m, s in zip(mk_ark, st)]
    return [(s[2], s[3], s[5], bf(t), m, a, ch[3]) for s, t, m, a, ch in zip(st, t_inv, mv, a_rb, chains)]


def _per_head(t, head0):
    zero = jnp.zeros_like(t)
    return jnp.concatenate([jnp.where(head0, t, zero), jnp.where(head0, zero, t)], axis=0)


def _rwkv_apply(prep, hts):
    c = RWKV_CHUNK
    lane = lax.broadcasted_iota(jnp.int32, (c, LANE), 1)
    per_head = functools.partial(_per_head, head0=lane < RWKV_HD)
    i2 = lax.broadcasted_iota(jnp.int32, (LANE, LANE), 0)
    j2 = lax.broadcasted_iota(jnp.int32, (LANE, LANE), 1)
    head_diag = (i2 // RWKV_HD) == (j2 // RWKV_HD)
    bf = lambda t: t.astype(BF16)
    hx = [lax.dot_general(pr[0], bf(h), (((1,), (1,)), ((), ())), preferred_element_type=F32)
          for pr, h in zip(prep, hts)]
    xs = [x[:c] + pr[4][:c] for x, pr in zip(hx, prep)]
    us = [jnp.dot(pr[3], per_head(bf(x)), preferred_element_type=F32) for pr, x in zip(prep, xs)]
    ys = [pr[4][c:] - jnp.dot(pr[5], per_head(bf(u)), preferred_element_type=F32) + x[c:]
          for pr, u, x in zip(prep, us, hx)]
    lhs_t = [jnp.concatenate([pr[6], -u], axis=0).T for pr, u in zip(prep, us)]
    inc = [jnp.dot(bf(l), pr[1], preferred_element_type=F32) for l, pr in zip(lhs_t, prep)]
    zs = jnp.zeros((LANE, LANE), F32)
    h_new = [h * pr[2] + jnp.where(head_diag, i, zs) for h, pr, i in zip(hts, prep, inc)]
    return list(zip(ys, h_new))


def _rwkv_scan_kernel(rf, kf, vf, kkf, bf, lwf, rb, kb, vb, kkb, bb, lwb, yf_ref, yb_ref, h_ref):
    @pl.when(pl.program_id(1) == 0)
    def _():
        h_ref[...] = jnp.zeros_like(h_ref)

    c = RWKV_CHUNK
    n_sub = rf.shape[1] // c
    dirs = ((rf, lwf, kf, vf, kkf, bf, True), (rb, lwb, kb, vb, kkb, bb, False))
    n_pairs = GROUP_W // LANE
    keys, chains = [], []
    for d, (r_, lw_, k_, v_, kk_, b_, fwd) in enumerate(dirs):
        for p in range(n_pairs):
            sl = slice(p * LANE, (p + 1) * LANE)
            for j in range(n_sub):
                rows = slice(j * c, (j + 1) * c)
                keys.append((d, p, j))
                ld = lambda ref: ref[0, rows, sl].astype(F32)
                chains.append((ld(r_), lw_[0, rows, sl], ld(k_), ld(v_), ld(kk_), ld(b_), fwd))
    prep = dict(zip(keys, _rwkv_prepare(chains)))
    dp = [(d, p) for d in range(2) for p in range(n_pairs)]
    hts = [h_ref[d, p] for d, p in dp]
    for step in range(n_sub):
        subs = [step if d == 0 else n_sub - 1 - step for d, _ in dp]
        outs = _rwkv_apply([prep[(d, p, j)] for (d, p), j in zip(dp, subs)], hts)
        hts = [hn for _, hn in outs]
        for (d, p), j, (y, _) in zip(dp, subs, outs):
            (yf_ref, yb_ref)[d][0, j * c:(j + 1) * c, p * LANE:(p + 1) * LANE] = y.astype(BF16)
    for (d, p), hn in zip(dp, hts):
        h_ref[d, p] = hn


def _rwkv_scan(r, k, v, kk, b, lwf, lwb):
    bsz, s, _ = r.shape
    c = RWKV_CHUNK
    rows = min(RWKV_SUBCHUNKS * c, s)
    nch = s // rows
    fs = pl.BlockSpec((1, rows, GROUP_W), lambda i, n: (i, n, 0))
    bs = pl.BlockSpec((1, rows, GROUP_W), lambda i, n: (i, nch - 1 - n, 0))
    osh = jax.ShapeDtypeStruct((bsz, s, GROUP_W), BF16)
    return pl.pallas_call(
        _rwkv_scan_kernel,
        name="rwkv_scan",
        grid=(bsz, nch),
        in_specs=[fs] * 6 + [bs] * 6,
        out_specs=[fs, bs],
        out_shape=[osh, osh],
        scratch_shapes=[pltpu.VMEM((2, GROUP_W // LANE, LANE, LANE), F32)],
        compiler_params=_cparams(("parallel", "arbitrary")),
    )(r, k, v, kk, b, lwf, r, k, v, kk, b, lwb)


def _rwkv_post_kernel(yf_ref, yb_ref, r_ref, k_ref, v_ref, g_ref, rk_ref, lnw_ref, lnb_ref, bd_ref, o_ref):
    bd = bd_ref[...]
    f32 = lambda ref: ref[...].astype(F32)
    y = f32(yf_ref) + f32(yb_ref) + _seg_sum(f32(r_ref) * f32(k_ref) * rk_ref[...], bd) * f32(v_ref)
    mean = _seg_sum(y, bd) * (1.0 / RWKV_HD)
    yc = y - mean
    var = _seg_sum(yc * yc, bd) * (1.0 / RWKV_HD)
    y = yc * lax.rsqrt(var + RWKV_LN_EPS) * lnw_ref[...] + lnb_ref[...]
    o_ref[...] = (y * f32(g_ref)).astype(BF16)


def _rwkv_post(yf, yb, r, k, v, g, rk, lnw, lnb, bd):
    t = yf.shape[0]
    tm = min(1024, t)
    big = pl.BlockSpec((tm, GROUP_W), lambda i: (i, 0))
    vec = pl.BlockSpec((1, GROUP_W), lambda i: (0, 0))
    return pl.pallas_call(
        _rwkv_post_kernel,
        name="rwkv_post",
        grid=(t // tm,),
        in_specs=[big] * 6 + [vec] * 3 + [pl.BlockSpec((GROUP_W, GROUP_W), lambda i: (0, 0))],
        out_specs=big,
        out_shape=jax.ShapeDtypeStruct((t, GROUP_W), BF16),
        compiler_params=_cparams(("parallel",)),
    )(yf, yb, r, k, v, g, rk, lnw, lnb, bd)


def _outproj_kernel(ya_ref, yb_ref, yc_ref, yd_ref, ga_ref, gb_ref, w_ref, x_ref, o_ref):
    ya = ya_ref[...].astype(F32)
    ya = ya * lax.rsqrt(jnp.mean(ya * ya, axis=-1, keepdims=True) + EPS) * ga_ref[...]
    yb = yb_ref[...].astype(F32)
    yb = yb * lax.rsqrt(jnp.mean(yb * yb, axis=-1, keepdims=True) + EPS) * gb_ref[...]
    cat = jnp.concatenate([ya.astype(BF16), yb.astype(BF16), yc_ref[...], yd_ref[...]], axis=-1)
    o_ref[...] = x_ref[...] + jnp.dot(cat, w_ref[...], preferred_element_type=F32)


def _outproj(ya, yb, yc, yd, ga, gb, w, x2d):
    t, d = x2d.shape
    tm = min(512, t)
    ysp = pl.BlockSpec((tm, GROUP_W), lambda i: (i, 0))
    gsp = pl.BlockSpec((1, GROUP_W), lambda i: (0, 0))
    return pl.pallas_call(
        _outproj_kernel,
        name="outproj",
        grid=(t // tm,),
        in_specs=[ysp, ysp, ysp, ysp, gsp, gsp,
                  pl.BlockSpec((d, d), lambda i: (0, 0)),
                  pl.BlockSpec((tm, d), lambda i: (i, 0))],
        out_specs=pl.BlockSpec((tm, d), lambda i: (i, 0)),
        out_shape=jax.ShapeDtypeStruct((t, d), F32),
        compiler_params=_cparams(("parallel",)),
    )(ya, yb, yc, yd, ga, gb, w, x2d)


def _ffn_kernel(x_ref, g_ref, wg_ref, wu_ref, wd_ref, o_ref, h_ref):
    @pl.when(pl.program_id(1) == 0)
    def _():
        x = x_ref[...]
        ms = jnp.mean(x * x, axis=-1, keepdims=True)
        h_ref[...] = (x * lax.rsqrt(ms + EPS) * g_ref[...]).astype(BF16)
        o_ref[...] = x

    h = h_ref[...]
    gt = jnp.dot(h, wg_ref[...], preferred_element_type=F32)
    up = jnp.dot(h, wu_ref[...], preferred_element_type=F32)
    act = (gt * _sigmoid(gt) * up).astype(BF16)
    o_ref[...] += jnp.dot(act, wd_ref[...], preferred_element_type=F32)


def _ffn(x2d, gain, wg, wu, wd):
    t, d = x2d.shape
    f = wg.shape[1]
    tm = min(512, t)
    tf = 512
    return pl.pallas_call(
        _ffn_kernel,
        name="ffn",
        grid=(t // tm, f // tf),
        in_specs=[pl.BlockSpec((tm, d), lambda i, j: (i, 0)),
                  pl.BlockSpec((1, d), lambda i, j: (0, 0)),
                  pl.BlockSpec((d, tf), lambda i, j: (0, j)),
                  pl.BlockSpec((d, tf), lambda i, j: (0, j)),
                  pl.BlockSpec((tf, d), lambda i, j: (j, 0))],
        out_specs=pl.BlockSpec((tm, d), lambda i, j: (i, 0)),
        out_shape=jax.ShapeDtypeStruct((t, d), F32),
        scratch_shapes=[pltpu.VMEM((tm, d), BF16)],
        compiler_params=_cparams(("parallel", "arbitrary")),
    )(x2d, gain, wg, wu, wd)


def _block_diag_ones(n, seg):
    i = jnp.arange(n) // seg
    return (i[:, None] == i[None, :]).astype(BF16)


def _rope_tables(n_tok, head_dim, n_copies):
    half = head_dim // 2
    quarter = half // 2
    rows = (jnp.arange(n_tok, dtype=jnp.int32) // GRID_W).astype(F32)
    cols = (jnp.arange(n_tok, dtype=jnp.int32) % GRID_W).astype(F32)
    inv = 1.0 / (ROPE_BASE ** (jnp.arange(0, half, 2, dtype=F32) / half))
    ang_r = rows[:, None] * inv[None, :]
    ang_c = cols[:, None] * inv[None, :]
    cos = jnp.concatenate([jnp.cos(ang_r)] * 2 + [jnp.cos(ang_c)] * 2, axis=-1)
    sin = jnp.concatenate([-jnp.sin(ang_r), jnp.sin(ang_r), -jnp.sin(ang_c), jnp.sin(ang_c)], axis=-1)
    del quarter
    return jnp.tile(cos, (1, n_copies)), jnp.tile(sin, (1, n_copies))


def _pair_block_diag(w):
    lead = w.shape[:-3]
    w = w.reshape(lead + (4, 2, LRU_BD, LRU_BD))
    z = jnp.zeros_like(w[..., 0, :, :])
    top = jnp.concatenate([w[..., 0, :, :], z], axis=-1)
    bot = jnp.concatenate([z, w[..., 1, :, :]], axis=-1)
    return jnp.concatenate([top, bot], axis=-2)


def _lru_gate_weights(wa, wx):
    pa, px = _pair_block_diag(wa), _pair_block_diag(wx)
    return jnp.concatenate([pa[0], px[0], pa[1], px[1]], axis=-1)


def _split_w(w):
    hi = w.astype(BF16)
    return hi, (w - hi.astype(F32)).astype(BF16)


def _prep_layer(p):
    w_in = p["w_in"]
    pad_cols = Z_COLS - w_in.shape[1]
    w_z = jnp.pad(w_in.astype(BF16), ((0, 0), (0, pad_cols)))
    mu_pad = jnp.pad(p["rwkv_mu"], (0, pad_cols))[None, :]
    w2 = p["rwkv_w2"]
    zl = jnp.zeros_like(w2[0])
    w2cat = jnp.concatenate([jnp.concatenate([w2[0], zl], axis=1),
                             jnp.concatenate([zl, w2[1]], axis=1)], axis=0)
    zrow = jnp.zeros_like(p["rwkv_a2"])
    a2pad = jnp.concatenate([p["rwkv_a2"], zrow], axis=0)
    g2pad = jnp.concatenate([zrow, p["rwkv_g2"], zrow], axis=0)
    out = {
        "norm_mix": p["norm_mix"][None, :],
        "w_z": w_z,
        "gq": jnp.tile(p["attn_q_norm"] * (ATT_HD ** -0.5), ATT_HQ)[None, :],
        "gk": jnp.tile(p["attn_k_norm"], ATT_HKV)[None, :],
        "ga": p["attn_out_norm"][None, :],
        "cw": p["lru_conv_w"], "cb": p["lru_conv_b"][None, :],
        "w_gates": _split_w(_lru_gate_weights(p["lru_wa"], p["lru_wx"])),
        "ba": p["lru_ba"], "bx": p["lru_bx"], "lam": p["lru_lambda"],
        "gb": p["lru_out_norm"][None, :],
        "ret_gain": p["ret_norm"][None, :],
        "mu": mu_pad,
        "w2": _split_w(w2cat), "a2": _split_w(a2pad), "g2": _split_w(g2pad),
        "w0": p["rwkv_w0"].reshape(1, 2 * GROUP_W), "a0": p["rwkv_a0"][None, :],
        "k_k": p["rwkv_k_k"][None, :], "k_a": p["rwkv_k_a"][None, :],
        "r_k": p["rwkv_r_k"].reshape(1, GROUP_W),
        "ln_w": p["rwkv_ln_w"][None, :], "ln_b": p["rwkv_ln_b"][None, :],
        "w_out": p["w_out"].astype(BF16),
        "norm_ffn": p["norm_ffn"][None, :],
        "w_gate": p["w_gate"].astype(BF16), "w_up": p["w_up"].astype(BF16),
        "w_down": p["w_down"].astype(BF16),
    }
    return out


def _layer(x, lp, consts):
    bsz, s, d = x.shape
    t = bsz * s
    x2d = x.reshape(t, d)
    z = _inproj(x2d, lp["norm_mix"], lp["w_z"]).reshape(bsz, s, Z_COLS)

    ya = _attention(z, consts["att_tabs"], lp["gq"], lp["gk"], consts["bd512"], consts["bd128"])
    yb = _rglru(z, lp["cw"], lp["cb"], lp["w_gates"][0], lp["w_gates"][1], lp["ba"], lp["bx"], lp["lam"])
    yc = _retention(z, consts["ret_cos"], consts["ret_sin"], consts["ret_lg"], lp["ret_gain"])
    r, k, v, kk, b, lwf, lwb, g = _rwkv_prep(
        z, lp["mu"], lp["w2"][0], lp["w2"][1], lp["a2"][0], lp["a2"][1], lp["g2"][0], lp["g2"][1],
        lp["w0"], lp["a0"], lp["k_k"], lp["k_a"], consts["bd512"])
    yf, ybw = _rwkv_scan(r, k, v, kk, b, lwf, lwb)
    f2 = lambda a: a.reshape(t, GROUP_W)
    yd = _rwkv_post(f2(yf), f2(ybw), f2(r), f2(k), f2(v), f2(g), lp["r_k"], lp["ln_w"], lp["ln_b"],
                    consts["bd512"])
    x2d = _outproj(f2(ya), f2(yb), f2(yc), yd, lp["ga"], lp["gb"], lp["w_out"], x2d)
    x2d = _ffn(x2d, lp["norm_ffn"], lp["w_gate"], lp["w_up"], lp["w_down"])
    return x2d.reshape(bsz, s, d)


def _consts(s):
    cq, sq = _rope_tables(s, ATT_HD, ATT_HQ)
    ck, sk = _rope_tables(s, ATT_HD, ATT_HKV)
    rc, rs = _rope_tables(s, RET_HD, 1)
    log_g = jnp.log1p(-jnp.exp2(-5.0 - jnp.arange(RET_H, dtype=F32)))
    return {
        "att_tabs": (cq, sq, ck, sk),
        "ret_cos": rc, "ret_sin": rs,
        "ret_lg": jnp.broadcast_to(log_g[:, None, None], (RET_H, 1, LANE)),
        "bd512": _block_diag_ones(GROUP_W, 64),
        "bd128": _block_diag_ones(LANE, 64),
    }


def _trunk(x, layers, consts):
    for lp in layers:
        x = _layer(x, lp, consts)
    return x


def kernel(x_prompt, x_sample, norm_mix, w_in, attn_q_norm, attn_k_norm, attn_out_norm, lru_conv_w, lru_conv_b, lru_wa, lru_ba, lru_wx, lru_bx, lru_lambda, lru_out_norm, ret_norm, rwkv_mu, rwkv_w0, rwkv_w2, rwkv_a0, rwkv_a2, rwkv_g2, rwkv_k_k, rwkv_k_a, rwkv_r_k, rwkv_ln_w, rwkv_ln_b, w_out, norm_ffn, w_gate, w_up, w_down):
    params = {
        "norm_mix": norm_mix, "w_in": w_in,
        "attn_q_norm": attn_q_norm, "attn_k_norm": attn_k_norm, "attn_out_norm": attn_out_norm,
        "lru_conv_w": lru_conv_w, "lru_conv_b": lru_conv_b, "lru_wa": lru_wa, "lru_ba": lru_ba,
        "lru_wx": lru_wx, "lru_bx": lru_bx, "lru_lambda": lru_lambda, "lru_out_norm": lru_out_norm,
        "ret_norm": ret_norm,
        "rwkv_mu": rwkv_mu, "rwkv_w0": rwkv_w0, "rwkv_w2": rwkv_w2, "rwkv_a0": rwkv_a0,
        "rwkv_a2": rwkv_a2, "rwkv_g2": rwkv_g2, "rwkv_k_k": rwkv_k_k, "rwkv_k_a": rwkv_k_a,
        "rwkv_r_k": rwkv_r_k, "rwkv_ln_w": rwkv_ln_w, "rwkv_ln_b": rwkv_ln_b,
        "w_out": w_out, "norm_ffn": norm_ffn, "w_gate": w_gate, "w_up": w_up, "w_down": w_down,
    }
    depth = w_in.shape[0]
    layers = [_prep_layer({name: arr[l] for name, arr in params.items()}) for l in range(depth)]
    y_prompt = _trunk(x_prompt, layers, _consts(x_prompt.shape[1]))
    y_sample = _trunk(x_sample, layers, _consts(x_sample.shape[1]))
    return (y_prompt, y_sample)
```

```python
import functools

import jax
import jax.numpy as jnp
from jax import lax
from jax.experimental import pallas as pl
from jax.experimental.pallas import tpu as pltpu

F32 = jnp.float32
BF16 = jnp.bfloat16

D_MODEL = 2048
GRID_W = 64
EPS = 1e-6
GROUP_W = 512
ATT_HD = 64
ATT_HQ = 8
ATT_HKV = 2
ROPE_BASE = 10000.0
LRU_BD = 64
LRU_C = 8.0
CONV_W = 4
CONV_LEFT = 2
RET_HD = 128
RET_H = 4
RWKV_HD = 64
RWKV_LN_EPS = 64e-5
D_FF = 5632

A_COLS = 768
B_COLS = 1024
C_COLS = 2048
D_MAIN = 1536
OFF_B = A_COLS
OFF_C = OFF_B + B_COLS
OFF_D = OFF_C + C_COLS
D_COLS_PAD = 1920
Z_COLS = OFF_D + D_COLS_PAD

LANE = 128
VMEM_LIMIT = 56 * 1024 * 1024

RET_CHUNK = 256
RWKV_CHUNK = 64
RWKV_SUBCHUNKS = 4


def _cparams(sem):
    return pltpu.CompilerParams(dimension_semantics=sem, vmem_limit_bytes=VMEM_LIMIT)


def _dot(a, b):
    return jnp.dot(a.astype(BF16), b.astype(BF16), preferred_element_type=F32)


def _dot_nt(a, b):
    return lax.dot_general(a.astype(BF16), b.astype(BF16), (((1,), (1,)), ((), ())),
                           preferred_element_type=F32)


def _split(x):
    hi = x.astype(BF16)
    lo = (x - hi.astype(F32)).astype(BF16)
    return hi, lo


def _dot3(x, wh, wl):
    xh, xl = _split(x)
    return (jnp.dot(xh, wh, preferred_element_type=F32)
            + jnp.dot(xl, wh, preferred_element_type=F32)
            + jnp.dot(xh, wl, preferred_element_type=F32))


def _seg_sum(x, bd):
    xh, xl = _split(x)
    return jnp.dot(xh, bd, preferred_element_type=F32) + jnp.dot(xl, bd, preferred_element_type=F32)


def _rope(x, cos, sin_signed, half):
    n = x.shape[-1]
    lane = lax.broadcasted_iota(jnp.int32, x.shape, 1)
    first = (lane % (2 * half)) < half
    rot = jnp.where(first, pltpu.roll(x, n - half, 1), pltpu.roll(x, half, 1))
    return x * cos + rot * sin_signed


def _softplus(x):
    return jnp.maximum(x, 0.0) + jnp.log1p(jnp.exp(-jnp.abs(x)))


def _sigmoid(x):
    return 0.5 * jnp.tanh(0.5 * x) + 0.5


def _inproj_kernel(x_ref, g_ref, w_ref, o_ref, h_ref):
    @pl.when(pl.program_id(1) == 0)
    def _():
        x = x_ref[...]
        ms = jnp.mean(x * x, axis=-1, keepdims=True)
        h_ref[...] = (x * lax.rsqrt(ms + EPS) * g_ref[...]).astype(BF16)

    o_ref[...] = jnp.dot(h_ref[...], w_ref[...], preferred_element_type=F32).astype(BF16)


def _inproj(x2d, gain, w, l):
    t, d = x2d.shape
    n = w.shape[2]
    tm = min(1024, t)
    tn = 1920
    return pl.pallas_call(
        _inproj_kernel,
        name="inproj",
        grid=(t // tm, n // tn),
        in_specs=[pl.BlockSpec((tm, d), lambda i, j: (i, 0)),
                  pl.BlockSpec((1, d), lambda i, j: (0, 0)),
                  pl.BlockSpec((None, d, tn), lambda i, j: (l, 0, j))],
        out_specs=pl.BlockSpec((tm, tn), lambda i, j: (i, j)),
        out_shape=jax.ShapeDtypeStruct((t, n), BF16),
        scratch_shapes=[pltpu.VMEM((tm, d), BF16)],
        compiler_params=_cparams(("parallel", "arbitrary")),
    )(x2d, gain, w)


def _attn_kernel(q_ref, kv_ref, cq_ref, sq_ref, ck_ref, sk_ref, gq_ref, gk_ref, bdq_ref, bdk_ref,
                 o_ref, k_s, v_s):
    @pl.when(pl.program_id(1) == 0)
    def _():
        kv = kv_ref[0]
        k = kv[:, :LANE].astype(F32)
        ms = _seg_sum(k * k, bdk_ref[...]) * (1.0 / ATT_HD)
        k = k * lax.rsqrt(ms + EPS) * gk_ref[...]
        k = _rope(k, ck_ref[...], sk_ref[...], ATT_HD // 4)
        ones = jnp.ones((kv.shape[0], ATT_HD), BF16)
        for g in range(ATT_HKV):
            k_s[g] = k[:, g * ATT_HD:(g + 1) * ATT_HD].astype(BF16)
            v_s[g] = jnp.concatenate([kv[:, LANE + g * ATT_HD:LANE + (g + 1) * ATT_HD], ones], axis=-1)

    q = q_ref[0].astype(F32)
    ms = _seg_sum(q * q, bdq_ref[...]) * (1.0 / ATT_HD)
    q = q * lax.rsqrt(ms + EPS) * gq_ref[...]
    q = _rope(q, cq_ref[...], sq_ref[...], ATT_HD // 4).astype(BF16)
    n_rep = ATT_HQ // ATT_HKV
    scores = lambda h: _dot_nt(q[:, h * ATT_HD:(h + 1) * ATT_HD], k_s[h // n_rep])
    s_next = scores(0)
    for h in range(ATT_HQ):
        g = h // n_rep
        s = s_next
        if h + 1 < ATT_HQ:
            s_next = scores(h + 1)
        m = jnp.max(s, axis=-1, keepdims=True)
        p = jnp.exp((s - m).astype(BF16))
        o = jnp.dot(p, v_s[g], preferred_element_type=F32)
        o_ref[0, :, h * ATT_HD:(h + 1) * ATT_HD] = (o[:, :ATT_HD] / o[:, ATT_HD:]).astype(BF16)


def _attention(z, tabs, gq, gk, bdq, bdk):
    b, s, _ = z.shape
    tq = min(512, s)
    cq, sq, ck, sk = tabs
    qw = ATT_HQ * ATT_HD
    kvw = 2 * ATT_HKV * ATT_HD
    return pl.pallas_call(
        _attn_kernel,
        name="attn",
        grid=(b, s // tq),
        in_specs=[pl.BlockSpec((1, tq, qw), lambda i, j: (i, j, 0)),
                  pl.BlockSpec((1, s, kvw), lambda i, j: (i, 0, qw // kvw)),
                  pl.BlockSpec((tq, qw), lambda i, j: (j, 0)),
                  pl.BlockSpec((tq, qw), lambda i, j: (j, 0)),
                  pl.BlockSpec((s, LANE), lambda i, j: (0, 0)),
                  pl.BlockSpec((s, LANE), lambda i, j: (0, 0)),
                  pl.BlockSpec((1, qw), lambda i, j: (0, 0)),
                  pl.BlockSpec((1, LANE), lambda i, j: (0, 0)),
                  pl.BlockSpec((qw, qw), lambda i, j: (0, 0)),
                  pl.BlockSpec((LANE, LANE), lambda i, j: (0, 0))],
        out_specs=pl.BlockSpec((1, tq, qw), lambda i, j: (i, j, 0)),
        out_shape=jax.ShapeDtypeStruct((b, s, GROUP_W), BF16),
        scratch_shapes=[pltpu.VMEM((ATT_HKV, s, ATT_HD), BF16),
                        pltpu.VMEM((ATT_HKV, s, 2 * ATT_HD), BF16)],
        compiler_params=_cparams(("parallel", "arbitrary")),
    )(z, z, cq, sq, ck, sk, gq, gk, bdq, bdk)


def _shift_rows(v, d, fill, row):
    n = v.shape[0]
    if d > 0:
        return jnp.where(row >= d, pltpu.roll(v, d, 0), fill)
    return jnp.where(row < n + d, pltpu.roll(v, n + d, 0), fill)


SUBLANES = 8


def _group_scan(a, b, h_in, sub, reverse):
    k = 1
    while k < SUBLANES:
        if reverse:
            keep = sub <= SUBLANES - 1 - k
            b = a * jnp.where(keep, pltpu.roll(b, SUBLANES - k, 0), 0.0) + b
            a = a * jnp.where(keep, pltpu.roll(a, SUBLANES - k, 0), 1.0)
        else:
            keep = sub >= k
            b = a * jnp.where(keep, pltpu.roll(b, k, 0), 0.0) + b
            a = a * jnp.where(keep, pltpu.roll(a, k, 0), 1.0)
        k *= 2
    return a * h_in + b


def _lru_kernel(x_ref, gate_ref, cw_ref, cb_ref, wgh_ref, wgl_ref,
                ba_ref, bx_ref, lam_ref, o_ref, a_s, b_s, h_s):
    x = x_ref[0].astype(F32)
    s_len = x.shape[0]
    n_grp = s_len // SUBLANES
    row = lax.broadcasted_iota(jnp.int32, x.shape, 0)
    xc = cb_ref[...]
    for j in range(CONV_W):
        d = CONV_LEFT - j
        xs = x if d == 0 else _shift_rows(x, d, 0.0, row)
        xc = xc + xs * cw_ref[j:j + 1, :]
    pre = _dot3(xc, wgh_ref[0], wgl_ref[0])
    for d in range(2):
        r = _sigmoid(pre[:, 2 * d * LANE:(2 * d + 1) * LANE] + ba_ref[d:d + 1, :])
        i = _sigmoid(pre[:, (2 * d + 1) * LANE:(2 * d + 2) * LANE] + bx_ref[d:d + 1, :])
        log_a = -LRU_C * r * _softplus(-lam_ref[d:d + 1, :])
        a = jnp.exp(log_a)
        om = -jnp.tanh(log_a) * (a * a + 1.0)
        a_s[d] = a
        b_s[d] = jnp.where(om > 0.0, om * lax.rsqrt(om), 0.0) * (i * xc)

    sub = lax.broadcasted_iota(jnp.int32, (SUBLANES, LANE), 0)

    def body(g, carry):
        hf_in, hb_in = carry
        gf = pl.ds(pl.multiple_of(g * SUBLANES, SUBLANES), SUBLANES)
        gb = pl.ds(pl.multiple_of((n_grp - 1 - g) * SUBLANES, SUBLANES), SUBLANES)
        hf = _group_scan(a_s[0, gf, :], b_s[0, gf, :], hf_in, sub, False)
        hb = _group_scan(a_s[1, gb, :], b_s[1, gb, :], hb_in, sub, True)
        h_s[0, gf, :] = hf
        h_s[1, gb, :] = hb
        return (jnp.broadcast_to(hf[SUBLANES - 1:SUBLANES], hf.shape), jnp.broadcast_to(hb[0:1], hb.shape))

    zero = jnp.zeros((SUBLANES, LANE), F32)
    lax.fori_loop(0, n_grp, body, (zero, zero), unroll=8)
    gate = gate_ref[0].astype(F32)
    gelu = 0.5 * gate * (1.0 + jnp.tanh(0.7978845608028654 * (gate + 0.044715 * gate * gate * gate)))
    o_ref[0] = ((h_s[0] + h_s[1]) * gelu).astype(BF16)


def _rglru(z, cw, cb, wgh, wgl, ba, bx, lam):
    b, s, _ = z.shape
    nc = GROUP_W // LANE
    xo = OFF_B // LANE
    go = (OFF_B + GROUP_W) // LANE
    vec = lambda r: pl.BlockSpec((r, LANE), lambda i, c: (0, c))
    wsp = pl.BlockSpec((1, LANE, 4 * LANE), lambda i, c: (c, 0, 0))
    return pl.pallas_call(
        _lru_kernel,
        name="rglru",
        grid=(b, nc),
        in_specs=[pl.BlockSpec((1, s, LANE), lambda i, c: (i, 0, xo + c)),
                  pl.BlockSpec((1, s, LANE), lambda i, c: (i, 0, go + c)),
                  vec(CONV_W), vec(1), wsp, wsp, vec(2), vec(2), vec(2)],
        out_specs=pl.BlockSpec((1, s, LANE), lambda i, c: (i, 0, c)),
        out_shape=jax.ShapeDtypeStruct((b, s, GROUP_W), BF16),
        scratch_shapes=[pltpu.VMEM((2, s, LANE), F32)] * 3,
        compiler_params=_cparams(("parallel", "parallel")),
    )(z, z, cw, cb, wgh, wgl, ba, bx, lam)


def _ret_kernel(q_ref, k_ref, v_ref, g_ref, cos_ref, sin_ref, lg_ref, gain_ref, o_ref,
                q_s, k_s, v_s, acc, st):
    s_len = q_ref.shape[1]
    c = min(RET_CHUNK, s_len)
    nch = s_len // c
    half = RET_HD // 4
    q_s[...] = _rope(q_ref[0].astype(F32), cos_ref[...], sin_ref[...], half).astype(BF16)
    k_s[...] = (_rope(k_ref[0].astype(F32), cos_ref[...], sin_ref[...], half) * (RET_HD ** -0.5)).astype(BF16)
    v_s[...] = v_ref[0]

    lg = lg_ref[0]
    ii = lax.broadcasted_iota(jnp.int32, (c, c), 0)
    jj = lax.broadcasted_iota(jnp.int32, (c, c), 1)
    gam = jnp.exp(jnp.abs(ii - jj).astype(F32) * lg[:, :1])
    ri = lax.broadcasted_iota(jnp.int32, (c, LANE), 0).astype(F32)
    qw_f = jnp.exp((ri + 1.0) * lg)
    qw_b = jnp.exp((c - ri) * lg)
    kw_f = jnp.exp((c - 1.0 - ri) * lg)
    kw_b = jnp.exp(ri * lg)
    g_chunk = jnp.exp(c * lg)
    gain = gain_ref[...]

    def kv_inc(kc, vc, kw):
        return jnp.dot((kc.astype(F32) * kw).T.astype(BF16), vc, preferred_element_type=F32)

    st[...] = jnp.zeros_like(st)

    def fwd_body(n, carry):
        sl = pl.ds(pl.multiple_of(n * c, c), c)
        qc, kc, vc = q_s[sl, :], k_s[sl, :], v_s[sl, :]
        sc = _dot_nt(qc, kc) * gam
        o = jnp.dot(sc.astype(BF16), vc, preferred_element_type=F32)
        o = o + jnp.dot(qc, st[...].astype(BF16), preferred_element_type=F32) * qw_f
        acc[sl, :] = o
        st[...] = g_chunk * st[...] + kv_inc(kc, vc, kw_f)
        return carry

    lax.fori_loop(0, nch, fwd_body, 0, unroll=8)
    st[...] = jnp.zeros_like(st)

    def bwd_body(m, carry):
        n = nch - 1 - m
        sl = pl.ds(pl.multiple_of(n * c, c), c)
        qc, kc, vc = q_s[sl, :], k_s[sl, :], v_s[sl, :]
        o = acc[sl, :] + jnp.dot(qc, st[...].astype(BF16), preferred_element_type=F32) * qw_b
        st[...] = g_chunk * st[...] + kv_inc(kc, vc, kw_b)
        ms = jnp.mean(o * o, axis=-1, keepdims=True)
        o = o * lax.rsqrt(ms + EPS) * gain
        gate = g_ref[0, sl, :].astype(F32)
        o_ref[0, sl, :] = (o * (gate * _sigmoid(gate))).astype(BF16)
        return carry

    lax.fori_loop(0, nch, bwd_body, 0, unroll=8)


def _retention(z, cos, sin, lg, gain):
    b, s, _ = z.shape
    base = OFF_C // LANE
    zs = lambda o: pl.BlockSpec((1, s, LANE), lambda i, h: (i, 0, base + o + h))
    return pl.pallas_call(
        _ret_kernel,
        name="retention",
        grid=(b, RET_H),
        in_specs=[zs(0), zs(RET_H), zs(2 * RET_H), zs(3 * RET_H),
                  pl.BlockSpec((s, LANE), lambda i, h: (0, 0)),
                  pl.BlockSpec((s, LANE), lambda i, h: (0, 0)),
                  pl.BlockSpec((1, 1, LANE), lambda i, h: (h, 0, 0)),
                  pl.BlockSpec((1, LANE), lambda i, h: (0, h))],
        out_specs=pl.BlockSpec((1, s, LANE), lambda i, h: (i, 0, h)),
        out_shape=jax.ShapeDtypeStruct((b, s, GROUP_W), BF16),
        scratch_shapes=[pltpu.VMEM((s, LANE), BF16), pltpu.VMEM((s, LANE), BF16),
                        pltpu.VMEM((s, LANE), BF16), pltpu.VMEM((s, LANE), F32),
                        pltpu.VMEM((LANE, LANE), F32)],
        compiler_params=_cparams(("parallel", "parallel")),
    )(z, z, z, z, cos, sin, lg, gain)


def _rwkv_prep_kernel(z_ref, zp_ref, zn_ref, mu_ref, w2h_ref, w2l_ref, a2h_ref, a2l_ref,
                      g2h_ref, g2l_ref, w0_ref, a0_ref, kk_ref, ka_ref, bd_ref,
                      r_o, k_o, v_o, kk_o, b_o, lwf_o, lwb_o, g_o):
    i = pl.program_id(1)
    last = pl.num_programs(1) - 1
    z = z_ref[0].astype(F32)
    tt = z.shape[0]
    hr = zp_ref.shape[1]
    prev_row = jnp.where(i > 0, zp_ref[0].astype(F32)[hr - 1:hr, :], 0.0)
    next_row = jnp.where(i < last, zn_ref[0].astype(F32)[0:1, :], 0.0)
    row = lax.broadcasted_iota(jnp.int32, z.shape, 0)
    prev = jnp.where(row == 0, prev_row, pltpu.roll(z, 1, 0))
    nxt = jnp.where(row == tt - 1, next_row, pltpu.roll(z, tt - 1, 0))
    zs = z + (0.5 * (prev + nxt) - z) * mu_ref[...]
    g = GROUP_W
    r, k, v = zs[:, :g], zs[:, g:2 * g], zs[:, 2 * g:3 * g]
    lora_w = _dot3(jnp.tanh(zs[:, D_MAIN:D_MAIN + LANE]), w2h_ref[...], w2l_ref[...])
    lw = -0.6065306597126334 * _sigmoid(w0_ref[...] + lora_w)
    a = _sigmoid(a0_ref[...] + _dot3(zs[:, D_MAIN + LANE:D_MAIN + 2 * LANE], a2h_ref[...], a2l_ref[...]))
    gg = _dot3(_sigmoid(zs[:, D_MAIN + LANE:]), g2h_ref[...], g2l_ref[...])
    kk = k * kk_ref[...]
    kk = kk * lax.rsqrt(_seg_sum(kk * kk, bd_ref[...]) + 1e-12)
    r_o[0] = r.astype(BF16)
    k_o[0] = (k * (1.0 + (a - 1.0) * ka_ref[...])).astype(BF16)
    v_o[0] = v.astype(BF16)
    kk_o[0] = kk
    b_o[0] = kk * a
    lwf_o[0] = lw[:, :g]
    lwb_o[0] = lw[:, g:]
    g_o[0] = gg.astype(BF16)


def _rwkv_prep(z, mu, w2h, w2l, a2h, a2l, g2h, g2l, w0, a0, k_k, k_a, bd):
    b, s, _ = z.shape
    tt = min(256, s)
    hr = 16
    nbh = s // hr
    cb = OFF_D // D_COLS_PAD
    full = lambda a: pl.BlockSpec(a.shape, lambda i, j: (0,) * a.ndim)
    osp = pl.BlockSpec((1, tt, GROUP_W), lambda i, j: (i, j, 0))
    osh = jax.ShapeDtypeStruct((b, s, GROUP_W), BF16)
    osh32 = jax.ShapeDtypeStruct((b, s, GROUP_W), F32)
    return pl.pallas_call(
        _rwkv_prep_kernel,
        name="rwkv_prep",
        grid=(b, s // tt),
        in_specs=[pl.BlockSpec((1, tt, D_COLS_PAD), lambda i, j: (i, j, cb)),
                  pl.BlockSpec((1, hr, D_COLS_PAD), lambda i, j: (i, jnp.maximum(j * (tt // hr) - 1, 0), cb)),
                  pl.BlockSpec((1, hr, D_COLS_PAD), lambda i, j: (i, jnp.minimum((j + 1) * (tt // hr), nbh - 1), cb)),
                  full(mu), full(w2h), full(w2l), full(a2h), full(a2l), full(g2h), full(g2l),
                  full(w0), full(a0), full(k_k), full(k_a), full(bd)],
        out_specs=[osp] * 8,
        out_shape=[osh] * 3 + [osh32] * 4 + [osh],
        compiler_params=_cparams(("parallel", "parallel")),
    )(z, z, z, mu, w2h, w2l, a2h, a2l, g2h, g2l, w0, a0, k_k, k_a, bd)


def _rwkv_prepare(chains):
    c = RWKV_CHUNK
    hd = RWKV_HD
    row = lax.broadcasted_iota(jnp.int32, (c, LANE), 0)
    lane = lax.broadcasted_iota(jnp.int32, (c, LANE), 1)
    head0 = lane < hd
    col = lane % c
    m_strict = {True: row > col, False: row < col}
    m_y = {True: row >= col, False: m_strict[False]}
    eye = (row == col).astype(F32)
    bf = lambda t: t.astype(BF16)
    cat = lambda ts: jnp.concatenate(ts, axis=0)
    per_head = functools.partial(_per_head, head0=head0)
    fwds = [ch[6] for ch in chains]

    def decays(ch):
        r, lw, k, v, kk, b, fwd = ch
        cum = lw
        d = 1
        while d < c:
            cum = cum + _shift_rows(cum, d if fwd else -d, 0.0, row)
            d *= 2
        cum_ex = cum - lw
        mid = cum[c // 2:c // 2 + 1, :]
        tot = cum[c - 1:c, :] if fwd else cum[0:1, :]
        e_ex = jnp.exp(cum_ex)
        c_hat = jnp.exp(cum_ex - mid)
        c_til = jnp.exp(mid - cum)
        c_end = jnp.exp(tot - cum)
        kk_hat = kk * c_hat
        if fwd:
            r0 = r * jnp.exp(cum)
            r_hat = r * jnp.exp(cum - mid)
        else:
            r0 = r * e_ex
            r_hat = r * c_hat
        lhs = cat([bf(kk_hat), bf(r_hat)])
        rhs = cat([per_head(bf(k * c_til)), per_head(bf(b * c_til))])
        q0 = cat([bf(kk * e_ex), bf(r0)])
        rhs_e = cat([bf(k * c_end), bf(b * c_end)])
        return lhs, rhs, q0, rhs_e, per_head(bf(v)), jnp.exp(tot)

    st = [decays(ch) for ch in chains]
    gram = [lax.dot_general(s[0], s[1], (((1,), (1,)), ((), ())), preferred_element_type=F32) for s in st]
    n2 = 2 * c
    zf = jnp.zeros((c, n2), F32)
    n_mat = [jnp.where(m_strict[f], g[:c, n2:], zf) for g, f in zip(gram, fwds)]
    mk_ark = [cat([bf(jnp.where(m_strict[f], g[:c, :n2], zf)), bf(jnp.where(m_y[f], g[c:, :n2], zf))])
              for g, f in zip(gram, fwds)]
    a_rb = [bf(jnp.where(m_y[f], g[c:, n2:], zf)) for g, f in zip(gram, fwds)]

    t_inv = [eye - n for n in n_mat]
    pw = [jnp.dot(bf(n), per_head(bf(n)), preferred_element_type=F32) for n in n_mat]
    step = 4
    while step < c:
        tp = [jnp.dot(cat([bf(t), bf(p)]), per_head(bf(p)), preferred_element_type=F32)
              for t, p in zip(t_inv, pw)]
        t_inv = [t + x[:c] for t, x in zip(t_inv, tp)]
        pw = [x[c:] for x in tp]
        step *= 2
    t_inv = [t + jnp.dot(bf(t), per_head(bf(p)), preferred_element_type=F32) for t, p in zip(t_inv, pw)]
    mv = [jnp.dot(m, s[4], preferred_element_type=F32) for m, s in zip(mk_ark, st)]
    return [(s[2], s[3], s[5], bf(t), m, a, ch[3]) for s, t, m, a, ch in zip(st, t_inv, mv, a_rb, chains)]


def _per_head(t, head0):
    zero = jnp.zeros_like(t)
    return jnp.concatenate([jnp.where(head0, t, zero), jnp.where(head0, zero, t)], axis=0)


def _rwkv_apply(prep, hts):
    c = RWKV_CHUNK
    lane = lax.broadcasted_iota(jnp.int32, (c, LANE), 1)
    per_head = functools.partial(_per_head, head0=lane < RWKV_HD)
    i2 = lax.broadcasted_iota(jnp.int32, (LANE, LANE), 0)
    j2 = lax.broadcasted_iota(jnp.int32, (LANE, LANE), 1)
    head_diag = (i2 // RWKV_HD) == (j2 // RWKV_HD)
    bf = lambda t: t.astype(BF16)
    hx = [lax.dot_general(pr[0], bf(h), (((1,), (1,)), ((), ())), preferred_element_type=F32)
          for pr, h in zip(prep, hts)]
    xs = [x[:c] + pr[4][:c] for x, pr in zip(hx, prep)]
    us = [jnp.dot(pr[3], per_head(bf(x)), preferred_element_type=F32) for pr, x in zip(prep, xs)]
    ys = [pr[4][c:] - jnp.dot(pr[5], per_head(bf(u)), preferred_element_type=F32) + x[c:]
          for pr, u, x in zip(prep, us, hx)]
    lhs_t = [jnp.concatenate([pr[6], -u], axis=0).T for pr, u in zip(prep, us)]
    inc = [jnp.dot(bf(l), pr[1], preferred_element_type=F32) for l, pr in zip(lhs_t, prep)]
    zs = jnp.zeros((LANE, LANE), F32)
    h_new = [h * pr[2] + jnp.where(head_diag, i, zs) for h, pr, i in zip(hts, prep, inc)]
    return list(zip(ys, h_new))


def _rwkv_scan_kernel(rf, kf, vf, kkf, bf, lwf, rb, kb, vb, kkb, bb, lwb, yf_ref, yb_ref, h_ref):
    @pl.when(pl.program_id(1) == 0)
    def _():
        h_ref[...] = jnp.zeros_like(h_ref)

    c = RWKV_CHUNK
    n_sub = rf.shape[1] // c
    dirs = ((rf, lwf, kf, vf, kkf, bf, True), (rb, lwb, kb, vb, kkb, bb, False))
    n_pairs = GROUP_W // LANE
    keys, chains = [], []
    for d, (r_, lw_, k_, v_, kk_, b_, fwd) in enumerate(dirs):
        for p in range(n_pairs):
            sl = slice(p * LANE, (p + 1) * LANE)
            for j in range(n_sub):
                rows = slice(j * c, (j + 1) * c)
                keys.append((d, p, j))
                ld = lambda ref: ref[0, rows, sl].astype(F32)
                chains.append((ld(r_), lw_[0, rows, sl], ld(k_), ld(v_), ld(kk_), ld(b_), fwd))
    prep = dict(zip(keys, _rwkv_prepare(chains)))
    dp = [(d, p) for d in range(2) for p in range(n_pairs)]
    hts = [h_ref[d, p] for d, p in dp]
    for step in range(n_sub):
        subs = [step if d == 0 else n_sub - 1 - step for d, _ in dp]
        outs = _rwkv_apply([prep[(d, p, j)] for (d, p), j in zip(dp, subs)], hts)
        hts = [hn for _, hn in outs]
        for (d, p), j, (y, _) in zip(dp, subs, outs):
            (yf_ref, yb_ref)[d][0, j * c:(j + 1) * c, p * LANE:(p + 1) * LANE] = y.astype(BF16)
    for (d, p), hn in zip(dp, hts):
        h_ref[d, p] = hn


def _rwkv_scan(r, k, v, kk, b, lwf, lwb):
    bsz, s, _ = r.shape
    c = RWKV_CHUNK
    rows = min(RWKV_SUBCHUNKS * c, s)
    nch = s // rows
    fs = pl.BlockSpec((1, rows, GROUP_W), lambda i, n: (i, n, 0))
    bs = pl.BlockSpec((1, rows, GROUP_W), lambda i, n: (i, nch - 1 - n, 0))
    osh = jax.ShapeDtypeStruct((bsz, s, GROUP_W), BF16)
    return pl.pallas_call(
        _rwkv_scan_kernel,
        name="rwkv_scan",
        grid=(bsz, nch),
        in_specs=[fs] * 6 + [bs] * 6,
        out_specs=[fs, bs],
        out_shape=[osh, osh],
        scratch_shapes=[pltpu.VMEM((2, GROUP_W // LANE, LANE, LANE), F32)],
        compiler_params=_cparams(("parallel", "arbitrary")),
    )(r, k, v, kk, b, lwf, r, k, v, kk, b, lwb)


def _rwkv_post_kernel(yf_ref, yb_ref, r_ref, k_ref, v_ref, g_ref, rk_ref, lnw_ref, lnb_ref, bd_ref, o_ref):
    bd = bd_ref[...]
    f32 = lambda ref: ref[...].astype(F32)
    y = f32(yf_ref) + f32(yb_ref) + _seg_sum(f32(r_ref) * f32(k_ref) * rk_ref[...], bd) * f32(v_ref)
    mean = _seg_sum(y, bd) * (1.0 / RWKV_HD)
    yc = y - mean
    var = _seg_sum(yc * yc, bd) * (1.0 / RWKV_HD)
    y = yc * lax.rsqrt(var + RWKV_LN_EPS) * lnw_ref[...] + lnb_ref[...]
    o_ref[...] = (y * f32(g_ref)).astype(BF16)


def _rwkv_post(yf, yb, r, k, v, g, rk, lnw, lnb, bd):
    t = yf.shape[0]
    tm = min(1024, t)
    big = pl.BlockSpec((tm, GROUP_W), lambda i: (i, 0))
    vec = pl.BlockSpec((1, GROUP_W), lambda i: (0, 0))
    return pl.pallas_call(
        _rwkv_post_kernel,
        name="rwkv_post",
        grid=(t // tm,),
        in_specs=[big] * 6 + [vec] * 3 + [pl.BlockSpec((GROUP_W, GROUP_W), lambda i: (0, 0))],
        out_specs=big,
        out_shape=jax.ShapeDtypeStruct((t, GROUP_W), BF16),
        compiler_params=_cparams(("parallel",)),
    )(yf, yb, r, k, v, g, rk, lnw, lnb, bd)


def _outproj_kernel(ya_ref, yb_ref, yc_ref, yd_ref, ga_ref, gb_ref, w_ref, x_ref, o_ref):
    ya = ya_ref[...].astype(F32)
    ya = ya * lax.rsqrt(jnp.mean(ya * ya, axis=-1, keepdims=True) + EPS) * ga_ref[...]
    yb = yb_ref[...].astype(F32)
    yb = yb * lax.rsqrt(jnp.mean(yb * yb, axis=-1, keepdims=True) + EPS) * gb_ref[...]
    cat = jnp.concatenate([ya.astype(BF16), yb.astype(BF16), yc_ref[...], yd_ref[...]], axis=-1)
    o_ref[...] = x_ref[...] + jnp.dot(cat, w_ref[...], preferred_element_type=F32)


def _outproj(ya, yb, yc, yd, ga, gb, w, l, x2d):
    t, d = x2d.shape
    tm = min(512, t)
    ysp = pl.BlockSpec((tm, GROUP_W), lambda i: (i, 0))
    gsp = pl.BlockSpec((1, GROUP_W), lambda i: (0, 0))
    return pl.pallas_call(
        _outproj_kernel,
        name="outproj",
        grid=(t // tm,),
        in_specs=[ysp, ysp, ysp, ysp, gsp, gsp,
                  pl.BlockSpec((None, d, d), lambda i: (l, 0, 0)),
                  pl.BlockSpec((tm, d), lambda i: (i, 0))],
        out_specs=pl.BlockSpec((tm, d), lambda i: (i, 0)),
        out_shape=jax.ShapeDtypeStruct((t, d), F32),
        compiler_params=_cparams(("parallel",)),
    )(ya, yb, yc, yd, ga, gb, w, x2d)


def _ffn_kernel(x_ref, g_ref, wg_ref, wu_ref, wd_ref, o_ref, h_ref):
    @pl.when(pl.program_id(1) == 0)
    def _():
        x = x_ref[...]
        ms = jnp.mean(x * x, axis=-1, keepdims=True)
        h_ref[...] = (x * lax.rsqrt(ms + EPS) * g_ref[...]).astype(BF16)
        o_ref[...] = x

    h = h_ref[...]
    gt = jnp.dot(h, wg_ref[...], preferred_element_type=F32)
    up = jnp.dot(h, wu_ref[...], preferred_element_type=F32)
    act = (gt * _sigmoid(gt) * up).astype(BF16)
    o_ref[...] += jnp.dot(act, wd_ref[...], preferred_element_type=F32)


def _ffn(x2d, gain, wg, wu, wd, l):
    t, d = x2d.shape
    f = wg.shape[2]
    tm = min(512, t)
    tf = 512
    return pl.pallas_call(
        _ffn_kernel,
        name="ffn",
        grid=(t // tm, f // tf),
        in_specs=[pl.BlockSpec((tm, d), lambda i, j: (i, 0)),
                  pl.BlockSpec((1, d), lambda i, j: (0, 0)),
                  pl.BlockSpec((None, d, tf), lambda i, j: (l, 0, j)),
                  pl.BlockSpec((None, d, tf), lambda i, j: (l, 0, j)),
                  pl.BlockSpec((None, tf, d), lambda i, j: (l, j, 0))],
        out_specs=pl.BlockSpec((tm, d), lambda i, j: (i, 0)),
        out_shape=jax.ShapeDtypeStruct((t, d), F32),
        scratch_shapes=[pltpu.VMEM((tm, d), BF16)],
        compiler_params=_cparams(("parallel", "arbitrary")),
    )(x2d, gain, wg, wu, wd)


def _block_diag_ones(n, seg):
    i = jnp.arange(n) // seg
    return (i[:, None] == i[None, :]).astype(BF16)


def _rope_tables(n_tok, head_dim, n_copies):
    half = head_dim // 2
    quarter = half // 2
    rows = (jnp.arange(n_tok, dtype=jnp.int32) // GRID_W).astype(F32)
    cols = (jnp.arange(n_tok, dtype=jnp.int32) % GRID_W).astype(F32)
    inv = 1.0 / (ROPE_BASE ** (jnp.arange(0, half, 2, dtype=F32) / half))
    ang_r = rows[:, None] * inv[None, :]
    ang_c = cols[:, None] * inv[None, :]
    cos = jnp.concatenate([jnp.cos(ang_r)] * 2 + [jnp.cos(ang_c)] * 2, axis=-1)
    sin = jnp.concatenate([-jnp.sin(ang_r), jnp.sin(ang_r), -jnp.sin(ang_c), jnp.sin(ang_c)], axis=-1)
    del quarter
    return jnp.tile(cos, (1, n_copies)), jnp.tile(sin, (1, n_copies))


def _pair_block_diag(w):
    lead = w.shape[:-3]
    w = w.reshape(lead + (4, 2, LRU_BD, LRU_BD))
    z = jnp.zeros_like(w[..., 0, :, :])
    top = jnp.concatenate([w[..., 0, :, :], z], axis=-1)
    bot = jnp.concatenate([z, w[..., 1, :, :]], axis=-1)
    return jnp.concatenate([top, bot], axis=-2)


def _lru_gate_weights(wa, wx):
    pa, px = _pair_block_diag(wa), _pair_block_diag(wx)
    return jnp.concatenate([pa[0], px[0], pa[1], px[1]], axis=-1)


def _split_w(w):
    hi = w.astype(BF16)
    return hi, (w - hi.astype(F32)).astype(BF16)


def _prep_layer(p):
    mu_pad = jnp.pad(p["rwkv_mu"], (0, D_COLS_PAD - p["rwkv_mu"].shape[0]))[None, :]
    w2 = p["rwkv_w2"]
    zl = jnp.zeros_like(w2[0])
    w2cat = jnp.concatenate([jnp.concatenate([w2[0], zl], axis=1),
                             jnp.concatenate([zl, w2[1]], axis=1)], axis=0)
    zrow = jnp.zeros_like(p["rwkv_a2"])
    a2pad = jnp.concatenate([p["rwkv_a2"], zrow], axis=0)
    g2pad = jnp.concatenate([zrow, p["rwkv_g2"], zrow], axis=0)
    out = {
        "norm_mix": p["norm_mix"][None, :],
        "gq": jnp.tile(p["attn_q_norm"] * (ATT_HD ** -0.5), ATT_HQ)[None, :],
        "gk": jnp.tile(p["attn_k_norm"], ATT_HKV)[None, :],
        "ga": p["attn_out_norm"][None, :],
        "cw": p["lru_conv_w"], "cb": p["lru_conv_b"][None, :],
        "w_gates": _split_w(_lru_gate_weights(p["lru_wa"], p["lru_wx"])),
        "ba": p["lru_ba"], "bx": p["lru_bx"], "lam": p["lru_lambda"],
        "gb": p["lru_out_norm"][None, :],
        "ret_gain": p["ret_norm"][None, :],
        "mu": mu_pad,
        "w2": _split_w(w2cat), "a2": _split_w(a2pad), "g2": _split_w(g2pad),
        "w0": p["rwkv_w0"].reshape(1, 2 * GROUP_W), "a0": p["rwkv_a0"][None, :],
        "k_k": p["rwkv_k_k"][None, :], "k_a": p["rwkv_k_a"][None, :],
        "r_k": p["rwkv_r_k"].reshape(1, GROUP_W),
        "ln_w": p["rwkv_ln_w"][None, :], "ln_b": p["rwkv_ln_b"][None, :],
        "norm_ffn": p["norm_ffn"][None, :],
    }
    return out


def _prep_big(w_in, w_out, w_gate, w_up, w_down):
    w_z = jnp.pad(w_in.astype(BF16), ((0, 0), (0, 0), (0, Z_COLS - w_in.shape[2])))
    return {"w_z": w_z, "w_out": w_out.astype(BF16), "w_gate": w_gate.astype(BF16),
            "w_up": w_up.astype(BF16), "w_down": w_down.astype(BF16)}


def _layer(x, l, lp, big, consts):
    bsz, s, d = x.shape
    t = bsz * s
    x2d = x.reshape(t, d)
    z = _inproj(x2d, lp["norm_mix"], big["w_z"], l).reshape(bsz, s, Z_COLS)

    ya = _attention(z, consts["att_tabs"], lp["gq"], lp["gk"], consts["bd512"], consts["bd128"])
    yb = _rglru(z, lp["cw"], lp["cb"], lp["w_gates"][0], lp["w_gates"][1], lp["ba"], lp["bx"], lp["lam"])
    yc = _retention(z, consts["ret_cos"], consts["ret_sin"], consts["ret_lg"], lp["ret_gain"])
    r, k, v, kk, b, lwf, lwb, g = _rwkv_prep(
        z, lp["mu"], lp["w2"][0], lp["w2"][1], lp["a2"][0], lp["a2"][1], lp["g2"][0], lp["g2"][1],
        lp["w0"], lp["a0"], lp["k_k"], lp["k_a"], consts["bd512"])
    yf, ybw = _rwkv_scan(r, k, v, kk, b, lwf, lwb)
    f2 = lambda a: a.reshape(t, GROUP_W)
    yd = _rwkv_post(f2(yf), f2(ybw), f2(r), f2(k), f2(v), f2(g), lp["r_k"], lp["ln_w"], lp["ln_b"],
                    consts["bd512"])
    x2d = _outproj(f2(ya), f2(yb), f2(yc), yd, lp["ga"], lp["gb"], big["w_out"], l, x2d)
    x2d = _ffn(x2d, lp["norm_ffn"], big["w_gate"], big["w_up"], big["w_down"], l)
    return x2d.reshape(bsz, s, d)


def _consts(s):
    cq, sq = _rope_tables(s, ATT_HD, ATT_HQ)
    ck, sk = _rope_tables(s, ATT_HD, ATT_HKV)
    rc, rs = _rope_tables(s, RET_HD, 1)
    log_g = jnp.log1p(-jnp.exp2(-5.0 - jnp.arange(RET_H, dtype=F32)))
    return {
        "att_tabs": (cq, sq, ck, sk),
        "ret_cos": rc, "ret_sin": rs,
        "ret_lg": jnp.broadcast_to(log_g[:, None, None], (RET_H, 1, LANE)),
        "bd512": _block_diag_ones(GROUP_W, 64),
        "bd128": _block_diag_ones(LANE, 64),
    }


def _trunk(x, layers, big, consts):
    for l, lp in enumerate(layers):
        x = _layer(x, l, lp, big, consts)
    return x


def kernel(x_prompt, x_sample, norm_mix, w_in, attn_q_norm, attn_k_norm, attn_out_norm, lru_conv_w, lru_conv_b, lru_wa, lru_ba, lru_wx, lru_bx, lru_lambda, lru_out_norm, ret_norm, rwkv_mu, rwkv_w0, rwkv_w2, rwkv_a0, rwkv_a2, rwkv_g2, rwkv_k_k, rwkv_k_a, rwkv_r_k, rwkv_ln_w, rwkv_ln_b, w_out, norm_ffn, w_gate, w_up, w_down):
    params = {
        "norm_mix": norm_mix,
        "attn_q_norm": attn_q_norm, "attn_k_norm": attn_k_norm, "attn_out_norm": attn_out_norm,
        "lru_conv_w": lru_conv_w, "lru_conv_b": lru_conv_b, "lru_wa": lru_wa, "lru_ba": lru_ba,
        "lru_wx": lru_wx, "lru_bx": lru_bx, "lru_lambda": lru_lambda, "lru_out_norm": lru_out_norm,
        "ret_norm": ret_norm,
        "rwkv_mu": rwkv_mu, "rwkv_w0": rwkv_w0, "rwkv_w2": rwkv_w2, "rwkv_a0": rwkv_a0,
        "rwkv_a2": rwkv_a2, "rwkv_g2": rwkv_g2, "rwkv_k_k": rwkv_k_k, "rwkv_k_a": rwkv_k_a,
        "rwkv_r_k": rwkv_r_k, "rwkv_ln_w": rwkv_ln_w, "rwkv_ln_b": rwkv_ln_b,
        "norm_ffn": norm_ffn,
    }
    depth = w_in.shape[0]
    layers = [_prep_layer({name: arr[l] for name, arr in params.items()}) for l in range(depth)]
    big = _prep_big(w_in, w_out, w_gate, w_up, w_down)
    y_prompt = _trunk(x_prompt, layers, big, _consts(x_prompt.shape[1]))
    y_sample = _trunk(x_sample, layers, big, _consts(x_sample.shape[1]))
    return (y_prompt, y_sample)
```

```python
import functools

import jax
import jax.numpy as jnp
from jax import lax
from jax.experimental import pallas as pl
from jax.experimental.pallas import tpu as pltpu

F32 = jnp.float32
BF16 = jnp.bfloat16

D_MODEL = 2048
GRID_W = 64
EPS = 1e-6
GROUP_W = 512
ATT_HD = 64
ATT_HQ = 8
ATT_HKV = 2
ROPE_BASE = 10000.0
LRU_BD = 64
LRU_C = 8.0
CONV_W = 4
CONV_LEFT = 2
RET_HD = 128
RET_H = 4
RWKV_HD = 64
RWKV_LN_EPS = 64e-5
D_FF = 5632

A_COLS = 768
B_COLS = 1024
C_COLS = 2048
D_MAIN = 1536
OFF_B = A_COLS
OFF_C = OFF_B + B_COLS
OFF_D = OFF_C + C_COLS
D_COLS_PAD = 1920
Z_COLS = OFF_D + D_COLS_PAD

LANE = 128
VMEM_LIMIT = 56 * 1024 * 1024

RET_CHUNK = 256
RWKV_CHUNK = 64
RWKV_SUBCHUNKS = 8

INPROJ_ROWS = 1024
INPROJ_COLS = 1920
ATTN_Q_ROWS = 512
RWKV_PREP_ROWS = 256
RWKV_POST_ROWS = 1024
OUTPROJ_ROWS = 512
FFN_ROWS = 512
FFN_COLS = 512


def _cparams(sem):
    return pltpu.CompilerParams(dimension_semantics=sem, vmem_limit_bytes=VMEM_LIMIT)


def _dot_nt(a, b):
    return lax.dot_general(a.astype(BF16), b.astype(BF16), (((1,), (1,)), ((), ())),
                           preferred_element_type=F32)


def _split(x):
    hi = x.astype(BF16)
    lo = (x - hi.astype(F32)).astype(BF16)
    return hi, lo


def _dot3(x, wh, wl):
    xh, xl = _split(x)
    return (jnp.dot(xh, wh, preferred_element_type=F32)
            + jnp.dot(xl, wh, preferred_element_type=F32)
            + jnp.dot(xh, wl, preferred_element_type=F32))


def _seg_sum(x, bd):
    xh, xl = _split(x)
    return jnp.dot(xh, bd, preferred_element_type=F32) + jnp.dot(xl, bd, preferred_element_type=F32)


def _rope(x, cos, sin_signed, half):
    n = x.shape[-1]
    lane = lax.broadcasted_iota(jnp.int32, x.shape, 1)
    first = (lane % (2 * half)) < half
    rot = jnp.where(first, pltpu.roll(x, n - half, 1), pltpu.roll(x, half, 1))
    return x * cos + rot * sin_signed


def _softplus(x):
    return jnp.maximum(x, 0.0) + jnp.log1p(jnp.exp(-jnp.abs(x)))


def _sigmoid(x):
    return 0.5 * jnp.tanh(0.5 * x) + 0.5


def _inproj_kernel(x_ref, g_ref, w_ref, o_ref, h_ref):
    @pl.when(pl.program_id(1) == 0)
    def _():
        x = x_ref[...]
        ms = jnp.mean(x * x, axis=-1, keepdims=True)
        h_ref[...] = (x * lax.rsqrt(ms + EPS) * g_ref[...]).astype(BF16)

    o_ref[...] = jnp.dot(h_ref[...], w_ref[...], preferred_element_type=F32).astype(BF16)


def _inproj(x2d, gain, w, l):
    t, d = x2d.shape
    n = w.shape[2]
    tm = min(INPROJ_ROWS, t)
    tn = INPROJ_COLS
    return pl.pallas_call(
        _inproj_kernel,
        name="inproj",
        grid=(t // tm, n // tn),
        in_specs=[pl.BlockSpec((tm, d), lambda i, j: (i, 0)),
                  pl.BlockSpec((1, d), lambda i, j: (0, 0)),
                  pl.BlockSpec((None, d, tn), lambda i, j: (l, 0, j))],
        out_specs=pl.BlockSpec((tm, tn), lambda i, j: (i, j)),
        out_shape=jax.ShapeDtypeStruct((t, n), BF16),
        scratch_shapes=[pltpu.VMEM((tm, d), BF16)],
        compiler_params=_cparams(("parallel", "arbitrary")),
    )(x2d, gain, w)


def _attn_kernel(q_ref, kv_ref, cq_ref, sq_ref, ck_ref, sk_ref, gq_ref, gk_ref, bdq_ref, bdk_ref,
                 o_ref, k_s, v_s):
    @pl.when(pl.program_id(1) == 0)
    def _():
        kv = kv_ref[0]
        k = kv[:, :LANE].astype(F32)
        ms = _seg_sum(k * k, bdk_ref[...]) * (1.0 / ATT_HD)
        k = k * lax.rsqrt(ms + EPS) * gk_ref[...]
        k = _rope(k, ck_ref[...], sk_ref[...], ATT_HD // 4)
        ones = jnp.ones((kv.shape[0], ATT_HD), BF16)
        for g in range(ATT_HKV):
            k_s[g] = k[:, g * ATT_HD:(g + 1) * ATT_HD].astype(BF16)
            v_s[g] = jnp.concatenate([kv[:, LANE + g * ATT_HD:LANE + (g + 1) * ATT_HD], ones], axis=-1)

    q = q_ref[0].astype(F32)
    ms = _seg_sum(q * q, bdq_ref[...]) * (1.0 / ATT_HD)
    q = q * lax.rsqrt(ms + EPS) * gq_ref[...]
    q = _rope(q, cq_ref[...], sq_ref[...], ATT_HD // 4).astype(BF16)
    n_rep = ATT_HQ // ATT_HKV
    scores = lambda h: _dot_nt(q[:, h * ATT_HD:(h + 1) * ATT_HD], k_s[h // n_rep])
    s_next = scores(0)
    for h in range(ATT_HQ):
        g = h // n_rep
        s = s_next
        if h + 1 < ATT_HQ:
            s_next = scores(h + 1)
        m = jnp.max(s, axis=-1, keepdims=True)
        p = jnp.exp((s - m).astype(BF16))
        o = jnp.dot(p, v_s[g], preferred_element_type=F32)
        o_ref[0, :, h * ATT_HD:(h + 1) * ATT_HD] = (o[:, :ATT_HD] / o[:, ATT_HD:]).astype(BF16)


def _attention(z, tabs, gq, gk, bdq, bdk):
    b, s, _ = z.shape
    tq = min(ATTN_Q_ROWS, s)
    cq, sq, ck, sk = tabs
    qw = ATT_HQ * ATT_HD
    kvw = 2 * ATT_HKV * ATT_HD
    return pl.pallas_call(
        _attn_kernel,
        name="attn",
        grid=(b, s // tq),
        in_specs=[pl.BlockSpec((1, tq, qw), lambda i, j: (i, j, 0)),
                  pl.BlockSpec((1, s, kvw), lambda i, j: (i, 0, qw // kvw)),
                  pl.BlockSpec((tq, qw), lambda i, j: (j, 0)),
                  pl.BlockSpec((tq, qw), lambda i, j: (j, 0)),
                  pl.BlockSpec((s, LANE), lambda i, j: (0, 0)),
                  pl.BlockSpec((s, LANE), lambda i, j: (0, 0)),
                  pl.BlockSpec((1, qw), lambda i, j: (0, 0)),
                  pl.BlockSpec((1, LANE), lambda i, j: (0, 0)),
                  pl.BlockSpec((qw, qw), lambda i, j: (0, 0)),
                  pl.BlockSpec((LANE, LANE), lambda i, j: (0, 0))],
        out_specs=pl.BlockSpec((1, tq, qw), lambda i, j: (i, j, 0)),
        out_shape=jax.ShapeDtypeStruct((b, s, GROUP_W), BF16),
        scratch_shapes=[pltpu.VMEM((ATT_HKV, s, ATT_HD), BF16),
                        pltpu.VMEM((ATT_HKV, s, 2 * ATT_HD), BF16)],
        compiler_params=_cparams(("parallel", "arbitrary")),
    )(z, z, cq, sq, ck, sk, gq, gk, bdq, bdk)


def _shift_rows(v, d, fill, row):
    n = v.shape[0]
    if d > 0:
        return jnp.where(row >= d, pltpu.roll(v, d, 0), fill)
    return jnp.where(row < n + d, pltpu.roll(v, n + d, 0), fill)


SUBLANES = 8


def _group_scan(a, b, h_in, sub, reverse):
    k = 1
    while k < SUBLANES:
        if reverse:
            keep = sub <= SUBLANES - 1 - k
            b = a * jnp.where(keep, pltpu.roll(b, SUBLANES - k, 0), 0.0) + b
            a = a * jnp.where(keep, pltpu.roll(a, SUBLANES - k, 0), 1.0)
        else:
            keep = sub >= k
            b = a * jnp.where(keep, pltpu.roll(b, k, 0), 0.0) + b
            a = a * jnp.where(keep, pltpu.roll(a, k, 0), 1.0)
        k *= 2
    return a * h_in + b


def _lru_kernel(x_ref, gate_ref, cw_ref, cb_ref, wgh_ref, wgl_ref,
                ba_ref, bx_ref, lam_ref, o_ref, a_s, b_s, h_s):
    x = x_ref[0].astype(F32)
    s_len = x.shape[0]
    n_grp = s_len // SUBLANES
    row = lax.broadcasted_iota(jnp.int32, x.shape, 0)
    xc = cb_ref[...]
    for j in range(CONV_W):
        d = CONV_LEFT - j
        xs = x if d == 0 else _shift_rows(x, d, 0.0, row)
        xc = xc + xs * cw_ref[j:j + 1, :]
    pre = _dot3(xc, wgh_ref[0], wgl_ref[0])
    for d in range(2):
        r = _sigmoid(pre[:, 2 * d * LANE:(2 * d + 1) * LANE] + ba_ref[d:d + 1, :])
        i = _sigmoid(pre[:, (2 * d + 1) * LANE:(2 * d + 2) * LANE] + bx_ref[d:d + 1, :])
        log_a = -LRU_C * r * _softplus(-lam_ref[d:d + 1, :])
        a = jnp.exp(log_a)
        om = -jnp.tanh(log_a) * (a * a + 1.0)
        a_s[d] = a
        b_s[d] = jnp.where(om > 0.0, om * lax.rsqrt(om), 0.0) * (i * xc)

    sub = lax.broadcasted_iota(jnp.int32, (SUBLANES, LANE), 0)

    def body(g, carry):
        hf_in, hb_in = carry
        gf = pl.ds(pl.multiple_of(g * SUBLANES, SUBLANES), SUBLANES)
        gb = pl.ds(pl.multiple_of((n_grp - 1 - g) * SUBLANES, SUBLANES), SUBLANES)
        hf = _group_scan(a_s[0, gf, :], b_s[0, gf, :], hf_in, sub, False)
        hb = _group_scan(a_s[1, gb, :], b_s[1, gb, :], hb_in, sub, True)
        h_s[0, gf, :] = hf
        h_s[1, gb, :] = hb
        return (jnp.broadcast_to(hf[SUBLANES - 1:SUBLANES], hf.shape), jnp.broadcast_to(hb[0:1], hb.shape))

    zero = jnp.zeros((SUBLANES, LANE), F32)
    lax.fori_loop(0, n_grp, body, (zero, zero), unroll=8)
    gate = gate_ref[0].astype(F32)
    gelu = 0.5 * gate * (1.0 + jnp.tanh(0.7978845608028654 * (gate + 0.044715 * gate * gate * gate)))
    o_ref[0] = ((h_s[0] + h_s[1]) * gelu).astype(BF16)


def _rglru(z, cw, cb, wgh, wgl, ba, bx, lam):
    b, s, _ = z.shape
    nc = GROUP_W // LANE
    xo = OFF_B // LANE
    go = (OFF_B + GROUP_W) // LANE
    vec = lambda r: pl.BlockSpec((r, LANE), lambda i, c: (0, c))
    wsp = pl.BlockSpec((1, LANE, 4 * LANE), lambda i, c: (c, 0, 0))
    return pl.pallas_call(
        _lru_kernel,
        name="rglru",
        grid=(b, nc),
        in_specs=[pl.BlockSpec((1, s, LANE), lambda i, c: (i, 0, xo + c)),
                  pl.BlockSpec((1, s, LANE), lambda i, c: (i, 0, go + c)),
                  vec(CONV_W), vec(1), wsp, wsp, vec(2), vec(2), vec(2)],
        out_specs=pl.BlockSpec((1, s, LANE), lambda i, c: (i, 0, c)),
        out_shape=jax.ShapeDtypeStruct((b, s, GROUP_W), BF16),
        scratch_shapes=[pltpu.VMEM((2, s, LANE), F32)] * 3,
        compiler_params=_cparams(("parallel", "parallel")),
    )(z, z, cw, cb, wgh, wgl, ba, bx, lam)


def _ret_kernel(q_ref, k_ref, v_ref, g_ref, cos_ref, sin_ref, lg_ref, gain_ref, o_ref,
                q_s, k_s, v_s, acc, st):
    s_len = q_ref.shape[1]
    c = min(RET_CHUNK, s_len)
    nch = s_len // c
    half = RET_HD // 4
    q_s[...] = _rope(q_ref[0].astype(F32), cos_ref[...], sin_ref[...], half).astype(BF16)
    k_s[...] = (_rope(k_ref[0].astype(F32), cos_ref[...], sin_ref[...], half) * (RET_HD ** -0.5)).astype(BF16)
    v_s[...] = v_ref[0]

    lg = lg_ref[0]
    ii = lax.broadcasted_iota(jnp.int32, (c, c), 0)
    jj = lax.broadcasted_iota(jnp.int32, (c, c), 1)
    gam = jnp.exp(jnp.abs(ii - jj).astype(F32) * lg[:, :1])
    ri = lax.broadcasted_iota(jnp.int32, (c, LANE), 0).astype(F32)
    qw_f = jnp.exp((ri + 1.0) * lg)
    qw_b = jnp.exp((c - ri) * lg)
    kw_f = jnp.exp((c - 1.0 - ri) * lg)
    kw_b = jnp.exp(ri * lg)
    g_chunk = jnp.exp(c * lg)
    gain = gain_ref[...]

    def kv_inc(kc, vc, kw):
        return jnp.dot((kc.astype(F32) * kw).T.astype(BF16), vc, preferred_element_type=F32)

    st[...] = jnp.zeros_like(st)

    def fwd_body(n, carry):
        sl = pl.ds(pl.multiple_of(n * c, c), c)
        qc, kc, vc = q_s[sl, :], k_s[sl, :], v_s[sl, :]
        sc = _dot_nt(qc, kc) * gam
        o = jnp.dot(sc.astype(BF16), vc, preferred_element_type=F32)
        o = o + jnp.dot(qc, st[...].astype(BF16), preferred_element_type=F32) * qw_f
        acc[sl, :] = o
        st[...] = g_chunk * st[...] + kv_inc(kc, vc, kw_f)
        return carry

    lax.fori_loop(0, nch, fwd_body, 0, unroll=8)
    st[...] = jnp.zeros_like(st)

    def bwd_body(m, carry):
        n = nch - 1 - m
        sl = pl.ds(pl.multiple_of(n * c, c), c)
        qc, kc, vc = q_s[sl, :], k_s[sl, :], v_s[sl, :]
        o = acc[sl, :] + jnp.dot(qc, st[...].astype(BF16), preferred_element_type=F32) * qw_b
        st[...] = g_chunk * st[...] + kv_inc(kc, vc, kw_b)
        ms = jnp.mean(o * o, axis=-1, keepdims=True)
        o = o * lax.rsqrt(ms + EPS) * gain
        gate = g_ref[0, sl, :].astype(F32)
        o_ref[0, sl, :] = (o * (gate * _sigmoid(gate))).astype(BF16)
        return carry

    lax.fori_loop(0, nch, bwd_body, 0, unroll=8)


def _retention(z, cos, sin, lg, gain):
    b, s, _ = z.shape
    base = OFF_C // LANE
    zs = lambda o: pl.BlockSpec((1, s, LANE), lambda i, h: (i, 0, base + o + h))
    return pl.pallas_call(
        _ret_kernel,
        name="retention",
        grid=(b, RET_H),
        in_specs=[zs(0), zs(RET_H), zs(2 * RET_H), zs(3 * RET_H),
                  pl.BlockSpec((s, LANE), lambda i, h: (0, 0)),
                  pl.BlockSpec((s, LANE), lambda i, h: (0, 0)),
                  pl.BlockSpec((1, 1, LANE), lambda i, h: (h, 0, 0)),
                  pl.BlockSpec((1, LANE), lambda i, h: (0, h))],
        out_specs=pl.BlockSpec((1, s, LANE), lambda i, h: (i, 0, h)),
        out_shape=jax.ShapeDtypeStruct((b, s, GROUP_W), BF16),
        scratch_shapes=[pltpu.VMEM((s, LANE), BF16), pltpu.VMEM((s, LANE), BF16),
                        pltpu.VMEM((s, LANE), BF16), pltpu.VMEM((s, LANE), F32),
                        pltpu.VMEM((LANE, LANE), F32)],
        compiler_params=_cparams(("parallel", "parallel")),
    )(z, z, z, z, cos, sin, lg, gain)


def _rwkv_prep_kernel(z_ref, zp_ref, zn_ref, mu_ref, w2h_ref, w2l_ref, a2h_ref, a2l_ref,
                      g2h_ref, g2l_ref, w0_ref, a0_ref, kk_ref, ka_ref, bd_ref,
                      r_o, k_o, v_o, kk_o, b_o, lwf_o, lwb_o, g_o):
    i = pl.program_id(1)
    last = pl.num_programs(1) - 1
    z = z_ref[0].astype(F32)
    tt = z.shape[0]
    hr = zp_ref.shape[1]
    prev_row = jnp.where(i > 0, zp_ref[0].astype(F32)[hr - 1:hr, :], 0.0)
    next_row = jnp.where(i < last, zn_ref[0].astype(F32)[0:1, :], 0.0)
    row = lax.broadcasted_iota(jnp.int32, z.shape, 0)
    prev = jnp.where(row == 0, prev_row, pltpu.roll(z, 1, 0))
    nxt = jnp.where(row == tt - 1, next_row, pltpu.roll(z, tt - 1, 0))
    zs = z + (0.5 * (prev + nxt) - z) * mu_ref[...]
    g = GROUP_W
    r, k, v = zs[:, :g], zs[:, g:2 * g], zs[:, 2 * g:3 * g]
    lora_w = _dot3(jnp.tanh(zs[:, D_MAIN:D_MAIN + LANE]), w2h_ref[...], w2l_ref[...])
    lw = -0.6065306597126334 * _sigmoid(w0_ref[...] + lora_w)
    a = _sigmoid(a0_ref[...] + _dot3(zs[:, D_MAIN + LANE:D_MAIN + 2 * LANE], a2h_ref[...], a2l_ref[...]))
    gg = _dot3(_sigmoid(zs[:, D_MAIN + LANE:]), g2h_ref[...], g2l_ref[...])
    kk = k * kk_ref[...]
    kk = kk * lax.rsqrt(_seg_sum(kk * kk, bd_ref[...]) + 1e-12)
    r_o[0] = r.astype(BF16)
    k_o[0] = (k * (1.0 + (a - 1.0) * ka_ref[...])).astype(BF16)
    v_o[0] = v.astype(BF16)
    kk_o[0] = kk
    b_o[0] = kk * a
    lwf_o[0] = lw[:, :g]
    lwb_o[0] = lw[:, g:]
    g_o[0] = gg.astype(BF16)


def _rwkv_prep(z, mu, w2h, w2l, a2h, a2l, g2h, g2l, w0, a0, k_k, k_a, bd):
    b, s, _ = z.shape
    tt = min(RWKV_PREP_ROWS, s)
    hr = 16
    nbh = s // hr
    cb = OFF_D // D_COLS_PAD
    full = lambda a: pl.BlockSpec(a.shape, lambda i, j: (0,) * a.ndim)
    osp = pl.BlockSpec((1, tt, GROUP_W), lambda i, j: (i, j, 0))
    osh = jax.ShapeDtypeStruct((b, s, GROUP_W), BF16)
    osh32 = jax.ShapeDtypeStruct((b, s, GROUP_W), F32)
    return pl.pallas_call(
        _rwkv_prep_kernel,
        name="rwkv_prep",
        grid=(b, s // tt),
        in_specs=[pl.BlockSpec((1, tt, D_COLS_PAD), lambda i, j: (i, j, cb)),
                  pl.BlockSpec((1, hr, D_COLS_PAD), lambda i, j: (i, jnp.maximum(j * (tt // hr) - 1, 0), cb)),
                  pl.BlockSpec((1, hr, D_COLS_PAD), lambda i, j: (i, jnp.minimum((j + 1) * (tt // hr), nbh - 1), cb)),
                  full(mu), full(w2h), full(w2l), full(a2h), full(a2l), full(g2h), full(g2l),
                  full(w0), full(a0), full(k_k), full(k_a), full(bd)],
        out_specs=[osp] * 8,
        out_shape=[osh] * 3 + [osh32] * 4 + [osh],
        compiler_params=_cparams(("parallel", "parallel")),
    )(z, z, z, mu, w2h, w2l, a2h, a2l, g2h, g2l, w0, a0, k_k, k_a, bd)


def _rwkv_prepare(chains):
    c = RWKV_CHUNK
    hd = RWKV_HD
    row = lax.broadcasted_iota(jnp.int32, (c, LANE), 0)
    lane = lax.broadcasted_iota(jnp.int32, (c, LANE), 1)
    head0 = lane < hd
    col = lane % c
    m_strict = {True: row > col, False: row < col}
    m_y = {True: row >= col, False: m_strict[False]}
    eye = (row == col).astype(F32)
    bf = lambda t: t.astype(BF16)
    cat = lambda ts: jnp.concatenate(ts, axis=0)
    per_head = functools.partial(_per_head, head0=head0)
    fwds = [ch[6] for ch in chains]

    def decays(ch):
        r, lw, k, v, kk, b, fwd = ch
        cum = lw
        d = 1
        while d < c:
            cum = cum + _shift_rows(cum, d if fwd else -d, 0.0, row)
            d *= 2
        cum_ex = cum - lw
        mid = cum[c // 2:c // 2 + 1, :]
        tot = cum[c - 1:c, :] if fwd else cum[0:1, :]
        e_ex = jnp.exp(cum_ex)
        c_hat = jnp.exp(cum_ex - mid)
        c_til = jnp.exp(mid - cum)
        c_end = jnp.exp(tot - cum)
        kk_hat = kk * c_hat
        if fwd:
            r0 = r * jnp.exp(cum)
            r_hat = r * jnp.exp(cum - mid)
        else:
            r0 = r * e_ex
            r_hat = r * c_hat
        lhs = cat([bf(kk_hat), bf(r_hat)])
        rhs = cat([per_head(bf(k * c_til)), per_head(bf(b * c_til))])
        q0 = cat([bf(kk * e_ex), bf(r0)])
        rhs_e = cat([bf(k * c_end), bf(b * c_end)])
        return lhs, rhs, q0, rhs_e, per_head(bf(v)), jnp.exp(tot)

    st = [decays(ch) for ch in chains]
    gram = [lax.dot_general(s[0], s[1], (((1,), (1,)), ((), ())), preferred_element_type=F32) for s in st]
    n2 = 2 * c
    zf = jnp.zeros((c, n2), F32)
    n_mat = [jnp.where(m_strict[f], g[:c, n2:], zf) for g, f in zip(gram, fwds)]
    mk_ark = [cat([bf(jnp.where(m_strict[f], g[:c, :n2], zf)), bf(jnp.where(m_y[f], g[c:, :n2], zf))])
              for g, f in zip(gram, fwds)]
    a_rb = [bf(jnp.where(m_y[f], g[c:, n2:], zf)) for g, f in zip(gram, fwds)]

    t_inv = [eye - n for n in n_mat]
    pw = [jnp.dot(bf(n), per_head(bf(n)), preferred_element_type=F32) for n in n_mat]
    step = 4
    while step < c:
        tp = [jnp.dot(cat([bf(t), bf(p)]), per_head(bf(p)), preferred_element_type=F32)
              for t, p in zip(t_inv, pw)]
        t_inv = [t + x[:c] for t, x in zip(t_inv, tp)]
        pw = [x[c:] for x in tp]
        step *= 2
    t_inv = [t + jnp.dot(bf(t), per_head(bf(p)), preferred_element_type=F32) for t, p in zip(t_inv, pw)]
    mv = [jnp.dot(m, s[4], preferred_element_type=F32) for m, s in zip(mk_ark, st)]
    return [(s[2], s[3], s[5], bf(t), m, a, ch[3]) for s, t, m, a, ch in zip(st, t_inv, mv, a_rb, chains)]


def _per_head(t, head0):
    zero = jnp.zeros_like(t)
    return jnp.concatenate([jnp.where(head0, t, zero), jnp.where(head0, zero, t)], axis=0)


def _rwkv_apply(prep, hts):
    c = RWKV_CHUNK
    lane = lax.broadcasted_iota(jnp.int32, (c, LANE), 1)
    per_head = functools.partial(_per_head, head0=lane < RWKV_HD)
    i2 = lax.broadcasted_iota(jnp.int32, (LANE, LANE), 0)
    j2 = lax.broadcasted_iota(jnp.int32, (LANE, LANE), 1)
    head_diag = (i2 // RWKV_HD) == (j2 // RWKV_HD)
    bf = lambda t: t.astype(BF16)
    hx = [lax.dot_general(pr[0], bf(h), (((1,), (1,)), ((), ())), preferred_element_type=F32)
          for pr, h in zip(prep, hts)]
    xs = [x[:c] + pr[4][:c] for x, pr in zip(hx, prep)]
    us = [jnp.dot(pr[3], per_head(bf(x)), preferred_element_type=F32) for pr, x in zip(prep, xs)]
    ys = [pr[4][c:] - jnp.dot(pr[5], per_head(bf(u)), preferred_element_type=F32) + x[c:]
          for pr, u, x in zip(prep, us, hx)]
    lhs_t = [jnp.concatenate([pr[6], -u], axis=0).T for pr, u in zip(prep, us)]
    inc = [jnp.dot(bf(l), pr[1], preferred_element_type=F32) for l, pr in zip(lhs_t, prep)]
    zs = jnp.zeros((LANE, LANE), F32)
    h_new = [h * pr[2] + jnp.where(head_diag, i, zs) for h, pr, i in zip(hts, prep, inc)]
    return list(zip(ys, h_new))


def _rwkv_scan_kernel(rf, kf, vf, kkf, bf, lwf, rb, kb, vb, kkb, bb, lwb, yf_ref, yb_ref, h_ref):
    @pl.when(pl.program_id(1) == 0)
    def _():
        h_ref[...] = jnp.zeros_like(h_ref)

    c = RWKV_CHUNK
    n_sub = rf.shape[1] // c
    dirs = ((rf, lwf, kf, vf, kkf, bf, True), (rb, lwb, kb, vb, kkb, bb, False))
    n_pairs = GROUP_W // LANE
    keys, chains = [], []
    for d, (r_, lw_, k_, v_, kk_, b_, fwd) in enumerate(dirs):
        for p in range(n_pairs):
            sl = slice(p * LANE, (p + 1) * LANE)
            for j in range(n_sub):
                rows = slice(j * c, (j + 1) * c)
                keys.append((d, p, j))
                ld = lambda ref: ref[0, rows, sl].astype(F32)
                chains.append((ld(r_), lw_[0, rows, sl], ld(k_), ld(v_), ld(kk_), ld(b_), fwd))
    prep = dict(zip(keys, _rwkv_prepare(chains)))
    dp = [(d, p) for d in range(2) for p in range(n_pairs)]
    hts = [h_ref[d, p] for d, p in dp]
    for step in range(n_sub):
        subs = [step if d == 0 else n_sub - 1 - step for d, _ in dp]
        outs = _rwkv_apply([prep[(d, p, j)] for (d, p), j in zip(dp, subs)], hts)
        hts = [hn for _, hn in outs]
        for (d, p), j, (y, _) in zip(dp, subs, outs):
            (yf_ref, yb_ref)[d][0, j * c:(j + 1) * c, p * LANE:(p + 1) * LANE] = y.astype(BF16)
    for (d, p), hn in zip(dp, hts):
        h_ref[d, p] = hn


def _rwkv_scan(r, k, v, kk, b, lwf, lwb):
    bsz, s, _ = r.shape
    c = RWKV_CHUNK
    rows = min(RWKV_SUBCHUNKS * c, s)
    nch = s // rows
    fs = pl.BlockSpec((1, rows, GROUP_W), lambda i, n: (i, n, 0))
    bs = pl.BlockSpec((1, rows, GROUP_W), lambda i, n: (i, nch - 1 - n, 0))
    osh = jax.ShapeDtypeStruct((bsz, s, GROUP_W), BF16)
    return pl.pallas_call(
        _rwkv_scan_kernel,
        name="rwkv_scan",
        grid=(bsz, nch),
        in_specs=[fs] * 6 + [bs] * 6,
        out_specs=[fs, bs],
        out_shape=[osh, osh],
        scratch_shapes=[pltpu.VMEM((2, GROUP_W // LANE, LANE, LANE), F32)],
        compiler_params=_cparams(("parallel", "arbitrary")),
    )(r, k, v, kk, b, lwf, r, k, v, kk, b, lwb)


def _rwkv_post_kernel(yf_ref, yb_ref, r_ref, k_ref, v_ref, g_ref, rk_ref, lnw_ref, lnb_ref, bd_ref, o_ref):
    bd = bd_ref[...]
    f32 = lambda ref: ref[...].astype(F32)
    y = f32(yf_ref) + f32(yb_ref) + _seg_sum(f32(r_ref) * f32(k_ref) * rk_ref[...], bd) * f32(v_ref)
    mean = _seg_sum(y, bd) * (1.0 / RWKV_HD)
    yc = y - mean
    var = _seg_sum(yc * yc, bd) * (1.0 / RWKV_HD)
    y = yc * lax.rsqrt(var + RWKV_LN_EPS) * lnw_ref[...] + lnb_ref[...]
    o_ref[...] = (y * f32(g_ref)).astype(BF16)


def _rwkv_post(yf, yb, r, k, v, g, rk, lnw, lnb, bd):
    t = yf.shape[0]
    tm = min(RWKV_POST_ROWS, t)
    big = pl.BlockSpec((tm, GROUP_W), lambda i: (i, 0))
    vec = pl.BlockSpec((1, GROUP_W), lambda i: (0, 0))
    return pl.pallas_call(
        _rwkv_post_kernel,
        name="rwkv_post",
        grid=(t // tm,),
        in_specs=[big] * 6 + [vec] * 3 + [pl.BlockSpec((GROUP_W, GROUP_W), lambda i: (0, 0))],
        out_specs=big,
        out_shape=jax.ShapeDtypeStruct((t, GROUP_W), BF16),
        compiler_params=_cparams(("parallel",)),
    )(yf, yb, r, k, v, g, rk, lnw, lnb, bd)


def _outproj_kernel(ya_ref, yb_ref, yc_ref, yd_ref, ga_ref, gb_ref, w_ref, x_ref, o_ref):
    ya = ya_ref[...].astype(F32)
    ya = ya * lax.rsqrt(jnp.mean(ya * ya, axis=-1, keepdims=True) + EPS) * ga_ref[...]
    yb = yb_ref[...].astype(F32)
    yb = yb * lax.rsqrt(jnp.mean(yb * yb, axis=-1, keepdims=True) + EPS) * gb_ref[...]
    cat = jnp.concatenate([ya.astype(BF16), yb.astype(BF16), yc_ref[...], yd_ref[...]], axis=-1)
    o_ref[...] = x_ref[...] + jnp.dot(cat, w_ref[...], preferred_element_type=F32)


def _outproj(ya, yb, yc, yd, ga, gb, w, l, x2d):
    t, d = x2d.shape
    tm = min(OUTPROJ_ROWS, t)
    ysp = pl.BlockSpec((tm, GROUP_W), lambda i: (i, 0))
    gsp = pl.BlockSpec((1, GROUP_W), lambda i: (0, 0))
    return pl.pallas_call(
        _outproj_kernel,
        name="outproj",
        grid=(t // tm,),
        in_specs=[ysp, ysp, ysp, ysp, gsp, gsp,
                  pl.BlockSpec((None, d, d), lambda i: (l, 0, 0)),
                  pl.BlockSpec((tm, d), lambda i: (i, 0))],
        out_specs=pl.BlockSpec((tm, d), lambda i: (i, 0)),
        out_shape=jax.ShapeDtypeStruct((t, d), F32),
        compiler_params=_cparams(("parallel",)),
    )(ya, yb, yc, yd, ga, gb, w, x2d)


def _ffn_kernel(x_ref, g_ref, wg_ref, wu_ref, wd_ref, o_ref, h_ref):
    @pl.when(pl.program_id(1) == 0)
    def _():
        x = x_ref[...]
        ms = jnp.mean(x * x, axis=-1, keepdims=True)
        h_ref[...] = (x * lax.rsqrt(ms + EPS) * g_ref[...]).astype(BF16)
        o_ref[...] = x

    h = h_ref[...]
    gt = jnp.dot(h, wg_ref[...], preferred_element_type=F32)
    up = jnp.dot(h, wu_ref[...], preferred_element_type=F32)
    act = (gt * _sigmoid(gt) * up).astype(BF16)
    o_ref[...] += jnp.dot(act, wd_ref[...], preferred_element_type=F32)


def _ffn(x2d, gain, wg, wu, wd, l):
    t, d = x2d.shape
    f = wg.shape[2]
    tm = min(FFN_ROWS, t)
    tf = FFN_COLS
    return pl.pallas_call(
        _ffn_kernel,
        name="ffn",
        grid=(t // tm, f // tf),
        in_specs=[pl.BlockSpec((tm, d), lambda i, j: (i, 0)),
                  pl.BlockSpec((1, d), lambda i, j: (0, 0)),
                  pl.BlockSpec((None, d, tf), lambda i, j: (l, 0, j)),
                  pl.BlockSpec((None, d, tf), lambda i, j: (l, 0, j)),
                  pl.BlockSpec((None, tf, d), lambda i, j: (l, j, 0))],
        out_specs=pl.BlockSpec((tm, d), lambda i, j: (i, 0)),
        out_shape=jax.ShapeDtypeStruct((t, d), F32),
        scratch_shapes=[pltpu.VMEM((tm, d), BF16)],
        compiler_params=_cparams(("parallel", "arbitrary")),
    )(x2d, gain, wg, wu, wd)


def _block_diag_ones(n, seg):
    i = jnp.arange(n) // seg
    return (i[:, None] == i[None, :]).astype(BF16)


def _rope_tables(n_tok, head_dim, n_copies):
    half = head_dim // 2
    quarter = half // 2
    rows = (jnp.arange(n_tok, dtype=jnp.int32) // GRID_W).astype(F32)
    cols = (jnp.arange(n_tok, dtype=jnp.int32) % GRID_W).astype(F32)
    inv = 1.0 / (ROPE_BASE ** (jnp.arange(0, half, 2, dtype=F32) / half))
    ang_r = rows[:, None] * inv[None, :]
    ang_c = cols[:, None] * inv[None, :]
    cos = jnp.concatenate([jnp.cos(ang_r)] * 2 + [jnp.cos(ang_c)] * 2, axis=-1)
    sin = jnp.concatenate([-jnp.sin(ang_r), jnp.sin(ang_r), -jnp.sin(ang_c), jnp.sin(ang_c)], axis=-1)
    del quarter
    return jnp.tile(cos, (1, n_copies)), jnp.tile(sin, (1, n_copies))


def _pair_block_diag(w):
    lead = w.shape[:-3]
    w = w.reshape(lead + (4, 2, LRU_BD, LRU_BD))
    z = jnp.zeros_like(w[..., 0, :, :])
    top = jnp.concatenate([w[..., 0, :, :], z], axis=-1)
    bot = jnp.concatenate([z, w[..., 1, :, :]], axis=-1)
    return jnp.concatenate([top, bot], axis=-2)


def _lru_gate_weights(wa, wx):
    pa, px = _pair_block_diag(wa), _pair_block_diag(wx)
    return jnp.concatenate([pa[0], px[0], pa[1], px[1]], axis=-1)


def _split_w(w):
    hi = w.astype(BF16)
    return hi, (w - hi.astype(F32)).astype(BF16)


def _prep_layer(p):
    mu_pad = jnp.pad(p["rwkv_mu"], (0, D_COLS_PAD - p["rwkv_mu"].shape[0]))[None, :]
    w2 = p["rwkv_w2"]
    zl = jnp.zeros_like(w2[0])
    w2cat = jnp.concatenate([jnp.concatenate([w2[0], zl], axis=1),
                             jnp.concatenate([zl, w2[1]], axis=1)], axis=0)
    zrow = jnp.zeros_like(p["rwkv_a2"])
    a2pad = jnp.concatenate([p["rwkv_a2"], zrow], axis=0)
    g2pad = jnp.concatenate([zrow, p["rwkv_g2"], zrow], axis=0)
    out = {
        "norm_mix": p["norm_mix"][None, :],
        "gq": jnp.tile(p["attn_q_norm"] * (ATT_HD ** -0.5), ATT_HQ)[None, :],
        "gk": jnp.tile(p["attn_k_norm"], ATT_HKV)[None, :],
        "ga": p["attn_out_norm"][None, :],
        "cw": p["lru_conv_w"], "cb": p["lru_conv_b"][None, :],
        "w_gates": _split_w(_lru_gate_weights(p["lru_wa"], p["lru_wx"])),
        "ba": p["lru_ba"], "bx": p["lru_bx"], "lam": p["lru_lambda"],
        "gb": p["lru_out_norm"][None, :],
        "ret_gain": p["ret_norm"][None, :],
        "mu": mu_pad,
        "w2": _split_w(w2cat), "a2": _split_w(a2pad), "g2": _split_w(g2pad),
        "w0": p["rwkv_w0"].reshape(1, 2 * GROUP_W), "a0": p["rwkv_a0"][None, :],
        "k_k": p["rwkv_k_k"][None, :], "k_a": p["rwkv_k_a"][None, :],
        "r_k": p["rwkv_r_k"].reshape(1, GROUP_W),
        "ln_w": p["rwkv_ln_w"][None, :], "ln_b": p["rwkv_ln_b"][None, :],
        "norm_ffn": p["norm_ffn"][None, :],
    }
    return out


def _prep_big(w_in, w_out, w_gate, w_up, w_down):
    w_z = jnp.pad(w_in.astype(BF16), ((0, 0), (0, 0), (0, Z_COLS - w_in.shape[2])))
    return {"w_z": w_z, "w_out": w_out.astype(BF16), "w_gate": w_gate.astype(BF16),
            "w_up": w_up.astype(BF16), "w_down": w_down.astype(BF16)}


def _layer(x, l, lp, big, consts):
    bsz, s, d = x.shape
    t = bsz * s
    x2d = x.reshape(t, d)
    z = _inproj(x2d, lp["norm_mix"], big["w_z"], l).reshape(bsz, s, Z_COLS)

    ya = _attention(z, consts["att_tabs"], lp["gq"], lp["gk"], consts["bd512"], consts["bd128"])
    yb = _rglru(z, lp["cw"], lp["cb"], lp["w_gates"][0], lp["w_gates"][1], lp["ba"], lp["bx"], lp["lam"])
    yc = _retention(z, consts["ret_cos"], consts["ret_sin"], consts["ret_lg"], lp["ret_gain"])
    r, k, v, kk, b, lwf, lwb, g = _rwkv_prep(
        z, lp["mu"], lp["w2"][0], lp["w2"][1], lp["a2"][0], lp["a2"][1], lp["g2"][0], lp["g2"][1],
        lp["w0"], lp["a0"], lp["k_k"], lp["k_a"], consts["bd512"])
    yf, ybw = _rwkv_scan(r, k, v, kk, b, lwf, lwb)
    f2 = lambda a: a.reshape(t, GROUP_W)
    yd = _rwkv_post(f2(yf), f2(ybw), f2(r), f2(k), f2(v), f2(g), lp["r_k"], lp["ln_w"], lp["ln_b"],
                    consts["bd512"])
    x2d = _outproj(f2(ya), f2(yb), f2(yc), yd, lp["ga"], lp["gb"], big["w_out"], l, x2d)
    x2d = _ffn(x2d, lp["norm_ffn"], big["w_gate"], big["w_up"], big["w_down"], l)
    return x2d.reshape(bsz, s, d)


def _consts(s):
    cq, sq = _rope_tables(s, ATT_HD, ATT_HQ)
    ck, sk = _rope_tables(s, ATT_HD, ATT_HKV)
    rc, rs = _rope_tables(s, RET_HD, 1)
    log_g = jnp.log1p(-jnp.exp2(-5.0 - jnp.arange(RET_H, dtype=F32)))
    return {
        "att_tabs": (cq, sq, ck, sk),
        "ret_cos": rc, "ret_sin": rs,
        "ret_lg": jnp.broadcast_to(log_g[:, None, None], (RET_H, 1, LANE)),
        "bd512": _block_diag_ones(GROUP_W, 64),
        "bd128": _block_diag_ones(LANE, 64),
    }


def _trunk(x, layers, big, consts):
    for l, lp in enumerate(layers):
        x = _layer(x, l, lp, big, consts)
    return x


def kernel(x_prompt, x_sample, norm_mix, w_in, attn_q_norm, attn_k_norm, attn_out_norm, lru_conv_w, lru_conv_b, lru_wa, lru_ba, lru_wx, lru_bx, lru_lambda, lru_out_norm, ret_norm, rwkv_mu, rwkv_w0, rwkv_w2, rwkv_a0, rwkv_a2, rwkv_g2, rwkv_k_k, rwkv_k_a, rwkv_r_k, rwkv_ln_w, rwkv_ln_b, w_out, norm_ffn, w_gate, w_up, w_down):
    params = {
        "norm_mix": norm_mix,
        "attn_q_norm": attn_q_norm, "attn_k_norm": attn_k_norm, "attn_out_norm": attn_out_norm,
        "lru_conv_w": lru_conv_w, "lru_conv_b": lru_conv_b, "lru_wa": lru_wa, "lru_ba": lru_ba,
        "lru_wx": lru_wx, "lru_bx": lru_bx, "lru_lambda": lru_lambda, "lru_out_norm": lru_out_norm,
        "ret_norm": ret_norm,
        "rwkv_mu": rwkv_mu, "rwkv_w0": rwkv_w0, "rwkv_w2": rwkv_w2, "rwkv_a0": rwkv_a0,
        "rwkv_a2": rwkv_a2, "rwkv_g2": rwkv_g2, "rwkv_k_k": rwkv_k_k, "rwkv_k_a": rwkv_k_a,
        "rwkv_r_k": rwkv_r_k, "rwkv_ln_w": rwkv_ln_w, "rwkv_ln_b": rwkv_ln_b,
        "norm_ffn": norm_ffn,
    }
    depth = w_in.shape[0]
    layers = [_prep_layer({name: arr[l] for name, arr in params.items()}) for l in range(depth)]
    big = _prep_big(w_in, w_out, w_gate, w_up, w_down)
    y_prompt = _trunk(x_prompt, layers, big, _consts(x_prompt.shape[1]))
    y_sample = _trunk(x_sample, layers, big, _consts(x_sample.shape[1]))
    return (y_prompt, y_sample)
```

```python
import functools

import jax
import jax.numpy as jnp
from jax import lax
from jax.experimental import pallas as pl
from jax.experimental.pallas import tpu as pltpu

F32 = jnp.float32
BF16 = jnp.bfloat16

D_MODEL = 2048
GRID_W = 64
EPS = 1e-6
GROUP_W = 512
ATT_HD = 64
ATT_HQ = 8
ATT_HKV = 2
ROPE_BASE = 10000.0
LRU_BD = 64
LRU_C = 8.0
CONV_W = 4
CONV_LEFT = 2
RET_HD = 128
RET_H = 4
RWKV_HD = 64
RWKV_LN_EPS = 64e-5
D_FF = 5632

A_COLS = 768
B_COLS = 1024
C_COLS = 2048
D_MAIN = 1536
OFF_B = A_COLS
OFF_C = OFF_B + B_COLS
OFF_D = OFF_C + C_COLS
D_COLS_PAD = 1920
Z_COLS = OFF_D + D_COLS_PAD

LANE = 128
VMEM_LIMIT = 56 * 1024 * 1024

RET_CHUNK = 256
RWKV_CHUNK = 64
RWKV_SUBCHUNKS = 8

INPROJ_ROWS = 1024
INPROJ_COLS = 1920
ATTN_Q_ROWS = 512
RWKV_PREP_ROWS = 256
RWKV_POST_ROWS = 1024
OUTPROJ_ROWS = 512
FFN_ROWS = 1024
FFN_COLS = 512


def _cparams(sem):
    return pltpu.CompilerParams(dimension_semantics=sem, vmem_limit_bytes=VMEM_LIMIT)


def _dot_nt(a, b):
    return lax.dot_general(a.astype(BF16), b.astype(BF16), (((1,), (1,)), ((), ())),
                           preferred_element_type=F32)


def _split(x):
    hi = x.astype(BF16)
    lo = (x - hi.astype(F32)).astype(BF16)
    return hi, lo


def _dot3(x, wh, wl):
    xh, xl = _split(x)
    return (jnp.dot(xh, wh, preferred_element_type=F32)
            + jnp.dot(xl, wh, preferred_element_type=F32)
            + jnp.dot(xh, wl, preferred_element_type=F32))


def _seg_sum(x, bd):
    xh, xl = _split(x)
    return jnp.dot(xh, bd, preferred_element_type=F32) + jnp.dot(xl, bd, preferred_element_type=F32)


def _rope(x, cos, sin_signed, half):
    n = x.shape[-1]
    lane = lax.broadcasted_iota(jnp.int32, x.shape, 1)
    first = (lane % (2 * half)) < half
    rot = jnp.where(first, pltpu.roll(x, n - half, 1), pltpu.roll(x, half, 1))
    return x * cos + rot * sin_signed


def _softplus(x):
    return jnp.maximum(x, 0.0) + jnp.log1p(jnp.exp(-jnp.abs(x)))


def _sigmoid(x):
    return 0.5 * jnp.tanh(0.5 * x) + 0.5


def _inproj_kernel(x_ref, g_ref, w_ref, o_ref, h_ref):
    @pl.when(pl.program_id(1) == 0)
    def _():
        x = x_ref[...]
        ms = jnp.mean(x * x, axis=-1, keepdims=True)
        h_ref[...] = (x * lax.rsqrt(ms + EPS) * g_ref[...]).astype(BF16)

    o_ref[...] = jnp.dot(h_ref[...], w_ref[...], preferred_element_type=F32).astype(BF16)


def _inproj(x2d, gain, w, l):
    t, d = x2d.shape
    n = w.shape[2]
    tm = min(INPROJ_ROWS, t)
    tn = INPROJ_COLS
    return pl.pallas_call(
        _inproj_kernel,
        name="inproj",
        grid=(t // tm, n // tn),
        in_specs=[pl.BlockSpec((tm, d), lambda i, j: (i, 0)),
                  pl.BlockSpec((1, d), lambda i, j: (0, 0)),
                  pl.BlockSpec((None, d, tn), lambda i, j: (l, 0, j))],
        out_specs=pl.BlockSpec((tm, tn), lambda i, j: (i, j)),
        out_shape=jax.ShapeDtypeStruct((t, n), BF16),
        scratch_shapes=[pltpu.VMEM((tm, d), BF16)],
        compiler_params=_cparams(("parallel", "arbitrary")),
    )(x2d, gain, w)


def _attn_kernel(q_ref, kv_ref, cq_ref, sq_ref, ck_ref, sk_ref, gq_ref, gk_ref, bdq_ref, bdk_ref,
                 o_ref, k_s, v_s):
    @pl.when(pl.program_id(1) == 0)
    def _():
        kv = kv_ref[0]
        k = kv[:, :LANE].astype(F32)
        ms = _seg_sum(k * k, bdk_ref[...]) * (1.0 / ATT_HD)
        k = k * lax.rsqrt(ms + EPS) * gk_ref[...]
        k = _rope(k, ck_ref[...], sk_ref[...], ATT_HD // 4)
        ones = jnp.ones((kv.shape[0], ATT_HD), BF16)
        for g in range(ATT_HKV):
            k_s[g] = k[:, g * ATT_HD:(g + 1) * ATT_HD].astype(BF16)
            v_s[g] = jnp.concatenate([kv[:, LANE + g * ATT_HD:LANE + (g + 1) * ATT_HD], ones], axis=-1)

    q = q_ref[0].astype(F32)
    ms = _seg_sum(q * q, bdq_ref[...]) * (1.0 / ATT_HD)
    q = q * lax.rsqrt(ms + EPS) * gq_ref[...]
    q = _rope(q, cq_ref[...], sq_ref[...], ATT_HD // 4).astype(BF16)
    n_rep = ATT_HQ // ATT_HKV
    scores = lambda h: _dot_nt(q[:, h * ATT_HD:(h + 1) * ATT_HD], k_s[h // n_rep])
    s_next = scores(0)
    for h in range(ATT_HQ):
        g = h // n_rep
        s = s_next
        if h + 1 < ATT_HQ:
            s_next = scores(h + 1)
        m = jnp.max(s, axis=-1, keepdims=True)
        p = jnp.exp((s - m).astype(BF16))
        o = jnp.dot(p, v_s[g], preferred_element_type=F32)
        o_ref[0, :, h * ATT_HD:(h + 1) * ATT_HD] = (o[:, :ATT_HD] / o[:, ATT_HD:]).astype(BF16)


def _attention(z, tabs, gq, gk, bdq, bdk):
    b, s, _ = z.shape
    tq = min(ATTN_Q_ROWS, s)
    cq, sq, ck, sk = tabs
    qw = ATT_HQ * ATT_HD
    kvw = 2 * ATT_HKV * ATT_HD
    return pl.pallas_call(
        _attn_kernel,
        name="attn",
        grid=(b, s // tq),
        in_specs=[pl.BlockSpec((1, tq, qw), lambda i, j: (i, j, 0)),
                  pl.BlockSpec((1, s, kvw), lambda i, j: (i, 0, qw // kvw)),
                  pl.BlockSpec((tq, qw), lambda i, j: (j, 0)),
                  pl.BlockSpec((tq, qw), lambda i, j: (j, 0)),
                  pl.BlockSpec((s, LANE), lambda i, j: (0, 0)),
                  pl.BlockSpec((s, LANE), lambda i, j: (0, 0)),
                  pl.BlockSpec((1, qw), lambda i, j: (0, 0)),
                  pl.BlockSpec((1, LANE), lambda i, j: (0, 0)),
                  pl.BlockSpec((qw, qw), lambda i, j: (0, 0)),
                  pl.BlockSpec((LANE, LANE), lambda i, j: (0, 0))],
        out_specs=pl.BlockSpec((1, tq, qw), lambda i, j: (i, j, 0)),
        out_shape=jax.ShapeDtypeStruct((b, s, GROUP_W), BF16),
        scratch_shapes=[pltpu.VMEM((ATT_HKV, s, ATT_HD), BF16),
                        pltpu.VMEM((ATT_HKV, s, 2 * ATT_HD), BF16)],
        compiler_params=_cparams(("parallel", "arbitrary")),
    )(z, z, cq, sq, ck, sk, gq, gk, bdq, bdk)


def _shift_rows(v, d, fill, row):
    n = v.shape[0]
    if d > 0:
        return jnp.where(row >= d, pltpu.roll(v, d, 0), fill)
    return jnp.where(row < n + d, pltpu.roll(v, n + d, 0), fill)


SUBLANES = 8


def _group_scan(a, b, h_in, sub, reverse):
    k = 1
    while k < SUBLANES:
        if reverse:
            keep = sub <= SUBLANES - 1 - k
            b = a * jnp.where(keep, pltpu.roll(b, SUBLANES - k, 0), 0.0) + b
            a = a * jnp.where(keep, pltpu.roll(a, SUBLANES - k, 0), 1.0)
        else:
            keep = sub >= k
            b = a * jnp.where(keep, pltpu.roll(b, k, 0), 0.0) + b
            a = a * jnp.where(keep, pltpu.roll(a, k, 0), 1.0)
        k *= 2
    return a * h_in + b


def _lru_kernel(x_ref, gate_ref, cw_ref, cb_ref, wgh_ref, wgl_ref,
                ba_ref, bx_ref, lam_ref, o_ref, a_s, b_s, h_s):
    x = x_ref[0].astype(F32)
    s_len = x.shape[0]
    n_grp = s_len // SUBLANES
    row = lax.broadcasted_iota(jnp.int32, x.shape, 0)
    xc = cb_ref[...]
    for j in range(CONV_W):
        d = CONV_LEFT - j
        xs = x if d == 0 else _shift_rows(x, d, 0.0, row)
        xc = xc + xs * cw_ref[j:j + 1, :]
    pre = _dot3(xc, wgh_ref[0], wgl_ref[0])
    for d in range(2):
        r = _sigmoid(pre[:, 2 * d * LANE:(2 * d + 1) * LANE] + ba_ref[d:d + 1, :])
        i = _sigmoid(pre[:, (2 * d + 1) * LANE:(2 * d + 2) * LANE] + bx_ref[d:d + 1, :])
        log_a = -LRU_C * r * _softplus(-lam_ref[d:d + 1, :])
        a = jnp.exp(log_a)
        om = -jnp.tanh(log_a) * (a * a + 1.0)
        a_s[d] = a
        b_s[d] = jnp.where(om > 0.0, om * lax.rsqrt(om), 0.0) * (i * xc)

    sub = lax.broadcasted_iota(jnp.int32, (SUBLANES, LANE), 0)

    def body(g, carry):
        hf_in, hb_in = carry
        gf = pl.ds(pl.multiple_of(g * SUBLANES, SUBLANES), SUBLANES)
        gb = pl.ds(pl.multiple_of((n_grp - 1 - g) * SUBLANES, SUBLANES), SUBLANES)
        hf = _group_scan(a_s[0, gf, :], b_s[0, gf, :], hf_in, sub, False)
        hb = _group_scan(a_s[1, gb, :], b_s[1, gb, :], hb_in, sub, True)
        h_s[0, gf, :] = hf
        h_s[1, gb, :] = hb
        return (jnp.broadcast_to(hf[SUBLANES - 1:SUBLANES], hf.shape), jnp.broadcast_to(hb[0:1], hb.shape))

    zero = jnp.zeros((SUBLANES, LANE), F32)
    lax.fori_loop(0, n_grp, body, (zero, zero), unroll=8)
    gate = gate_ref[0].astype(F32)
    gelu = 0.5 * gate * (1.0 + jnp.tanh(0.7978845608028654 * (gate + 0.044715 * gate * gate * gate)))
    o_ref[0] = ((h_s[0] + h_s[1]) * gelu).astype(BF16)


def _rglru(z, cw, cb, wgh, wgl, ba, bx, lam):
    b, s, _ = z.shape
    nc = GROUP_W // LANE
    xo = OFF_B // LANE
    go = (OFF_B + GROUP_W) // LANE
    vec = lambda r: pl.BlockSpec((r, LANE), lambda i, c: (0, c))
    wsp = pl.BlockSpec((1, LANE, 4 * LANE), lambda i, c: (c, 0, 0))
    return pl.pallas_call(
        _lru_kernel,
        name="rglru",
        grid=(b, nc),
        in_specs=[pl.BlockSpec((1, s, LANE), lambda i, c: (i, 0, xo + c)),
                  pl.BlockSpec((1, s, LANE), lambda i, c: (i, 0, go + c)),
                  vec(CONV_W), vec(1), wsp, wsp, vec(2), vec(2), vec(2)],
        out_specs=pl.BlockSpec((1, s, LANE), lambda i, c: (i, 0, c)),
        out_shape=jax.ShapeDtypeStruct((b, s, GROUP_W), BF16),
        scratch_shapes=[pltpu.VMEM((2, s, LANE), F32)] * 3,
        compiler_params=_cparams(("parallel", "parallel")),
    )(z, z, cw, cb, wgh, wgl, ba, bx, lam)


def _ret_kernel(q_ref, k_ref, v_ref, g_ref, cos_ref, sin_ref, lg_ref, gain_ref, o_ref,
                q_s, k_s, v_s, acc, st):
    s_len = q_ref.shape[1]
    c = min(RET_CHUNK, s_len)
    nch = s_len // c
    half = RET_HD // 4
    q_s[...] = _rope(q_ref[0].astype(F32), cos_ref[...], sin_ref[...], half).astype(BF16)
    k_s[...] = (_rope(k_ref[0].astype(F32), cos_ref[...], sin_ref[...], half) * (RET_HD ** -0.5)).astype(BF16)
    v_s[...] = v_ref[0]

    lg = lg_ref[0]
    ii = lax.broadcasted_iota(jnp.int32, (c, c), 0)
    jj = lax.broadcasted_iota(jnp.int32, (c, c), 1)
    gam = jnp.exp(jnp.abs(ii - jj).astype(F32) * lg[:, :1])
    ri = lax.broadcasted_iota(jnp.int32, (c, LANE), 0).astype(F32)
    qw_f = jnp.exp((ri + 1.0) * lg)
    qw_b = jnp.exp((c - ri) * lg)
    kw_f = jnp.exp((c - 1.0 - ri) * lg)
    kw_b = jnp.exp(ri * lg)
    g_chunk = jnp.exp(c * lg)
    gain = gain_ref[...]

    def kv_inc(kc, vc, kw):
        return jnp.dot((kc.astype(F32) * kw).T.astype(BF16), vc, preferred_element_type=F32)

    st[...] = jnp.zeros_like(st)

    def fwd_body(n, carry):
        sl = pl.ds(pl.multiple_of(n * c, c), c)
        qc, kc, vc = q_s[sl, :], k_s[sl, :], v_s[sl, :]
        sc = _dot_nt(qc, kc) * gam
        o = jnp.dot(sc.astype(BF16), vc, preferred_element_type=F32)
        o = o + jnp.dot(qc, st[...].astype(BF16), preferred_element_type=F32) * qw_f
        acc[sl, :] = o
        st[...] = g_chunk * st[...] + kv_inc(kc, vc, kw_f)
        return carry

    lax.fori_loop(0, nch, fwd_body, 0, unroll=8)
    st[...] = jnp.zeros_like(st)

    def bwd_body(m, carry):
        n = nch - 1 - m
        sl = pl.ds(pl.multiple_of(n * c, c), c)
        qc, kc, vc = q_s[sl, :], k_s[sl, :], v_s[sl, :]
        o = acc[sl, :] + jnp.dot(qc, st[...].astype(BF16), preferred_element_type=F32) * qw_b
        st[...] = g_chunk * st[...] + kv_inc(kc, vc, kw_b)
        ms = jnp.mean(o * o, axis=-1, keepdims=True)
        o = o * lax.rsqrt(ms + EPS) * gain
        gate = g_ref[0, sl, :].astype(F32)
        o_ref[0, sl, :] = (o * (gate * _sigmoid(gate))).astype(BF16)
        return carry

    lax.fori_loop(0, nch, bwd_body, 0, unroll=8)


def _retention(z, cos, sin, lg, gain):
    b, s, _ = z.shape
    base = OFF_C // LANE
    zs = lambda o: pl.BlockSpec((1, s, LANE), lambda i, h: (i, 0, base + o + h))
    return pl.pallas_call(
        _ret_kernel,
        name="retention",
        grid=(b, RET_H),
        in_specs=[zs(0), zs(RET_H), zs(2 * RET_H), zs(3 * RET_H),
                  pl.BlockSpec((s, LANE), lambda i, h: (0, 0)),
                  pl.BlockSpec((s, LANE), lambda i, h: (0, 0)),
                  pl.BlockSpec((1, 1, LANE), lambda i, h: (h, 0, 0)),
                  pl.BlockSpec((1, LANE), lambda i, h: (0, h))],
        out_specs=pl.BlockSpec((1, s, LANE), lambda i, h: (i, 0, h)),
        out_shape=jax.ShapeDtypeStruct((b, s, GROUP_W), BF16),
        scratch_shapes=[pltpu.VMEM((s, LANE), BF16), pltpu.VMEM((s, LANE), BF16),
                        pltpu.VMEM((s, LANE), BF16), pltpu.VMEM((s, LANE), F32),
                        pltpu.VMEM((LANE, LANE), F32)],
        compiler_params=_cparams(("parallel", "parallel")),
    )(z, z, z, z, cos, sin, lg, gain)


def _rwkv_prep_kernel(z_ref, zp_ref, zn_ref, mu_ref, w2h_ref, w2l_ref, a2h_ref, a2l_ref,
                      g2h_ref, g2l_ref, w0_ref, a0_ref, kk_ref, ka_ref, bd_ref,
                      r_o, k_o, v_o, kk_o, b_o, lwf_o, lwb_o, g_o):
    i = pl.program_id(1)
    last = pl.num_programs(1) - 1
    z = z_ref[0].astype(F32)
    tt = z.shape[0]
    hr = zp_ref.shape[1]
    prev_row = jnp.where(i > 0, zp_ref[0].astype(F32)[hr - 1:hr, :], 0.0)
    next_row = jnp.where(i < last, zn_ref[0].astype(F32)[0:1, :], 0.0)
    row = lax.broadcasted_iota(jnp.int32, z.shape, 0)
    prev = jnp.where(row == 0, prev_row, pltpu.roll(z, 1, 0))
    nxt = jnp.where(row == tt - 1, next_row, pltpu.roll(z, tt - 1, 0))
    zs = z + (0.5 * (prev + nxt) - z) * mu_ref[...]
    g = GROUP_W
    r, k, v = zs[:, :g], zs[:, g:2 * g], zs[:, 2 * g:3 * g]
    lora_w = _dot3(jnp.tanh(zs[:, D_MAIN:D_MAIN + LANE]), w2h_ref[...], w2l_ref[...])
    lw = -0.6065306597126334 * _sigmoid(w0_ref[...] + lora_w)
    a = _sigmoid(a0_ref[...] + _dot3(zs[:, D_MAIN + LANE:D_MAIN + 2 * LANE], a2h_ref[...], a2l_ref[...]))
    gg = _dot3(_sigmoid(zs[:, D_MAIN + LANE:]), g2h_ref[...], g2l_ref[...])
    kk = k * kk_ref[...]
    kk = kk * lax.rsqrt(_seg_sum(kk * kk, bd_ref[...]) + 1e-12)
    r_o[0] = r.astype(BF16)
    k_o[0] = (k * (1.0 + (a - 1.0) * ka_ref[...])).astype(BF16)
    v_o[0] = v.astype(BF16)
    kk_o[0] = kk
    b_o[0] = kk * a
    lwf_o[0] = lw[:, :g]
    lwb_o[0] = lw[:, g:]
    g_o[0] = gg.astype(BF16)


def _rwkv_prep(z, mu, w2h, w2l, a2h, a2l, g2h, g2l, w0, a0, k_k, k_a, bd):
    b, s, _ = z.shape
    tt = min(RWKV_PREP_ROWS, s)
    hr = 16
    nbh = s // hr
    cb = OFF_D // D_COLS_PAD
    full = lambda a: pl.BlockSpec(a.shape, lambda i, j: (0,) * a.ndim)
    osp = pl.BlockSpec((1, tt, GROUP_W), lambda i, j: (i, j, 0))
    osh = jax.ShapeDtypeStruct((b, s, GROUP_W), BF16)
    osh32 = jax.ShapeDtypeStruct((b, s, GROUP_W), F32)
    return pl.pallas_call(
        _rwkv_prep_kernel,
        name="rwkv_prep",
        grid=(b, s // tt),
        in_specs=[pl.BlockSpec((1, tt, D_COLS_PAD), lambda i, j: (i, j, cb)),
                  pl.BlockSpec((1, hr, D_COLS_PAD), lambda i, j: (i, jnp.maximum(j * (tt // hr) - 1, 0), cb)),
                  pl.BlockSpec((1, hr, D_COLS_PAD), lambda i, j: (i, jnp.minimum((j + 1) * (tt // hr), nbh - 1), cb)),
                  full(mu), full(w2h), full(w2l), full(a2h), full(a2l), full(g2h), full(g2l),
                  full(w0), full(a0), full(k_k), full(k_a), full(bd)],
        out_specs=[osp] * 8,
        out_shape=[osh] * 3 + [osh32] * 4 + [osh],
        compiler_params=_cparams(("parallel", "parallel")),
    )(z, z, z, mu, w2h, w2l, a2h, a2l, g2h, g2l, w0, a0, k_k, k_a, bd)


def _rwkv_prepare(chains):
    c = RWKV_CHUNK
    hd = RWKV_HD
    row = lax.broadcasted_iota(jnp.int32, (c, LANE), 0)
    lane = lax.broadcasted_iota(jnp.int32, (c, LANE), 1)
    head0 = lane < hd
    col = lane % c
    m_strict = {True: row > col, False: row < col}
    m_y = {True: row >= col, False: m_strict[False]}
    eye = (row == col).astype(F32)
    bf = lambda t: t.astype(BF16)
    cat = lambda ts: jnp.concatenate(ts, axis=0)
    per_head = functools.partial(_per_head, head0=head0)
    fwds = [ch[6] for ch in chains]

    def decays(ch):
        r, lw, k, v, kk, b, fwd = ch
        cum = lw
        d = 1
        while d < c:
            cum = cum + _shift_rows(cum, d if fwd else -d, 0.0, row)
            d *= 2
        cum_ex = cum - lw
        mid = cum[c // 2:c // 2 + 1, :]
        tot = cum[c - 1:c, :] if fwd else cum[0:1, :]
        e_ex = jnp.exp(cum_ex)
        c_hat = jnp.exp(cum_ex - mid)
        c_til = jnp.exp(mid - cum)
        c_end = jnp.exp(tot - cum)
        kk_hat = kk * c_hat
        if fwd:
            r0 = r * jnp.exp(cum)
            r_hat = r * jnp.exp(cum - mid)
        else:
            r0 = r * e_ex
            r_hat = r * c_hat
        lhs = cat([bf(kk_hat), bf(r_hat)])
        rhs = cat([per_head(bf(k * c_til)), per_head(bf(b * c_til))])
        q0 = cat([bf(kk * e_ex), bf(r0)])
        rhs_e = cat([bf(k * c_end), bf(b * c_end)])
        return lhs, rhs, q0, rhs_e, per_head(bf(v)), jnp.exp(tot)

    st = [decays(ch) for ch in chains]
    gram = [lax.dot_general(s[0], s[1], (((1,), (1,)), ((), ())), preferred_element_type=F32) for s in st]
    n2 = 2 * c
    zf = jnp.zeros((c, n2), F32)
    n_mat = [jnp.where(m_strict[f], g[:c, n2:], zf) for g, f in zip(gram, fwds)]
    mk_ark = [cat([bf(jnp.where(m_strict[f], g[:c, :n2], zf)), bf(jnp.where(m_y[f], g[c:, :n2], zf))])
              for g, f in zip(gram, fwds)]
    a_rb = [bf(jnp.where(m_y[f], g[c:, n2:], zf)) for g, f in zip(gram, fwds)]

    t_inv = [eye - n for n in n_mat]
    pw = [jnp.dot(bf(n), per_head(bf(n)), preferred_element_type=F32) for n in n_mat]
    step = 4
    while step < c:
        tp = [jnp.dot(cat([bf(t), bf(p)]), per_head(bf(p)), preferred_element_type=F32)
              for t, p in zip(t_inv, pw)]
        t_inv = [t + x[:c] for t, x in zip(t_inv, tp)]
        pw = [x[c:] for x in tp]
        step *= 2
    t_inv = [t + jnp.dot(bf(t), per_head(bf(p)), preferred_element_type=F32) for t, p in zip(t_inv, pw)]
    mv = [jnp.dot(m, s[4], preferred_element_type=F32) for m, s in zip(mk_ark, st)]
    return [(s[2], s[3], s[5], bf(t), m, a, ch[3]) for s, t, m, a, ch in zip(st, t_inv, mv, a_rb, chains)]


def _per_head(t, head0):
    zero = jnp.zeros_like(t)
    return jnp.concatenate([jnp.where(head0, t, zero), jnp.where(head0, zero, t)], axis=0)


def _rwkv_apply(prep, hts):
    c = RWKV_CHUNK
    lane = lax.broadcasted_iota(jnp.int32, (c, LANE), 1)
    per_head = functools.partial(_per_head, head0=lane < RWKV_HD)
    i2 = lax.broadcasted_iota(jnp.int32, (LANE, LANE), 0)
    j2 = lax.broadcasted_iota(jnp.int32, (LANE, LANE), 1)
    head_diag = (i2 // RWKV_HD) == (j2 // RWKV_HD)
    bf = lambda t: t.astype(BF16)
    hx = [lax.dot_general(pr[0], bf(h), (((1,), (1,)), ((), ())), preferred_element_type=F32)
          for pr, h in zip(prep, hts)]
    xs = [x[:c] + pr[4][:c] for x, pr in zip(hx, prep)]
    us = [jnp.dot(pr[3], per_head(bf(x)), preferred_element_type=F32) for pr, x in zip(prep, xs)]
    ys = [pr[4][c:] - jnp.dot(pr[5], per_head(bf(u)), preferred_element_type=F32) + x[c:]
          for pr, u, x in zip(prep, us, hx)]
    lhs_t = [jnp.concatenate([pr[6], -u], axis=0).T for pr, u in zip(prep, us)]
    inc = [jnp.dot(bf(l), pr[1], preferred_element_type=F32) for l, pr in zip(lhs_t, prep)]
    zs = jnp.zeros((LANE, LANE), F32)
    h_new = [h * pr[2] + jnp.where(head_diag, i, zs) for h, pr, i in zip(hts, prep, inc)]
    return list(zip(ys, h_new))


def _rwkv_scan_kernel(rf, kf, vf, kkf, bf, lwf, rb, kb, vb, kkb, bb, lwb, yf_ref, yb_ref, h_ref):
    @pl.when(pl.program_id(1) == 0)
    def _():
        h_ref[...] = jnp.zeros_like(h_ref)

    c = RWKV_CHUNK
    n_sub = rf.shape[1] // c
    dirs = ((rf, lwf, kf, vf, kkf, bf, True), (rb, lwb, kb, vb, kkb, bb, False))
    n_pairs = GROUP_W // LANE
    keys, chains = [], []
    for d, (r_, lw_, k_, v_, kk_, b_, fwd) in enumerate(dirs):
        for p in range(n_pairs):
            sl = slice(p * LANE, (p + 1) * LANE)
            for j in range(n_sub):
                rows = slice(j * c, (j + 1) * c)
                keys.append((d, p, j))
                ld = lambda ref: ref[0, rows, sl].astype(F32)
                chains.append((ld(r_), lw_[0, rows, sl], ld(k_), ld(v_), ld(kk_), ld(b_), fwd))
    prep = dict(zip(keys, _rwkv_prepare(chains)))
    dp = [(d, p) for d in range(2) for p in range(n_pairs)]
    hts = [h_ref[d, p] for d, p in dp]
    for step in range(n_sub):
        subs = [step if d == 0 else n_sub - 1 - step for d, _ in dp]
        outs = _rwkv_apply([prep[(d, p, j)] for (d, p), j in zip(dp, subs)], hts)
        hts = [hn for _, hn in outs]
        for (d, p), j, (y, _) in zip(dp, subs, outs):
            (yf_ref, yb_ref)[d][0, j * c:(j + 1) * c, p * LANE:(p + 1) * LANE] = y.astype(BF16)
    for (d, p), hn in zip(dp, hts):
        h_ref[d, p] = hn


def _rwkv_scan(r, k, v, kk, b, lwf, lwb):
    bsz, s, _ = r.shape
    c = RWKV_CHUNK
    rows = min(RWKV_SUBCHUNKS * c, s)
    nch = s // rows
    fs = pl.BlockSpec((1, rows, GROUP_W), lambda i, n: (i, n, 0))
    bs = pl.BlockSpec((1, rows, GROUP_W), lambda i, n: (i, nch - 1 - n, 0))
    osh = jax.ShapeDtypeStruct((bsz, s, GROUP_W), BF16)
    return pl.pallas_call(
        _rwkv_scan_kernel,
        name="rwkv_scan",
        grid=(bsz, nch),
        in_specs=[fs] * 6 + [bs] * 6,
        out_specs=[fs, bs],
        out_shape=[osh, osh],
        scratch_shapes=[pltpu.VMEM((2, GROUP_W // LANE, LANE, LANE), F32)],
        compiler_params=_cparams(("parallel", "arbitrary")),
    )(r, k, v, kk, b, lwf, r, k, v, kk, b, lwb)


def _rwkv_post_kernel(yf_ref, yb_ref, r_ref, k_ref, v_ref, g_ref, rk_ref, lnw_ref, lnb_ref, bd_ref, o_ref):
    bd = bd_ref[...]
    f32 = lambda ref: ref[...].astype(F32)
    y = f32(yf_ref) + f32(yb_ref) + _seg_sum(f32(r_ref) * f32(k_ref) * rk_ref[...], bd) * f32(v_ref)
    mean = _seg_sum(y, bd) * (1.0 / RWKV_HD)
    yc = y - mean
    var = _seg_sum(yc * yc, bd) * (1.0 / RWKV_HD)
    y = yc * lax.rsqrt(var + RWKV_LN_EPS) * lnw_ref[...] + lnb_ref[...]
    o_ref[...] = (y * f32(g_ref)).astype(BF16)


def _rwkv_post(yf, yb, r, k, v, g, rk, lnw, lnb, bd):
    t = yf.shape[0]
    tm = min(RWKV_POST_ROWS, t)
    big = pl.BlockSpec((tm, GROUP_W), lambda i: (i, 0))
    vec = pl.BlockSpec((1, GROUP_W), lambda i: (0, 0))
    return pl.pallas_call(
        _rwkv_post_kernel,
        name="rwkv_post",
        grid=(t // tm,),
        in_specs=[big] * 6 + [vec] * 3 + [pl.BlockSpec((GROUP_W, GROUP_W), lambda i: (0, 0))],
        out_specs=big,
        out_shape=jax.ShapeDtypeStruct((t, GROUP_W), BF16),
        compiler_params=_cparams(("parallel",)),
    )(yf, yb, r, k, v, g, rk, lnw, lnb, bd)


def _outproj_kernel(ya_ref, yb_ref, yc_ref, yd_ref, ga_ref, gb_ref, w_ref, x_ref, o_ref):
    ya = ya_ref[...].astype(F32)
    ya = ya * lax.rsqrt(jnp.mean(ya * ya, axis=-1, keepdims=True) + EPS) * ga_ref[...]
    yb = yb_ref[...].astype(F32)
    yb = yb * lax.rsqrt(jnp.mean(yb * yb, axis=-1, keepdims=True) + EPS) * gb_ref[...]
    cat = jnp.concatenate([ya.astype(BF16), yb.astype(BF16), yc_ref[...], yd_ref[...]], axis=-1)
    o_ref[...] = x_ref[...] + jnp.dot(cat, w_ref[...], preferred_element_type=F32)


def _outproj(ya, yb, yc, yd, ga, gb, w, l, x2d):
    t, d = x2d.shape
    tm = min(OUTPROJ_ROWS, t)
    ysp = pl.BlockSpec((tm, GROUP_W), lambda i: (i, 0))
    gsp = pl.BlockSpec((1, GROUP_W), lambda i: (0, 0))
    return pl.pallas_call(
        _outproj_kernel,
        name="outproj",
        grid=(t // tm,),
        in_specs=[ysp, ysp, ysp, ysp, gsp, gsp,
                  pl.BlockSpec((None, d, d), lambda i: (l, 0, 0)),
                  pl.BlockSpec((tm, d), lambda i: (i, 0))],
        out_specs=pl.BlockSpec((tm, d), lambda i: (i, 0)),
        out_shape=jax.ShapeDtypeStruct((t, d), F32),
        compiler_params=_cparams(("parallel",)),
    )(ya, yb, yc, yd, ga, gb, w, x2d)


def _ffn_kernel(x_ref, g_ref, wg_ref, wu_ref, wd_ref, o_ref, h_ref):
    @pl.when(pl.program_id(1) == 0)
    def _():
        x = x_ref[...]
        ms = jnp.mean(x * x, axis=-1, keepdims=True)
        h_ref[...] = (x * lax.rsqrt(ms + EPS) * g_ref[...]).astype(BF16)
        o_ref[...] = x

    h = h_ref[...]
    gt = jnp.dot(h, wg_ref[...], preferred_element_type=F32)
    up = jnp.dot(h, wu_ref[...], preferred_element_type=F32)
    act = (gt * _sigmoid(gt) * up).astype(BF16)
    o_ref[...] += jnp.dot(act, wd_ref[...], preferred_element_type=F32)


def _ffn(x2d, gain, wg, wu, wd, l):
    t, d = x2d.shape
    f = wg.shape[2]
    tm = min(FFN_ROWS, t)
    tf = FFN_COLS
    return pl.pallas_call(
        _ffn_kernel,
        name="ffn",
        grid=(t // tm, f // tf),
        in_specs=[pl.BlockSpec((tm, d), lambda i, j: (i, 0)),
                  pl.BlockSpec((1, d), lambda i, j: (0, 0)),
                  pl.BlockSpec((None, d, tf), lambda i, j: (l, 0, j)),
                  pl.BlockSpec((None, d, tf), lambda i, j: (l, 0, j)),
                  pl.BlockSpec((None, tf, d), lambda i, j: (l, j, 0))],
        out_specs=pl.BlockSpec((tm, d), lambda i, j: (i, 0)),
        out_shape=jax.ShapeDtypeStruct((t, d), F32),
        scratch_shapes=[pltpu.VMEM((tm, d), BF16)],
        compiler_params=_cparams(("parallel", "arbitrary")),
    )(x2d, gain, wg, wu, wd)


def _block_diag_ones(n, seg):
    i = jnp.arange(n) // seg
    return (i[:, None] == i[None, :]).astype(BF16)


def _rope_tables(n_tok, head_dim, n_copies):
    half = head_dim // 2
    quarter = half // 2
    rows = (jnp.arange(n_tok, dtype=jnp.int32) // GRID_W).astype(F32)
    cols = (jnp.arange(n_tok, dtype=jnp.int32) % GRID_W).astype(F32)
    inv = 1.0 / (ROPE_BASE ** (jnp.arange(0, half, 2, dtype=F32) / half))
    ang_r = rows[:, None] * inv[None, :]
    ang_c = cols[:, None] * inv[None, :]
    cos = jnp.concatenate([jnp.cos(ang_r)] * 2 + [jnp.cos(ang_c)] * 2, axis=-1)
    sin = jnp.concatenate([-jnp.sin(ang_r), jnp.sin(ang_r), -jnp.sin(ang_c), jnp.sin(ang_c)], axis=-1)
    del quarter
    return jnp.tile(cos, (1, n_copies)), jnp.tile(sin, (1, n_copies))


def _pair_block_diag(w):
    lead = w.shape[:-3]
    w = w.reshape(lead + (4, 2, LRU_BD, LRU_BD))
    z = jnp.zeros_like(w[..., 0, :, :])
    top = jnp.concatenate([w[..., 0, :, :], z], axis=-1)
    bot = jnp.concatenate([z, w[..., 1, :, :]], axis=-1)
    return jnp.concatenate([top, bot], axis=-2)


def _lru_gate_weights(wa, wx):
    pa, px = _pair_block_diag(wa), _pair_block_diag(wx)
    return jnp.concatenate([pa[0], px[0], pa[1], px[1]], axis=-1)


def _split_w(w):
    hi = w.astype(BF16)
    return hi, (w - hi.astype(F32)).astype(BF16)


def _prep_layer(p):
    mu_pad = jnp.pad(p["rwkv_mu"], (0, D_COLS_PAD - p["rwkv_mu"].shape[0]))[None, :]
    w2 = p["rwkv_w2"]
    zl = jnp.zeros_like(w2[0])
    w2cat = jnp.concatenate([jnp.concatenate([w2[0], zl], axis=1),
                             jnp.concatenate([zl, w2[1]], axis=1)], axis=0)
    zrow = jnp.zeros_like(p["rwkv_a2"])
    a2pad = jnp.concatenate([p["rwkv_a2"], zrow], axis=0)
    g2pad = jnp.concatenate([zrow, p["rwkv_g2"], zrow], axis=0)
    out = {
        "norm_mix": p["norm_mix"][None, :],
        "gq": jnp.tile(p["attn_q_norm"] * (ATT_HD ** -0.5), ATT_HQ)[None, :],
        "gk": jnp.tile(p["attn_k_norm"], ATT_HKV)[None, :],
        "ga": p["attn_out_norm"][None, :],
        "cw": p["lru_conv_w"], "cb": p["lru_conv_b"][None, :],
        "w_gates": _split_w(_lru_gate_weights(p["lru_wa"], p["lru_wx"])),
        "ba": p["lru_ba"], "bx": p["lru_bx"], "lam": p["lru_lambda"],
        "gb": p["lru_out_norm"][None, :],
        "ret_gain": p["ret_norm"][None, :],
        "mu": mu_pad,
        "w2": _split_w(w2cat), "a2": _split_w(a2pad), "g2": _split_w(g2pad),
        "w0": p["rwkv_w0"].reshape(1, 2 * GROUP_W), "a0": p["rwkv_a0"][None, :],
        "k_k": p["rwkv_k_k"][None, :], "k_a": p["rwkv_k_a"][None, :],
        "r_k": p["rwkv_r_k"].reshape(1, GROUP_W),
        "ln_w": p["rwkv_ln_w"][None, :], "ln_b": p["rwkv_ln_b"][None, :],
        "norm_ffn": p["norm_ffn"][None, :],
    }
    return out


def _prep_big(w_in, w_out, w_gate, w_up, w_down):
    w_z = jnp.pad(w_in.astype(BF16), ((0, 0), (0, 0), (0, Z_COLS - w_in.shape[2])))
    return {"w_z": w_z, "w_out": w_out.astype(BF16), "w_gate": w_gate.astype(BF16),
            "w_up": w_up.astype(BF16), "w_down": w_down.astype(BF16)}


def _layer(x, l, lp, big, consts):
    bsz, s, d = x.shape
    t = bsz * s
    x2d = x.reshape(t, d)
    z = _inproj(x2d, lp["norm_mix"], big["w_z"], l).reshape(bsz, s, Z_COLS)

    ya = _attention(z, consts["att_tabs"], lp["gq"], lp["gk"], consts["bd512"], consts["bd128"])
    yb = _rglru(z, lp["cw"], lp["cb"], lp["w_gates"][0], lp["w_gates"][1], lp["ba"], lp["bx"], lp["lam"])
    yc = _retention(z, consts["ret_cos"], consts["ret_sin"], consts["ret_lg"], lp["ret_gain"])
    r, k, v, kk, b, lwf, lwb, g = _rwkv_prep(
        z, lp["mu"], lp["w2"][0], lp["w2"][1], lp["a2"][0], lp["a2"][1], lp["g2"][0], lp["g2"][1],
        lp["w0"], lp["a0"], lp["k_k"], lp["k_a"], consts["bd512"])
    yf, ybw = _rwkv_scan(r, k, v, kk, b, lwf, lwb)
    f2 = lambda a: a.reshape(t, GROUP_W)
    yd = _rwkv_post(f2(yf), f2(ybw), f2(r), f2(k), f2(v), f2(g), lp["r_k"], lp["ln_w"], lp["ln_b"],
                    consts["bd512"])
    x2d = _outproj(f2(ya), f2(yb), f2(yc), yd, lp["ga"], lp["gb"], big["w_out"], l, x2d)
    x2d = _ffn(x2d, lp["norm_ffn"], big["w_gate"], big["w_up"], big["w_down"], l)
    return x2d.reshape(bsz, s, d)


def _consts(s):
    cq, sq = _rope_tables(s, ATT_HD, ATT_HQ)
    ck, sk = _rope_tables(s, ATT_HD, ATT_HKV)
    rc, rs = _rope_tables(s, RET_HD, 1)
    log_g = jnp.log1p(-jnp.exp2(-5.0 - jnp.arange(RET_H, dtype=F32)))
    return {
        "att_tabs": (cq, sq, ck, sk),
        "ret_cos": rc, "ret_sin": rs,
        "ret_lg": jnp.broadcast_to(log_g[:, None, None], (RET_H, 1, LANE)),
        "bd512": _block_diag_ones(GROUP_W, 64),
        "bd128": _block_diag_ones(LANE, 64),
    }


def _trunk(x, layers, big, consts):
    for l, lp in enumerate(layers):
        x = _layer(x, l, lp, big, consts)
    return x


def kernel(x_prompt, x_sample, norm_mix, w_in, attn_q_norm, attn_k_norm, attn_out_norm, lru_conv_w, lru_conv_b, lru_wa, lru_ba, lru_wx, lru_bx, lru_lambda, lru_out_norm, ret_norm, rwkv_mu, rwkv_w0, rwkv_w2, rwkv_a0, rwkv_a2, rwkv_g2, rwkv_k_k, rwkv_k_a, rwkv_r_k, rwkv_ln_w, rwkv_ln_b, w_out, norm_ffn, w_gate, w_up, w_down):
    params = {
        "norm_mix": norm_mix,
        "attn_q_norm": attn_q_norm, "attn_k_norm": attn_k_norm, "attn_out_norm": attn_out_norm,
        "lru_conv_w": lru_conv_w, "lru_conv_b": lru_conv_b, "lru_wa": lru_wa, "lru_ba": lru_ba,
        "lru_wx": lru_wx, "lru_bx": lru_bx, "lru_lambda": lru_lambda, "lru_out_norm": lru_out_norm,
        "ret_norm": ret_norm,
        "rwkv_mu": rwkv_mu, "rwkv_w0": rwkv_w0, "rwkv_w2": rwkv_w2, "rwkv_a0": rwkv_a0,
        "rwkv_a2": rwkv_a2, "rwkv_g2": rwkv_g2, "rwkv_k_k": rwkv_k_k, "rwkv_k_a": rwkv_k_a,
        "rwkv_r_k": rwkv_r_k, "rwkv_ln_w": rwkv_ln_w, "rwkv_ln_b": rwkv_ln_b,
        "norm_ffn": norm_ffn,
    }
    depth = w_in.shape[0]
    layers = [_prep_layer({name: arr[l] for name, arr in params.items()}) for l in range(depth)]
    big = _prep_big(w_in, w_out, w_gate, w_up, w_down)
    y_prompt = _trunk(x_prompt, layers, big, _consts(x_prompt.shape[1]))
    y_sample = _trunk(x_sample, layers, big, _consts(x_sample.shape[1]))
    return (y_prompt, y_sample)
```
